```python
import math
import jax, jax.numpy as jnp
from jax import lax
import numpy as np

D_MODEL = 2048
BATCH = 1
SEQ = 8192
DEPTH = 2

HEAD_DIM = 128
N_HEADS_DIFF = 4
DIFF_QK_DIM = 64
DIFF_V_DIM = 2 * DIFF_QK_DIM
DIFF_LAMBDA_STD = 0.1
N_HEADS_FOX = 6
FOX_HEAD_DIM = HEAD_DIM
FORGET_BIAS_MIN = 1.0
FORGET_BIAS_MAX = 4.0
N_HEADS_MLA = 6
MLA_Q_RANK = 512
MLA_KV_RANK = 256
MLA_NOPE_DIM = 128
MLA_ROPE_DIM = 64
MLA_QK_DIM = MLA_ROPE_DIM + MLA_NOPE_DIM
MLA_V_DIM = 128
MIX_WIDTH = N_HEADS_DIFF * DIFF_V_DIM + N_HEADS_FOX * FOX_HEAD_DIM + N_HEADS_MLA * MLA_V_DIM
D_FF = 4 * D_MODEL
ROPE_THETA = 500000.0
PARTIAL_ROT_DIM = DIFF_QK_DIM // 4
BLOCK_Q = 128
EPS = 1e-6

DIFF_Q_COLS = N_HEADS_DIFF * 2 * DIFF_QK_DIM
DIFF_V_COLS = N_HEADS_DIFF * DIFF_V_DIM
FOX_COLS = N_HEADS_FOX * FOX_HEAD_DIM
MLA_KV_DOWN_COLS = MLA_KV_RANK + MLA_ROPE_DIM
IN_SECTIONS = (DIFF_Q_COLS, DIFF_Q_COLS, DIFF_V_COLS,
               FOX_COLS, FOX_COLS, FOX_COLS, N_HEADS_FOX,
               MLA_Q_RANK, MLA_KV_DOWN_COLS)
IN_COLS = 2 * DIFF_Q_COLS + DIFF_V_COLS + 3 * FOX_COLS + N_HEADS_FOX + MLA_Q_RANK + MLA_KV_DOWN_COLS

kernel_name = "hymba_style_diff_fox_mla_trunk"


def rms_norm(x, g):
    xf = x.astype(jnp.float32)
    y = xf * lax.rsqrt(jnp.mean(xf * xf, axis=-1, keepdims=True) + EPS)
    return (y * g.astype(jnp.float32)).astype(x.dtype)


def rope_tables(seq_len, rot_dim):
    half = rot_dim // 2
    inv_freq = ROPE_THETA ** (-jnp.arange(half, dtype=jnp.float32) / half)
    ang = jnp.arange(seq_len, dtype=jnp.float32)[:, None] * inv_freq[None, :]
    return jnp.cos(ang), jnp.sin(ang)


def apply_rope(x, cos, sin):
    half = cos.shape[-1]
    rot_dim = 2 * half
    c = cos[None, :, None, :]
    s = sin[None, :, None, :]
    xr = x[..., :rot_dim].astype(jnp.float32)
    x1, x2 = xr[..., :half], xr[..., half:]
    rotated = jnp.concatenate([x1 * c - x2 * s, x2 * c + x1 * s], axis=-1).astype(x.dtype)
    return jnp.concatenate([rotated, x[..., rot_dim:]], axis=-1)


def causal_block_attention(q, k, v, scale, log_decay_cum=None):
    b, s, h, dk = q.shape
    nb = s // BLOCK_Q
    q_blocks = jnp.moveaxis(q.reshape(b, nb, BLOCK_Q, h, dk), 1, 0)
    if log_decay_cum is None:
        c_blocks, c_keys = None, None
    else:
        c_blocks = jnp.moveaxis(log_decay_cum.reshape(b, nb, BLOCK_Q, h), 1, 0)
        c_keys = jnp.transpose(log_decay_cum, (0, 2, 1))[:, :, None, :]
    k_pos = jnp.arange(s)

    def attend(args):
        i, qi, ci = args
        logits = jnp.einsum("bqhd,bkhd->bhqk", qi, k, preferred_element_type=jnp.float32) * scale
        if ci is not None:
            logits = logits + jnp.transpose(ci, (0, 2, 1))[..., None] - c_keys
        q_pos = i * BLOCK_Q + jnp.arange(BLOCK_Q)
        logits = jnp.where(k_pos[None, :] <= q_pos[:, None], logits, -jnp.inf)
        p = jax.nn.softmax(logits, axis=-1).astype(v.dtype)
        return jnp.einsum("bhqk,bkhv->bqhv", p, v)

    out = lax.map(attend, (jnp.arange(nb), q_blocks, c_blocks))
    return jnp.moveaxis(out, 0, 1).reshape(b, s, h, v.shape[-1])


def diff_attention(q, k, v, cos, sin, q_norm, k_norm, lq1, lk1, lq2, lk2, subln, layer_idx):
    b, s, _ = q.shape
    q = rms_norm(q.reshape(b, s, 2 * N_HEADS_DIFF, DIFF_QK_DIM), q_norm)
    k = rms_norm(k.reshape(b, s, 2 * N_HEADS_DIFF, DIFF_QK_DIM), k_norm)
    q = apply_rope(q, cos, sin).reshape(b, s, N_HEADS_DIFF, 2, DIFF_QK_DIM)
    k = apply_rope(k, cos, sin).reshape(b, s, N_HEADS_DIFF, 2, DIFF_QK_DIM)
    v = v.reshape(b, s, N_HEADS_DIFF, DIFF_V_DIM)
    lambda_init = 0.8 - 0.6 * math.exp(-0.3 * layer_idx)
    lam = (jnp.exp(jnp.sum(lq1.astype(jnp.float32) * lk1.astype(jnp.float32)))
           - jnp.exp(jnp.sum(lq2.astype(jnp.float32) * lk2.astype(jnp.float32)))
           + lambda_init)
    scale = DIFF_QK_DIM ** -0.5
    o1 = causal_block_attention(q[:, :, :, 0], k[:, :, :, 0], v, scale)
    o2 = causal_block_attention(q[:, :, :, 1], k[:, :, :, 1], v, scale)
    o = o1.astype(jnp.float32) - lam * o2.astype(jnp.float32)
    o = rms_norm(o, subln) * (1.0 - lambda_init)
    return o.reshape(b, s, N_HEADS_DIFF * DIFF_V_DIM).astype(v.dtype)


def forgetting_attention(q, k, v, f_logit, f_bias, q_norm, k_norm):
    b, s, _ = q.shape
    q = rms_norm(q.reshape(b, s, N_HEADS_FOX, FOX_HEAD_DIM), q_norm)
    k = rms_norm(k.reshape(b, s, N_HEADS_FOX, FOX_HEAD_DIM), k_norm)
    v = v.reshape(b, s, N_HEADS_FOX, FOX_HEAD_DIM)
    log_f = jax.nn.log_sigmoid(f_logit.astype(jnp.float32) + f_bias.astype(jnp.float32))
    cum = jnp.cumsum(log_f, axis=1)
    o = causal_block_attention(q, k, v, FOX_HEAD_DIM ** -0.5, cum)
    return o.reshape(b, s, N_HEADS_FOX * FOX_HEAD_DIM)


def latent_attention(q_down, kv_down, cos, sin, q_a_norm, kv_a_norm, w_q_up, w_kv_up, q_norm, k_norm):
    b, s, _ = q_down.shape
    q = (rms_norm(q_down, q_a_norm) @ w_q_up).reshape(b, s, N_HEADS_MLA, MLA_QK_DIM)
    c_kv = rms_norm(kv_down[..., :MLA_KV_RANK], kv_a_norm)
    k_rope = kv_down[..., MLA_KV_RANK:].reshape(b, s, 1, MLA_ROPE_DIM)
    kv = (c_kv @ w_kv_up).reshape(b, s, N_HEADS_MLA, MLA_NOPE_DIM + MLA_V_DIM)
    k_nope, v = kv[..., :MLA_NOPE_DIM], kv[..., MLA_NOPE_DIM:]
    q_rope = apply_rope(rms_norm(q[..., :MLA_ROPE_DIM], q_norm[:MLA_ROPE_DIM]), cos, sin)
    q_nope = rms_norm(q[..., MLA_ROPE_DIM:], q_norm[MLA_ROPE_DIM:])
    k_rope = apply_rope(rms_norm(k_rope, k_norm[:MLA_ROPE_DIM]), cos, sin)
    k_nope = rms_norm(k_nope, k_norm[MLA_ROPE_DIM:])
    q = jnp.concatenate([q_rope, q_nope], axis=-1)
    k = jnp.concatenate([jnp.broadcast_to(k_rope, (b, s, N_HEADS_MLA, MLA_ROPE_DIM)), k_nope], axis=-1)
    o = causal_block_attention(q, k, v, MLA_QK_DIM ** -0.5)
    return o.reshape(b, s, N_HEADS_MLA * MLA_V_DIM)


def split_columns(proj):
    parts, start = [], 0
    for width in IN_SECTIONS:
        parts.append(proj[..., start:start + width])
        start += width
    return parts


def setup_inputs(seed: int = 0) -> dict:
    key = jax.random.key(seed)
    ks = jax.random.split(key, 24)

    def normal(k, shape, scale):
        return scale * jax.random.normal(k, shape, jnp.float32)

    def gain(k, dim):
        return 1.0 + normal(k, (DEPTH, dim), 0.05)

    L = DEPTH
    return {
        "x": normal(ks[0], (BATCH, SEQ, D_MODEL), 1.0),
        "norm_mix": gain(ks[1], D_MODEL),
        "w_in": normal(ks[2], (L, D_MODEL, IN_COLS), D_MODEL ** -0.5),
        "diff_q_norm": gain(ks[3], DIFF_QK_DIM),
        "diff_k_norm": gain(ks[4], DIFF_QK_DIM),
        "diff_lambda_q1": normal(ks[5], (L, DIFF_QK_DIM), DIFF_LAMBDA_STD),
        "diff_lambda_k1": normal(ks[6], (L, DIFF_QK_DIM), DIFF_LAMBDA_STD),
        "diff_lambda_q2": normal(ks[7], (L, DIFF_QK_DIM), DIFF_LAMBDA_STD),
        "diff_lambda_k2": normal(ks[8], (L, DIFF_QK_DIM), DIFF_LAMBDA_STD),
        "diff_subln": gain(ks[9], DIFF_V_DIM),
        "fox_q_norm": gain(ks[10], FOX_HEAD_DIM),
        "fox_k_norm": gain(ks[11], FOX_HEAD_DIM),
        "fox_forget_bias": jax.random.uniform(ks[12], (L, N_HEADS_FOX), jnp.float32,
                                              FORGET_BIAS_MIN, FORGET_BIAS_MAX),
        "mla_q_a_norm": gain(ks[13], MLA_Q_RANK),
        "mla_kv_a_norm": gain(ks[14], MLA_KV_RANK),
        "mla_w_q_up": normal(ks[15], (L, MLA_Q_RANK, N_HEADS_MLA * MLA_QK_DIM), MLA_Q_RANK ** -0.5),
        "mla_w_kv_up": normal(ks[16], (L, MLA_KV_RANK, N_HEADS_MLA * (MLA_NOPE_DIM + MLA_V_DIM)),
                              MLA_KV_RANK ** -0.5),
        "mla_q_norm": gain(ks[17], MLA_QK_DIM),
        "mla_k_norm": gain(ks[18], MLA_QK_DIM),
        "w_out": normal(ks[19], (L, MIX_WIDTH, D_MODEL), MIX_WIDTH ** -0.5),
        "norm_ffn": gain(ks[20], D_MODEL),
        "w_ff_up": normal(ks[21], (L, D_MODEL, D_FF), D_MODEL ** -0.5),
        "w_ff_down": normal(ks[22], (L, D_FF, D_MODEL), D_FF ** -0.5),
    }


def reference(x, norm_mix, w_in, diff_q_norm, diff_k_norm, diff_lambda_q1, diff_lambda_k1,
              diff_lambda_q2, diff_lambda_k2, diff_subln, fox_q_norm, fox_k_norm, fox_forget_bias,
              mla_q_a_norm, mla_kv_a_norm, mla_w_q_up, mla_w_kv_up, mla_q_norm, mla_k_norm,
              w_out, norm_ffn, w_ff_up, w_ff_down):
    seq = x.shape[1]
    cos_p, sin_p = rope_tables(seq, PARTIAL_ROT_DIM)
    cos_m, sin_m = rope_tables(seq, MLA_ROPE_DIM)
    for l in range(DEPTH):
        h = rms_norm(x, norm_mix[l])
        proj = h @ w_in[l]
        dq, dk, dv, fq, fk, fv, ff, mq, mkv = split_columns(proj)
        o_a = diff_attention(dq, dk, dv, cos_p, sin_p, diff_q_norm[l], diff_k_norm[l],
                             diff_lambda_q1[l], diff_lambda_k1[l], diff_lambda_q2[l],
                             diff_lambda_k2[l], diff_subln[l], l)
        o_b = forgetting_attention(fq, fk, fv, ff, fox_forget_bias[l], fox_q_norm[l], fox_k_norm[l])
        o_c = latent_attention(mq, mkv, cos_m, sin_m, mla_q_a_norm[l], mla_kv_a_norm[l],
                               mla_w_q_up[l], mla_w_kv_up[l], mla_q_norm[l], mla_k_norm[l])
        mixed = jnp.concatenate([o_a, o_b.astype(o_a.dtype), o_c.astype(o_a.dtype)], axis=-1)
        x = x + (mixed @ w_out[l]).astype(x.dtype)
        h = rms_norm(x, norm_ffn[l])
        x = x + (jnp.square(jax.nn.relu(h @ w_ff_up[l])) @ w_ff_down[l]).astype(x.dtype)
    return x
```

```python
import functools
import math

import jax
import jax.numpy as jnp
from jax import lax
from jax.experimental import pallas as pl
from jax.experimental.pallas import tpu as pltpu

D_MODEL = 2048
DEPTH = 2
N_HEADS_DIFF = 4
DIFF_QK_DIM = 64
DIFF_V_DIM = 128
N_HEADS_FOX = 6
FOX_HEAD_DIM = 128
N_HEADS_MLA = 6
MLA_Q_RANK = 512
MLA_KV_RANK = 256
MLA_NOPE_DIM = 128
MLA_ROPE_DIM = 64
MLA_QK_DIM = MLA_ROPE_DIM + MLA_NOPE_DIM
MLA_V_DIM = 128
D_FF = 4 * D_MODEL
ROPE_THETA = 500000.0
PARTIAL_ROT_DIM = DIFF_QK_DIM // 4
EPS = 1e-6

DIFF_Q_COLS = N_HEADS_DIFF * 2 * DIFF_QK_DIM
DIFF_V_COLS = N_HEADS_DIFF * DIFF_V_DIM
FOX_COLS = N_HEADS_FOX * FOX_HEAD_DIM
MLA_V_COLS = N_HEADS_MLA * MLA_V_DIM
IN_SECTIONS = (DIFF_Q_COLS, DIFF_Q_COLS, DIFF_V_COLS, FOX_COLS, FOX_COLS, FOX_COLS,
               N_HEADS_FOX, MLA_Q_RANK, MLA_KV_RANK + MLA_ROPE_DIM)

LANES = 128
MLA_PAD_DIM = 2 * LANES
VMEM_LIMIT_BYTES = 56 * 1024 * 1024

OFF_DQ = 0
OFF_DK = OFF_DQ + DIFF_Q_COLS
OFF_DV = OFF_DK + DIFF_Q_COLS
OFF_FQ = OFF_DV + DIFF_V_COLS
OFF_FK = OFF_FQ + FOX_COLS
OFF_FV = OFF_FK + FOX_COLS
OFF_MQ = OFF_FV + FOX_COLS
OFF_CKV = OFF_MQ + MLA_Q_RANK
OFF_KR = OFF_CKV + MLA_KV_RANK
OFF_FF = OFF_KR + LANES
PROJ_COLS = OFF_FF + LANES

PV_G_DQ = 0
PV_G_DK = PV_G_DQ + LANES
PV_G_FQ = PV_G_DK + LANES
PV_G_FK = PV_G_FQ + LANES
PV_F_BIAS = PV_G_FK + LANES
PV_G_QA = PV_F_BIAS + LANES
PV_G_KVA = PV_G_QA + MLA_Q_RANK
PV_G_QM = PV_G_KVA + MLA_KV_RANK
PV_G_KM = PV_G_QM + MLA_PAD_DIM
PV_LEN = PV_G_KM + MLA_PAD_DIM

NEG_BIG = -1e30

BF16 = jnp.bfloat16
F32 = jnp.float32


def _params(*semantics):
    return pltpu.CompilerParams(dimension_semantics=semantics,
                                vmem_limit_bytes=VMEM_LIMIT_BYTES)


def _in_proj_kernel(x_ref, g_ref, w_ref, o_ref):
    x = x_ref[...]
    inv = lax.rsqrt(jnp.mean(x * x, axis=-1, keepdims=True) + EPS)
    h = (x * inv * g_ref[...]).astype(BF16)
    o_ref[...] = jnp.dot(h, w_ref[...], preferred_element_type=F32)


def _in_proj(x, g, w, *, tm, tn):
    s, d = x.shape
    n = w.shape[1]
    return pl.pallas_call(
        _in_proj_kernel,
        grid=(n // tn, s // tm),
        in_specs=[pl.BlockSpec((tm, d), lambda j, i: (i, 0)),
                  pl.BlockSpec((1, d), lambda j, i: (0, 0)),
                  pl.BlockSpec((d, tn), lambda j, i: (0, j))],
        out_specs=pl.BlockSpec((tm, tn), lambda j, i: (i, j)),
        out_shape=jax.ShapeDtypeStruct((s, n), F32),
        compiler_params=_params("arbitrary", "arbitrary"),
        name="in_proj",
    )(x, g, w)


def _rms(x, n_valid):
    ss = jnp.sum(x * x, axis=-1, keepdims=True)
    return x * lax.rsqrt(ss * (1.0 / n_valid) + EPS)


def _rms_two_halves(x):
    lo = lax.broadcasted_iota(jnp.int32, x.shape, 1) < DIFF_QK_DIM
    x2 = x * x
    s_lo = jnp.sum(jnp.where(lo, x2, 0.0), axis=-1, keepdims=True)
    s_hi = jnp.sum(jnp.where(lo, 0.0, x2), axis=-1, keepdims=True)
    return x * lax.rsqrt(jnp.where(lo, s_lo, s_hi) * (1.0 / DIFF_QK_DIM) + EPS)


def _rope(x, c, s_up, s_dn, half):
    return x * c + pltpu.roll(x, LANES - half, 1) * s_up + pltpu.roll(x, half, 1) * s_dn


def _prep_kernel(proj_ref, tab_ref, pv_ref, wq_ref, wkv_ref,
                 qd_ref, kd_ref, vd_ref, qf_ref, kf_ref, vf_ref, cq_ref, ckt_ref,
                 qm_ref, km_ref, vm_ref, carry_ref, *, tm):
    @pl.when(pl.program_id(0) == 0)
    def _():
        carry_ref[...] = jnp.zeros_like(carry_ref)

    def pv(off, n):
        return pv_ref[:, off:off + n]

    def lanes(ref, off, n=LANES):
        return ref[:, off:off + n]

    cd, sd_up, sd_dn, cm, sm_up, sm_dn = (lanes(tab_ref, k * LANES) for k in range(6))
    half_d = PARTIAL_ROT_DIM // 2
    half_m = MLA_ROPE_DIM // 2

    g_dq, g_dk = pv(PV_G_DQ, LANES), pv(PV_G_DK, LANES)
    for b in range(N_HEADS_DIFF):
        q = _rope(_rms_two_halves(lanes(proj_ref, OFF_DQ + b * LANES)) * g_dq, cd, sd_up, sd_dn, half_d)
        qd_ref[:, b * LANES:(b + 1) * LANES] = (q * DIFF_QK_DIM ** -0.5).astype(BF16)
        k = _rope(_rms_two_halves(lanes(proj_ref, OFF_DK + b * LANES)) * g_dk, cd, sd_up, sd_dn, half_d)
        kd_ref[:, b * LANES:(b + 1) * LANES] = k.astype(BF16)
    vd_ref[...] = lanes(proj_ref, OFF_DV, DIFF_V_COLS).astype(BF16)

    g_fq, g_fk = pv(PV_G_FQ, LANES), pv(PV_G_FK, LANES)
    for h in range(N_HEADS_FOX):
        q = _rms(lanes(proj_ref, OFF_FQ + h * LANES), FOX_HEAD_DIM) * g_fq
        qf_ref[:, h * LANES:(h + 1) * LANES] = (q * FOX_HEAD_DIM ** -0.5).astype(BF16)
        k = _rms(lanes(proj_ref, OFF_FK + h * LANES), FOX_HEAD_DIM) * g_fk
        kf_ref[:, h * LANES:(h + 1) * LANES] = k.astype(BF16)
    vf_ref[...] = lanes(proj_ref, OFF_FV, FOX_COLS).astype(BF16)

    z = lanes(proj_ref, OFF_FF) + pv(PV_F_BIAS, LANES)
    c = jnp.minimum(z, 0.0) - jnp.log1p(jnp.exp(-jnp.abs(z)))
    row = lax.broadcasted_iota(jnp.int32, c.shape, 0)
    shift = 1
    while shift < tm:
        c = c + jnp.where(row >= shift, pltpu.roll(c, shift, 0), 0.0)
        shift *= 2
    c = c + carry_ref[...]
    carry_ref[...] = c[tm - 1:tm, :]
    for h in range(N_HEADS_FOX):
        cq_ref[:, h * LANES:(h + 1) * LANES] = jnp.broadcast_to(c[:, h:h + 1], (tm, LANES))
    ckt_ref[...] = c.T[0:8, :]

    q_lat = (_rms(lanes(proj_ref, OFF_MQ, MLA_Q_RANK), MLA_Q_RANK) * pv(PV_G_QA, MLA_Q_RANK)).astype(BF16)
    q_up = jnp.dot(q_lat, wq_ref[...], preferred_element_type=F32)
    c_kv = (_rms(lanes(proj_ref, OFF_CKV, MLA_KV_RANK), MLA_KV_RANK) * pv(PV_G_KVA, MLA_KV_RANK)).astype(BF16)
    kv_up = jnp.dot(c_kv, wkv_ref[...], preferred_element_type=F32)
    g_qn, g_qr = pv(PV_G_QM, LANES), pv(PV_G_QM + LANES, LANES)
    g_kn, g_kr = pv(PV_G_KM, LANES), pv(PV_G_KM + LANES, LANES)
    k_rope = _rope(_rms(lanes(proj_ref, OFF_KR), MLA_ROPE_DIM) * g_kr, cm, sm_up, sm_dn, half_m).astype(BF16)
    mla_scale = MLA_QK_DIM ** -0.5
    for h in range(N_HEADS_MLA):
        base = h * MLA_PAD_DIM
        q_nope = _rms(q_up[:, base:base + LANES], MLA_NOPE_DIM) * g_qn
        q_rope = _rope(_rms(q_up[:, base + LANES:base + 2 * LANES], MLA_ROPE_DIM) * g_qr,
                       cm, sm_up, sm_dn, half_m)
        qm_ref[:, base:base + LANES] = (q_nope * mla_scale).astype(BF16)
        qm_ref[:, base + LANES:base + 2 * LANES] = (q_rope * mla_scale).astype(BF16)
        k_nope = _rms(kv_up[:, h * LANES:(h + 1) * LANES], MLA_NOPE_DIM) * g_kn
        km_ref[:, base:base + LANES] = k_nope.astype(BF16)
        km_ref[:, base + LANES:base + 2 * LANES] = k_rope
    vm_ref[...] = kv_up[:, N_HEADS_MLA * MLA_NOPE_DIM:].astype(BF16)


def _prep(proj, tab, pvec, wq, wkv, *, tm):
    s = proj.shape[0]
    row_blk = lambda n: pl.BlockSpec((tm, n), lambda i: (i, 0))
    full = lambda a: pl.BlockSpec(a.shape, lambda i: (0, 0))
    mla_cols = N_HEADS_MLA * MLA_PAD_DIM
    out_cols = (DIFF_Q_COLS, DIFF_Q_COLS, DIFF_V_COLS, FOX_COLS, FOX_COLS, FOX_COLS)
    out_shape = ([jax.ShapeDtypeStruct((s, n), BF16) for n in out_cols]
                 + [jax.ShapeDtypeStruct((s, FOX_COLS), F32), jax.ShapeDtypeStruct((8, s), F32)]
                 + [jax.ShapeDtypeStruct((s, n), BF16) for n in (mla_cols, mla_cols, MLA_V_COLS)])
    out_specs = ([row_blk(n) for n in out_cols]
                 + [row_blk(FOX_COLS), pl.BlockSpec((8, tm), lambda i: (0, i))]
                 + [row_blk(n) for n in (mla_cols, mla_cols, MLA_V_COLS)])
    return pl.pallas_call(
        functools.partial(_prep_kernel, tm=tm),
        grid=(s // tm,),
        in_specs=[row_blk(PROJ_COLS), row_blk(6 * LANES), full(pvec), full(wq), full(wkv)],
        out_specs=out_specs,
        out_shape=out_shape,
        scratch_shapes=[pltpu.VMEM((1, LANES), F32)],
        compiler_params=_params("arbitrary"),
        name="prep",
    )(proj, tab, pvec, wq, wkv)


def _attn_kernel(*refs, mode, tq, lambda_init):
    if mode == "diff":
        q_ref, k_ref, v_ref, lam_ref, sub_ref, o_ref, m_ref, l_ref, acc_ref = refs
    elif mode == "fox":
        q_ref, k_ref, v_ref, cq_ref, ck_ref, o_ref, m_ref, l_ref, acc_ref = refs
    else:
        q_ref, k_ref, v_ref, o_ref, m_ref, l_ref, acc_ref = refs
    i = pl.program_id(1)
    reps = tq // LANES

    m_ref[...] = jnp.full(m_ref.shape, NEG_BIG, F32)
    l_ref[...] = jnp.zeros(l_ref.shape, F32)
    acc_ref[...] = jnp.zeros(acc_ref.shape, F32)

    q = q_ref[...]
    if mode == "diff":
        lo = lax.broadcasted_iota(jnp.int32, q.shape, 1) < DIFF_QK_DIM
        zero = jnp.zeros_like(q)
        q_streams = (jnp.where(lo, q, zero), jnp.where(lo, zero, q))
    else:
        q_streams = (q,)
    if mode == "fox":
        cq = jnp.concatenate([cq_ref[...]] * reps, axis=1)

    def step(j, masked):
        start = pl.multiple_of(j * tq, tq)
        k = k_ref[pl.ds(start, tq), :]
        v = v_ref[pl.ds(start, tq), :]
        for si, qs in enumerate(q_streams):
            s = lax.dot_general(qs, k, (((1,), (1,)), ((), ())), preferred_element_type=F32)
            if mode == "fox":
                s = s + cq - ck_ref[0, :, pl.ds(start, tq)]
            if masked:
                r = lax.broadcasted_iota(jnp.int32, s.shape, 0)
                c = lax.broadcasted_iota(jnp.int32, s.shape, 1)
                s = jnp.where(c <= r, s, NEG_BIG)
            m_prev = m_ref[si]
            m_next = jnp.maximum(m_prev, jnp.max(s, axis=1, keepdims=True))
            p = jnp.exp(s - jnp.concatenate([m_next] * reps, axis=1))
            alpha = jnp.exp(m_prev - m_next)
            l_ref[si] = alpha * l_ref[si] + jnp.sum(p, axis=1, keepdims=True)
            m_ref[si] = m_next
            acc_ref[si] = acc_ref[si] * alpha + jnp.dot(p.astype(BF16), v, preferred_element_type=F32)

    def body(j, carry):
        step(j, False)
        return carry

    lax.fori_loop(0, i, body, 0)
    step(i, True)

    if mode == "diff":
        lp = lam_ref[...]
        lam = (jnp.exp(jnp.sum(lp[0:1] * lp[1:2], axis=1, keepdims=True))
               - jnp.exp(jnp.sum(lp[2:3] * lp[3:4], axis=1, keepdims=True)) + lambda_init)
        o = acc_ref[0] / l_ref[0] - lam * (acc_ref[1] / l_ref[1])
        o = o * lax.rsqrt(jnp.mean(o * o, axis=-1, keepdims=True) + EPS) * sub_ref[...]
        o_ref[...] = (o * (1.0 - lambda_init)).astype(o_ref.dtype)
    else:
        o_ref[...] = (acc_ref[0] / l_ref[0]).astype(o_ref.dtype)


def _attention(q, k, v, extras, *, mode, n_heads, dk, tq, lambda_init=0.0):
    s = q.shape[0]
    dv = LANES
    n_streams = 2 if mode == "diff" else 1
    in_specs = [pl.BlockSpec((tq, dk), lambda h, i: (i, h)),
                pl.BlockSpec((s, dk), lambda h, i: (0, h)),
                pl.BlockSpec((s, dv), lambda h, i: (0, h))]
    if mode == "diff":
        lam, sub = extras
        in_specs += [pl.BlockSpec(lam.shape, lambda h, i: (0, 0)),
                     pl.BlockSpec(sub.shape, lambda h, i: (0, 0))]
    elif mode == "fox":
        in_specs += [pl.BlockSpec((tq, LANES), lambda h, i: (i, h)),
                     pl.BlockSpec((1, 1, s), lambda h, i: (h, 0, 0))]
    return pl.pallas_call(
        functools.partial(_attn_kernel, mode=mode, tq=tq, lambda_init=lambda_init),
        grid=(n_heads, s // tq),
        in_specs=in_specs,
        out_specs=pl.BlockSpec((tq, dv), lambda h, i: (i, h)),
        out_shape=jax.ShapeDtypeStruct((s, n_heads * dv), BF16),
        scratch_shapes=[pltpu.VMEM((n_streams, tq, LANES), F32),
                        pltpu.VMEM((n_streams, tq, LANES), F32),
                        pltpu.VMEM((n_streams, tq, dv), F32)],
        compiler_params=_params("arbitrary", "arbitrary"),
        name="attn_" + mode,
    )(q, k, v, *extras)


def _out_proj_kernel(x_ref, oa_ref, ob_ref, oc_ref, w_ref, o_ref):
    a0, a1 = DIFF_V_COLS, DIFF_V_COLS + FOX_COLS
    acc = x_ref[...]
    acc = acc + jnp.dot(oa_ref[...], w_ref[0:a0, :], preferred_element_type=F32)
    acc = acc + jnp.dot(ob_ref[...], w_ref[a0:a1, :], preferred_element_type=F32)
    acc = acc + jnp.dot(oc_ref[...], w_ref[a1:, :], preferred_element_type=F32)
    o_ref[...] = acc


def _out_proj(x, oa, ob, oc, w, *, tm):
    s, d = x.shape
    row_blk = lambda n: pl.BlockSpec((tm, n), lambda i: (i, 0))
    return pl.pallas_call(
        _out_proj_kernel,
        grid=(s // tm,),
        in_specs=[row_blk(d), row_blk(oa.shape[1]), row_blk(ob.shape[1]), row_blk(oc.shape[1]),
                  pl.BlockSpec(w.shape, lambda i: (0, 0))],
        out_specs=row_blk(d),
        out_shape=jax.ShapeDtypeStruct((s, d), F32),
        compiler_params=_params("arbitrary"),
        name="out_proj",
    )(x, oa, ob, oc, w)


def _ffn_kernel(x_ref, g_ref, wu_ref, wd_ref, o_ref, h_ref):
    @pl.when(pl.program_id(1) == 0)
    def _():
        x = x_ref[...]
        inv = lax.rsqrt(jnp.mean(x * x, axis=-1, keepdims=True) + EPS)
        h_ref[...] = (x * inv * g_ref[...]).astype(BF16)
        o_ref[...] = x

    u = jnp.dot(h_ref[...], wu_ref[...], preferred_element_type=F32)
    a = jnp.square(jnp.maximum(u, 0.0)).astype(BF16)
    o_ref[...] += jnp.dot(a, wd_ref[...], preferred_element_type=F32)


def _ffn(x, g, wu, wd, *, tm, tf):
    s, d = x.shape
    f = wu.shape[1]
    return pl.pallas_call(
        _ffn_kernel,
        grid=(s // tm, f // tf),
        in_specs=[pl.BlockSpec((tm, d), lambda i, j: (i, 0)),
                  pl.BlockSpec((1, d), lambda i, j: (0, 0)),
                  pl.BlockSpec((d, tf), lambda i, j: (0, j)),
                  pl.BlockSpec((tf, d), lambda i, j: (j, 0))],
        out_specs=pl.BlockSpec((tm, d), lambda i, j: (i, 0)),
        out_shape=jax.ShapeDtypeStruct((s, d), F32),
        scratch_shapes=[pltpu.VMEM((tm, d), BF16)],
        compiler_params=_params("arbitrary", "arbitrary"),
        name="ffn",
    )(x, g, wu, wd)


def _split_columns(w):
    parts, start = [], 0
    for width in IN_SECTIONS:
        parts.append(w[:, start:start + width])
        start += width
    return parts


def _pack_w_in(w):
    dq, dk, dv, fq, fk, fv, ff, mq, mkv = _split_columns(w)
    zeros = lambda n: jnp.zeros((w.shape[0], n), w.dtype)
    cols = [dq, dk, dv, fq, fk, fv, mq, mkv[:, :MLA_KV_RANK],
            mkv[:, MLA_KV_RANK:], zeros(LANES - MLA_ROPE_DIM), ff, zeros(LANES - N_HEADS_FOX)]
    return jnp.concatenate(cols, axis=1).astype(BF16)


def _pack_w_q_up(w):
    w = w.reshape(MLA_Q_RANK, N_HEADS_MLA, MLA_QK_DIM)
    pad = jnp.zeros((MLA_Q_RANK, N_HEADS_MLA, MLA_PAD_DIM - MLA_QK_DIM), w.dtype)
    w = jnp.concatenate([w[:, :, MLA_ROPE_DIM:], w[:, :, :MLA_ROPE_DIM], pad], axis=2)
    return w.reshape(MLA_Q_RANK, N_HEADS_MLA * MLA_PAD_DIM).astype(BF16)


def _pack_w_kv_up(w):
    w = w.reshape(MLA_KV_RANK, N_HEADS_MLA, MLA_NOPE_DIM + MLA_V_DIM)
    k_nope = w[:, :, :MLA_NOPE_DIM].reshape(MLA_KV_RANK, N_HEADS_MLA * MLA_NOPE_DIM)
    v = w[:, :, MLA_NOPE_DIM:].reshape(MLA_KV_RANK, MLA_V_COLS)
    return jnp.concatenate([k_nope, v], axis=1).astype(BF16)


def _pack_vec(diff_q_norm, diff_k_norm, fox_q_norm, fox_k_norm, fox_forget_bias,
              mla_q_a_norm, mla_kv_a_norm, mla_q_norm, mla_k_norm):
    def nope_rope_pad(g):
        return jnp.concatenate([g[MLA_ROPE_DIM:], g[:MLA_ROPE_DIM],
                                jnp.zeros((MLA_PAD_DIM - MLA_QK_DIM,), g.dtype)])
    parts = [jnp.tile(diff_q_norm, 2), jnp.tile(diff_k_norm, 2), fox_q_norm, fox_k_norm,
             jnp.pad(fox_forget_bias, (0, LANES - N_HEADS_FOX)),
             mla_q_a_norm, mla_kv_a_norm, nope_rope_pad(mla_q_norm), nope_rope_pad(mla_k_norm)]
    return jnp.concatenate(parts).astype(F32).reshape(1, PV_LEN)


def _rope_tables(seq):
    def cos_sin(rot_dim):
        half = rot_dim // 2
        inv_freq = ROPE_THETA ** (-jnp.arange(half, dtype=F32) / half)
        ang = jnp.arange(seq, dtype=F32)[:, None] * inv_freq[None, :]
        return jnp.cos(ang), jnp.sin(ang)

    cos_p, sin_p = cos_sin(PARTIAL_ROT_DIM)
    cos_m, sin_m = cos_sin(MLA_ROPE_DIM)
    hp, hm = PARTIAL_ROT_DIM // 2, MLA_ROPE_DIM // 2
    ones = lambda n: jnp.ones((seq, n), F32)
    zeros = lambda n: jnp.zeros((seq, n), F32)
    rest = DIFF_QK_DIM - PARTIAL_ROT_DIM
    cd = jnp.tile(jnp.concatenate([cos_p, cos_p, ones(rest)], axis=1), (1, 2))
    sd_up = jnp.tile(jnp.concatenate([-sin_p, zeros(hp + rest)], axis=1), (1, 2))
    sd_dn = jnp.tile(jnp.concatenate([zeros(hp), sin_p, zeros(rest)], axis=1), (1, 2))
    pad = LANES - MLA_ROPE_DIM
    cm = jnp.concatenate([cos_m, cos_m, zeros(pad)], axis=1)
    sm_up = jnp.concatenate([-sin_m, zeros(hm + pad)], axis=1)
    sm_dn = jnp.concatenate([zeros(hm), sin_m, zeros(pad)], axis=1)
    return jnp.concatenate([cd, sd_up, sd_dn, cm, sm_up, sm_dn], axis=1)


def _tile(seq, want):
    return min(seq, want)


def kernel(x, norm_mix, w_in, diff_q_norm, diff_k_norm, diff_lambda_q1, diff_lambda_k1,
           diff_lambda_q2, diff_lambda_k2, diff_subln, fox_q_norm, fox_k_norm, fox_forget_bias,
           mla_q_a_norm, mla_kv_a_norm, mla_w_q_up, mla_w_kv_up, mla_q_norm, mla_k_norm,
           w_out, norm_ffn, w_ff_up, w_ff_down):
    batch, seq, d = x.shape
    assert batch == 1 and d == D_MODEL and seq % LANES == 0
    tab = _rope_tables(seq)
    xs = x.reshape(seq, d)
    tq = _tile(seq, 512)
    for l in range(DEPTH):
        proj = _in_proj(xs, norm_mix[l].reshape(1, d), _pack_w_in(w_in[l]),
                        tm=_tile(seq, 512), tn=PROJ_COLS // 2)
        pvec = _pack_vec(diff_q_norm[l], diff_k_norm[l], fox_q_norm[l], fox_k_norm[l],
                         fox_forget_bias[l], mla_q_a_norm[l], mla_kv_a_norm[l],
                         mla_q_norm[l], mla_k_norm[l])
        (qd, kd, vd, qf, kf, vf, cq, ckt, qm, km, vm) = _prep(
            proj, tab, pvec, _pack_w_q_up(mla_w_q_up[l]), _pack_w_kv_up(mla_w_kv_up[l]),
            tm=_tile(seq, 256))
        lambda_init = 0.8 - 0.6 * math.exp(-0.3 * l)
        lam = jnp.stack([diff_lambda_q1[l], diff_lambda_k1[l], diff_lambda_q2[l], diff_lambda_k2[l]])
        o_a = _attention(qd, kd, vd, (lam, diff_subln[l].reshape(1, DIFF_V_DIM)), mode="diff",
                         n_heads=N_HEADS_DIFF, dk=LANES, tq=tq, lambda_init=lambda_init)
        o_b = _attention(qf, kf, vf, (cq, ckt.reshape(8, 1, seq)), mode="fox",
                         n_heads=N_HEADS_FOX, dk=LANES, tq=tq)
        o_c = _attention(qm, km, vm, (), mode="mla", n_heads=N_HEADS_MLA, dk=MLA_PAD_DIM, tq=tq)
        xs = _out_proj(xs, o_a, o_b, o_c, w_out[l].astype(BF16), tm=_tile(seq, 512))
        xs = _ffn(xs, norm_ffn[l].reshape(1, d), w_ff_up[l].astype(BF16), w_ff_down[l].astype(BF16),
                  tm=_tile(seq, 512), tf=1024)
    return xs.reshape(batch, seq, d)
```

```python
import functools
import math

import jax
import jax.numpy as jnp
from jax import lax
from jax.experimental import pallas as pl
from jax.experimental.pallas import tpu as pltpu

D_MODEL = 2048
DEPTH = 2
N_HEADS_DIFF = 4
DIFF_QK_DIM = 64
DIFF_V_DIM = 128
N_HEADS_FOX = 6
FOX_HEAD_DIM = 128
N_HEADS_MLA = 6
MLA_Q_RANK = 512
MLA_KV_RANK = 256
MLA_NOPE_DIM = 128
MLA_ROPE_DIM = 64
MLA_QK_DIM = MLA_ROPE_DIM + MLA_NOPE_DIM
MLA_V_DIM = 128
D_FF = 4 * D_MODEL
ROPE_THETA = 500000.0
PARTIAL_ROT_DIM = DIFF_QK_DIM // 4
EPS = 1e-6

DIFF_Q_COLS = N_HEADS_DIFF * 2 * DIFF_QK_DIM
DIFF_V_COLS = N_HEADS_DIFF * DIFF_V_DIM
FOX_COLS = N_HEADS_FOX * FOX_HEAD_DIM
MLA_V_COLS = N_HEADS_MLA * MLA_V_DIM
IN_SECTIONS = (DIFF_Q_COLS, DIFF_Q_COLS, DIFF_V_COLS, FOX_COLS, FOX_COLS, FOX_COLS,
               N_HEADS_FOX, MLA_Q_RANK, MLA_KV_RANK + MLA_ROPE_DIM)

LANES = 128
MLA_PAD_DIM = 2 * LANES
VMEM_LIMIT_BYTES = 56 * 1024 * 1024

OFF_DQ = 0
OFF_DK = OFF_DQ + DIFF_Q_COLS
OFF_DV = OFF_DK + DIFF_Q_COLS
OFF_FQ = OFF_DV + DIFF_V_COLS
OFF_FK = OFF_FQ + FOX_COLS
OFF_FV = OFF_FK + FOX_COLS
MAIN_COLS = OFF_FV + FOX_COLS
TAIL_MQ = 0
TAIL_CKV = TAIL_MQ + MLA_Q_RANK
TAIL_KR = TAIL_CKV + MLA_KV_RANK
TAIL_FF = TAIL_KR + LANES
TAIL_COLS = TAIL_FF + LANES

PV_G_DQ = 0
PV_G_DK = PV_G_DQ + LANES
PV_G_FQ = PV_G_DK + LANES
PV_G_FK = PV_G_FQ + LANES
PV_F_BIAS = PV_G_FK + LANES
PV_G_QA = PV_F_BIAS + LANES
PV_G_KVA = PV_G_QA + MLA_Q_RANK
PV_G_QM = PV_G_KVA + MLA_KV_RANK
PV_G_KM = PV_G_QM + MLA_PAD_DIM
PV_LEN = PV_G_KM + MLA_PAD_DIM

NEG_BIG = -1e30

BF16 = jnp.bfloat16
F32 = jnp.float32


def _params(*semantics):
    return pltpu.CompilerParams(dimension_semantics=semantics,
                                vmem_limit_bytes=VMEM_LIMIT_BYTES)


def _in_proj_kernel(x_ref, g_ref, w_ref, o_ref, wb_ref):
    @pl.when(pl.program_id(1) == 0)
    def _():
        wb_ref[...] = w_ref[...].astype(BF16)

    x = x_ref[...]
    inv = lax.rsqrt(jnp.mean(x * x, axis=-1, keepdims=True) + EPS)
    h = (x * inv * g_ref[...]).astype(BF16)
    o_ref[...] = jnp.dot(h, wb_ref[...], preferred_element_type=F32)


def _in_proj(x, g, w, layer, n_cols, *, tm, tn, name):
    s, d = x.shape
    return pl.pallas_call(
        _in_proj_kernel,
        grid=(n_cols // tn, s // tm),
        in_specs=[pl.BlockSpec((tm, d), lambda j, i: (i, 0)),
                  pl.BlockSpec((1, d), lambda j, i: (0, 0)),
                  pl.BlockSpec((None, d, tn), lambda j, i: (layer, 0, j))],
        out_specs=pl.BlockSpec((tm, tn), lambda j, i: (i, j)),
        out_shape=jax.ShapeDtypeStruct((s, n_cols), F32),
        scratch_shapes=[pltpu.VMEM((d, tn), BF16)],
        compiler_params=_params("arbitrary", "arbitrary"),
        name=name,
    )(x, g, w)


def _rms(x, n_valid):
    ss = jnp.sum(x * x, axis=-1, keepdims=True)
    return x * lax.rsqrt(ss * (1.0 / n_valid) + EPS)


def _rms_two_halves(x):
    lo = lax.broadcasted_iota(jnp.int32, x.shape, 1) < DIFF_QK_DIM
    x2 = x * x
    s_lo = jnp.sum(jnp.where(lo, x2, 0.0), axis=-1, keepdims=True)
    s_hi = jnp.sum(jnp.where(lo, 0.0, x2), axis=-1, keepdims=True)
    return x * lax.rsqrt(jnp.where(lo, s_lo, s_hi) * (1.0 / DIFF_QK_DIM) + EPS)


def _rope(x, c, s_up, s_dn, half):
    return x * c + pltpu.roll(x, LANES - half, 1) * s_up + pltpu.roll(x, half, 1) * s_dn


def _prep_kernel(proj_ref, tail_ref, tab_ref, pv_ref, wq_ref, wkv_ref,
                 qd_ref, kd_ref, vd_ref, qf_ref, kf_ref, vf_ref, cq_ref, ckt_ref,
                 qm_ref, km_ref, vm_ref, carry_ref, *, tm):
    @pl.when(pl.program_id(0) == 0)
    def _():
        carry_ref[...] = jnp.zeros_like(carry_ref)

    def pv(off, n):
        return pv_ref[:, off:off + n]

    def lanes(ref, off, n=LANES):
        return ref[:, off:off + n]

    cd, sd_up, sd_dn, cm, sm_up, sm_dn = (lanes(tab_ref, k * LANES) for k in range(6))
    half_d = PARTIAL_ROT_DIM // 2
    half_m = MLA_ROPE_DIM // 2

    g_dq, g_dk = pv(PV_G_DQ, LANES), pv(PV_G_DK, LANES)
    for b in range(N_HEADS_DIFF):
        q = _rope(_rms_two_halves(lanes(proj_ref, OFF_DQ + b * LANES)) * g_dq, cd, sd_up, sd_dn, half_d)
        qd_ref[:, b * LANES:(b + 1) * LANES] = (q * DIFF_QK_DIM ** -0.5).astype(BF16)
        k = _rope(_rms_two_halves(lanes(proj_ref, OFF_DK + b * LANES)) * g_dk, cd, sd_up, sd_dn, half_d)
        kd_ref[:, b * LANES:(b + 1) * LANES] = k.astype(BF16)
    vd_ref[...] = lanes(proj_ref, OFF_DV, DIFF_V_COLS).astype(BF16)

    g_fq, g_fk = pv(PV_G_FQ, LANES), pv(PV_G_FK, LANES)
    for h in range(N_HEADS_FOX):
        q = _rms(lanes(proj_ref, OFF_FQ + h * LANES), FOX_HEAD_DIM) * g_fq
        qf_ref[:, h * LANES:(h + 1) * LANES] = (q * FOX_HEAD_DIM ** -0.5).astype(BF16)
        k = _rms(lanes(proj_ref, OFF_FK + h * LANES), FOX_HEAD_DIM) * g_fk
        kf_ref[:, h * LANES:(h + 1) * LANES] = k.astype(BF16)
    vf_ref[...] = lanes(proj_ref, OFF_FV, FOX_COLS).astype(BF16)

    z = lanes(tail_ref, TAIL_FF) + pv(PV_F_BIAS, LANES)
    c = jnp.minimum(z, 0.0) - jnp.log1p(jnp.exp(-jnp.abs(z)))
    row = lax.broadcasted_iota(jnp.int32, c.shape, 0)
    shift = 1
    while shift < tm:
        c = c + jnp.where(row >= shift, pltpu.roll(c, shift, 0), 0.0)
        shift *= 2
    c = c + carry_ref[...]
    carry_ref[...] = c[tm - 1:tm, :]
    for h in range(N_HEADS_FOX):
        cq_ref[:, h * LANES:(h + 1) * LANES] = jnp.broadcast_to(c[:, h:h + 1], (tm, LANES))
    ckt_ref[...] = c.T[0:8, :]

    q_lat = (_rms(lanes(tail_ref, TAIL_MQ, MLA_Q_RANK), MLA_Q_RANK) * pv(PV_G_QA, MLA_Q_RANK)).astype(BF16)
    q_up = jnp.dot(q_lat, wq_ref[...], preferred_element_type=F32)
    c_kv = (_rms(lanes(tail_ref, TAIL_CKV, MLA_KV_RANK), MLA_KV_RANK) * pv(PV_G_KVA, MLA_KV_RANK)).astype(BF16)
    kv_up = jnp.dot(c_kv, wkv_ref[...], preferred_element_type=F32)
    g_qn, g_qr = pv(PV_G_QM, LANES), pv(PV_G_QM + LANES, LANES)
    g_kn, g_kr = pv(PV_G_KM, LANES), pv(PV_G_KM + LANES, LANES)
    k_rope = _rope(_rms(lanes(tail_ref, TAIL_KR), MLA_ROPE_DIM) * g_kr, cm, sm_up, sm_dn, half_m).astype(BF16)
    mla_scale = MLA_QK_DIM ** -0.5
    for h in range(N_HEADS_MLA):
        base = h * MLA_PAD_DIM
        q_nope = _rms(q_up[:, base:base + LANES], MLA_NOPE_DIM) * g_qn
        q_rope = _rope(_rms(q_up[:, base + LANES:base + 2 * LANES], MLA_ROPE_DIM) * g_qr,
                       cm, sm_up, sm_dn, half_m)
        qm_ref[:, base:base + LANES] = (q_nope * mla_scale).astype(BF16)
        qm_ref[:, base + LANES:base + 2 * LANES] = (q_rope * mla_scale).astype(BF16)
        k_nope = _rms(kv_up[:, h * LANES:(h + 1) * LANES], MLA_NOPE_DIM) * g_kn
        km_ref[:, base:base + LANES] = k_nope.astype(BF16)
        km_ref[:, base + LANES:base + 2 * LANES] = k_rope
    vm_ref[...] = kv_up[:, N_HEADS_MLA * MLA_NOPE_DIM:].astype(BF16)


def _prep(proj, tail, tab, pvec, wq, wkv, *, tm):
    s = proj.shape[0]
    row_blk = lambda n: pl.BlockSpec((tm, n), lambda i: (i, 0))
    full = lambda a: pl.BlockSpec(a.shape, lambda i: (0, 0))
    mla_cols = N_HEADS_MLA * MLA_PAD_DIM
    out_cols = (DIFF_Q_COLS, DIFF_Q_COLS, DIFF_V_COLS, FOX_COLS, FOX_COLS, FOX_COLS)
    out_shape = ([jax.ShapeDtypeStruct((s, n), BF16) for n in out_cols]
                 + [jax.ShapeDtypeStruct((s, FOX_COLS), F32), jax.ShapeDtypeStruct((8, s), F32)]
                 + [jax.ShapeDtypeStruct((s, n), BF16) for n in (mla_cols, mla_cols, MLA_V_COLS)])
    out_specs = ([row_blk(n) for n in out_cols]
                 + [row_blk(FOX_COLS), pl.BlockSpec((8, tm), lambda i: (0, i))]
                 + [row_blk(n) for n in (mla_cols, mla_cols, MLA_V_COLS)])
    return pl.pallas_call(
        functools.partial(_prep_kernel, tm=tm),
        grid=(s // tm,),
        in_specs=[row_blk(MAIN_COLS), row_blk(TAIL_COLS), row_blk(6 * LANES),
                  full(pvec), full(wq), full(wkv)],
        out_specs=out_specs,
        out_shape=out_shape,
        scratch_shapes=[pltpu.VMEM((1, LANES), F32)],
        compiler_params=_params("arbitrary"),
        name="prep",
    )(proj, tail, tab, pvec, wq, wkv)


def _attn_kernel(*refs, mode, tq, lambda_init):
    if mode == "diff":
        q_ref, k_ref, v_ref, lam_ref, sub_ref, o_ref, m_ref, l_ref, acc_ref = refs
    elif mode == "fox":
        q_ref, k_ref, v_ref, cq_ref, ck_ref, o_ref, m_ref, l_ref, acc_ref = refs
    else:
        q_ref, k_ref, v_ref, o_ref, m_ref, l_ref, acc_ref = refs
    i = pl.program_id(1)
    reps = tq // LANES

    m_ref[...] = jnp.full(m_ref.shape, NEG_BIG, F32)
    l_ref[...] = jnp.zeros(l_ref.shape, F32)
    acc_ref[...] = jnp.zeros(acc_ref.shape, F32)

    q = q_ref[...]
    if mode == "diff":
        lo = lax.broadcasted_iota(jnp.int32, q.shape, 1) < DIFF_QK_DIM
        zero = jnp.zeros_like(q)
        q_streams = (jnp.where(lo, q, zero), jnp.where(lo, zero, q))
    else:
        q_streams = (q,)
    if mode == "fox":
        cq = jnp.concatenate([cq_ref[...]] * reps, axis=1)

    def step(j, masked):
        start = pl.multiple_of(j * tq, tq)
        k = k_ref[pl.ds(start, tq), :]
        v = v_ref[pl.ds(start, tq), :]
        for si, qs in enumerate(q_streams):
            s = lax.dot_general(qs, k, (((1,), (1,)), ((), ())), preferred_element_type=F32)
            if mode == "fox":
                s = s + cq - ck_ref[0, :, pl.ds(start, tq)]
            if masked:
                r = lax.broadcasted_iota(jnp.int32, s.shape, 0)
                c = lax.broadcasted_iota(jnp.int32, s.shape, 1)
                s = jnp.where(c <= r, s, NEG_BIG)
            m_prev = m_ref[si]
            m_next = jnp.maximum(m_prev, jnp.max(s, axis=1, keepdims=True))
            p = jnp.exp(s - jnp.concatenate([m_next] * reps, axis=1))
            alpha = jnp.exp(m_prev - m_next)
            l_ref[si] = alpha * l_ref[si] + jnp.sum(p, axis=1, keepdims=True)
            m_ref[si] = m_next
            acc_ref[si] = acc_ref[si] * alpha + jnp.dot(p.astype(BF16), v, preferred_element_type=F32)

    def pair(jj, carry):
        step(2 * jj, False)
        step(2 * jj + 1, False)
        return carry

    lax.fori_loop(0, lax.shift_right_logical(i, 1), pair, 0)

    @pl.when(lax.bitwise_and(i, 1) == 1)
    def _():
        step(i - 1, False)

    step(i, True)

    if mode == "diff":
        lp = lam_ref[...]
        lam = (jnp.exp(jnp.sum(lp[0:1] * lp[1:2], axis=1, keepdims=True))
               - jnp.exp(jnp.sum(lp[2:3] * lp[3:4], axis=1, keepdims=True)) + lambda_init)
        o = acc_ref[0] / l_ref[0] - lam * (acc_ref[1] / l_ref[1])
        o = o * lax.rsqrt(jnp.mean(o * o, axis=-1, keepdims=True) + EPS) * sub_ref[...]
        o_ref[...] = (o * (1.0 - lambda_init)).astype(o_ref.dtype)
    else:
        o_ref[...] = (acc_ref[0] / l_ref[0]).astype(o_ref.dtype)


def _attention(q, k, v, extras, *, mode, n_heads, dk, tq, lambda_init=0.0):
    s = q.shape[0]
    dv = LANES
    n_streams = 2 if mode == "diff" else 1
    in_specs = [pl.BlockSpec((tq, dk), lambda h, i: (i, h)),
                pl.BlockSpec((s, dk), lambda h, i: (0, h)),
                pl.BlockSpec((s, dv), lambda h, i: (0, h))]
    if mode == "diff":
        lam, sub = extras
        in_specs += [pl.BlockSpec(lam.shape, lambda h, i: (0, 0)),
                     pl.BlockSpec(sub.shape, lambda h, i: (0, 0))]
    elif mode == "fox":
        in_specs += [pl.BlockSpec((tq, LANES), lambda h, i: (i, h)),
                     pl.BlockSpec((1, 1, s), lambda h, i: (h, 0, 0))]
    return pl.pallas_call(
        functools.partial(_attn_kernel, mode=mode, tq=tq, lambda_init=lambda_init),
        grid=(n_heads, s // tq),
        in_specs=in_specs,
        out_specs=pl.BlockSpec((tq, dv), lambda h, i: (i, h)),
        out_shape=jax.ShapeDtypeStruct((s, n_heads * dv), BF16),
        scratch_shapes=[pltpu.VMEM((n_streams, tq, LANES), F32),
                        pltpu.VMEM((n_streams, tq, LANES), F32),
                        pltpu.VMEM((n_streams, tq, dv), F32)],
        compiler_params=_params("arbitrary", "arbitrary"),
        name="attn_" + mode,
    )(q, k, v, *extras)


def _out_proj_kernel(x_ref, oa_ref, ob_ref, oc_ref, w_ref, o_ref):
    a0, a1 = DIFF_V_COLS, DIFF_V_COLS + FOX_COLS
    acc = x_ref[...]
    acc = acc + jnp.dot(oa_ref[...], w_ref[0:a0, :], preferred_element_type=F32)
    acc = acc + jnp.dot(ob_ref[...], w_ref[a0:a1, :], preferred_element_type=F32)
    acc = acc + jnp.dot(oc_ref[...], w_ref[a1:, :], preferred_element_type=F32)
    o_ref[...] = acc


def _out_proj(x, oa, ob, oc, w, *, tm):
    s, d = x.shape
    row_blk = lambda n: pl.BlockSpec((tm, n), lambda i: (i, 0))
    return pl.pallas_call(
        _out_proj_kernel,
        grid=(s // tm,),
        in_specs=[row_blk(d), row_blk(oa.shape[1]), row_blk(ob.shape[1]), row_blk(oc.shape[1]),
                  pl.BlockSpec(w.shape, lambda i: (0, 0))],
        out_specs=row_blk(d),
        out_shape=jax.ShapeDtypeStruct((s, d), F32),
        compiler_params=_params("arbitrary"),
        name="out_proj",
    )(x, oa, ob, oc, w)


def _ffn_kernel(x_ref, g_ref, wu_ref, wd_ref, o_ref, h_ref):
    @pl.when(pl.program_id(1) == 0)
    def _():
        x = x_ref[...]
        inv = lax.rsqrt(jnp.mean(x * x, axis=-1, keepdims=True) + EPS)
        h_ref[...] = (x * inv * g_ref[...]).astype(BF16)
        o_ref[...] = x

    u = jnp.dot(h_ref[...], wu_ref[...], preferred_element_type=F32)
    a = jnp.square(jnp.maximum(u, 0.0)).astype(BF16)
    o_ref[...] += jnp.dot(a, wd_ref[...], preferred_element_type=F32)


def _ffn(x, g, wu, wd, *, tm, tf):
    s, d = x.shape
    f = wu.shape[1]
    return pl.pallas_call(
        _ffn_kernel,
        grid=(s // tm, f // tf),
        in_specs=[pl.BlockSpec((tm, d), lambda i, j: (i, 0)),
                  pl.BlockSpec((1, d), lambda i, j: (0, 0)),
                  pl.BlockSpec((d, tf), lambda i, j: (0, j)),
                  pl.BlockSpec((tf, d), lambda i, j: (j, 0))],
        out_specs=pl.BlockSpec((tm, d), lambda i, j: (i, 0)),
        out_shape=jax.ShapeDtypeStruct((s, d), F32),
        scratch_shapes=[pltpu.VMEM((tm, d), BF16)],
        compiler_params=_params("arbitrary", "arbitrary"),
        name="ffn",
    )(x, g, wu, wd)


def _pack_w_in_tail(w_in):
    ff0 = MAIN_COLS
    mq0 = ff0 + N_HEADS_FOX
    ckv0 = mq0 + MLA_Q_RANK
    kr0 = ckv0 + MLA_KV_RANK
    assert kr0 + MLA_ROPE_DIM == sum(IN_SECTIONS) == w_in.shape[-1]
    zeros = lambda n: jnp.zeros(w_in.shape[:2] + (n,), w_in.dtype)
    cols = [w_in[..., mq0:ckv0], w_in[..., ckv0:kr0], w_in[..., kr0:], zeros(LANES - MLA_ROPE_DIM),
            w_in[..., ff0:mq0], zeros(LANES - N_HEADS_FOX)]
    return jnp.concatenate(cols, axis=-1)


def _pack_w_q_up(w):
    w = w.reshape(MLA_Q_RANK, N_HEADS_MLA, MLA_QK_DIM)
    pad = jnp.zeros((MLA_Q_RANK, N_HEADS_MLA, MLA_PAD_DIM - MLA_QK_DIM), w.dtype)
    w = jnp.concatenate([w[:, :, MLA_ROPE_DIM:], w[:, :, :MLA_ROPE_DIM], pad], axis=2)
    return w.reshape(MLA_Q_RANK, N_HEADS_MLA * MLA_PAD_DIM).astype(BF16)


def _pack_w_kv_up(w):
    w = w.reshape(MLA_KV_RANK, N_HEADS_MLA, MLA_NOPE_DIM + MLA_V_DIM)
    k_nope = w[:, :, :MLA_NOPE_DIM].reshape(MLA_KV_RANK, N_HEADS_MLA * MLA_NOPE_DIM)
    v = w[:, :, MLA_NOPE_DIM:].reshape(MLA_KV_RANK, MLA_V_COLS)
    return jnp.concatenate([k_nope, v], axis=1).astype(BF16)


def _pack_vec(diff_q_norm, diff_k_norm, fox_q_norm, fox_k_norm, fox_forget_bias,
              mla_q_a_norm, mla_kv_a_norm, mla_q_norm, mla_k_norm):
    def nope_rope_pad(g):
        return jnp.concatenate([g[MLA_ROPE_DIM:], g[:MLA_ROPE_DIM],
                                jnp.zeros((MLA_PAD_DIM - MLA_QK_DIM,), g.dtype)])
    parts = [jnp.tile(diff_q_norm, 2), jnp.tile(diff_k_norm, 2), fox_q_norm, fox_k_norm,
             jnp.pad(fox_forget_bias, (0, LANES - N_HEADS_FOX)),
             mla_q_a_norm, mla_kv_a_norm, nope_rope_pad(mla_q_norm), nope_rope_pad(mla_k_norm)]
    return jnp.concatenate(parts).astype(F32).reshape(1, PV_LEN)


def _rope_tables(seq):
    def cos_sin(rot_dim):
        half = rot_dim // 2
        inv_freq = ROPE_THETA ** (-jnp.arange(half, dtype=F32) / half)
        ang = jnp.arange(seq, dtype=F32)[:, None] * inv_freq[None, :]
        return jnp.cos(ang), jnp.sin(ang)

    cos_p, sin_p = cos_sin(PARTIAL_ROT_DIM)
    cos_m, sin_m = cos_sin(MLA_ROPE_DIM)
    hp, hm = PARTIAL_ROT_DIM // 2, MLA_ROPE_DIM // 2
    ones = lambda n: jnp.ones((seq, n), F32)
    zeros = lambda n: jnp.zeros((seq, n), F32)
    rest = DIFF_QK_DIM - PARTIAL_ROT_DIM
    cd = jnp.tile(jnp.concatenate([cos_p, cos_p, ones(rest)], axis=1), (1, 2))
    sd_up = jnp.tile(jnp.concatenate([-sin_p, zeros(hp + rest)], axis=1), (1, 2))
    sd_dn = jnp.tile(jnp.concatenate([zeros(hp), sin_p, zeros(rest)], axis=1), (1, 2))
    pad = LANES - MLA_ROPE_DIM
    cm = jnp.concatenate([cos_m, cos_m, zeros(pad)], axis=1)
    sm_up = jnp.concatenate([-sin_m, zeros(hm + pad)], axis=1)
    sm_dn = jnp.concatenate([zeros(hm), sin_m, zeros(pad)], axis=1)
    return jnp.concatenate([cd, sd_up, sd_dn, cm, sm_up, sm_dn], axis=1)


def _tile(seq, want):
    return min(seq, want)


def kernel(x, norm_mix, w_in, diff_q_norm, diff_k_norm, diff_lambda_q1, diff_lambda_k1,
           diff_lambda_q2, diff_lambda_k2, diff_subln, fox_q_norm, fox_k_norm, fox_forget_bias,
           mla_q_a_norm, mla_kv_a_norm, mla_w_q_up, mla_w_kv_up, mla_q_norm, mla_k_norm,
           w_out, norm_ffn, w_ff_up, w_ff_down):
    batch, seq, d = x.shape
    assert batch == 1 and d == D_MODEL and seq % LANES == 0
    tab = _rope_tables(seq)
    xs = x.reshape(seq, d)
    tq = _tile(seq, 512)
    w_in_tail = _pack_w_in_tail(w_in)
    for l in range(DEPTH):
        g_mix = norm_mix[l].reshape(1, d)
        proj = _in_proj(xs, g_mix, w_in, l, MAIN_COLS, tm=_tile(seq, 512), tn=MAIN_COLS // 3,
                        name="in_proj_main")
        tail = _in_proj(xs, g_mix, w_in_tail, l, TAIL_COLS, tm=_tile(seq, 512), tn=TAIL_COLS,
                        name="in_proj_tail")
        pvec = _pack_vec(diff_q_norm[l], diff_k_norm[l], fox_q_norm[l], fox_k_norm[l],
                         fox_forget_bias[l], mla_q_a_norm[l], mla_kv_a_norm[l],
                         mla_q_norm[l], mla_k_norm[l])
        (qd, kd, vd, qf, kf, vf, cq, ckt, qm, km, vm) = _prep(
            proj, tail, tab, pvec, _pack_w_q_up(mla_w_q_up[l]), _pack_w_kv_up(mla_w_kv_up[l]),
            tm=_tile(seq, 256))
        lambda_init = 0.8 - 0.6 * math.exp(-0.3 * l)
        lam = jnp.stack([diff_lambda_q1[l], diff_lambda_k1[l], diff_lambda_q2[l], diff_lambda_k2[l]])
        o_a = _attention(qd, kd, vd, (lam, diff_subln[l].reshape(1, DIFF_V_DIM)), mode="diff",
                         n_heads=N_HEADS_DIFF, dk=LANES, tq=tq, lambda_init=lambda_init)
        o_b = _attention(qf, kf, vf, (cq, ckt.reshape(8, 1, seq)), mode="fox",
                         n_heads=N_HEADS_FOX, dk=LANES, tq=tq)
        o_c = _attention(qm, km, vm, (), mode="mla", n_heads=N_HEADS_MLA, dk=MLA_PAD_DIM, tq=tq)
        xs = _out_proj(xs, o_a, o_b, o_c, w_out[l].astype(BF16), tm=_tile(seq, 512))
        xs = _ffn(xs, norm_ffn[l].reshape(1, d), w_ff_up[l].astype(BF16), w_ff_down[l].astype(BF16),
                  tm=_tile(seq, 512), tf=1024)
    return xs.reshape(batch, seq, d)
```

```python
import functools
import math

import jax
import jax.numpy as jnp
from jax import lax
from jax.experimental import pallas as pl
from jax.experimental.pallas import tpu as pltpu

D_MODEL = 2048
DEPTH = 2
N_HEADS_DIFF = 4
DIFF_QK_DIM = 64
DIFF_V_DIM = 128
N_HEADS_FOX = 6
FOX_HEAD_DIM = 128
N_HEADS_MLA = 6
MLA_Q_RANK = 512
MLA_KV_RANK = 256
MLA_NOPE_DIM = 128
MLA_ROPE_DIM = 64
MLA_QK_DIM = MLA_ROPE_DIM + MLA_NOPE_DIM
MLA_V_DIM = 128
D_FF = 4 * D_MODEL
ROPE_THETA = 500000.0
PARTIAL_ROT_DIM = DIFF_QK_DIM // 4
EPS = 1e-6

DIFF_Q_COLS = N_HEADS_DIFF * 2 * DIFF_QK_DIM
DIFF_V_COLS = N_HEADS_DIFF * DIFF_V_DIM
FOX_COLS = N_HEADS_FOX * FOX_HEAD_DIM
MLA_V_COLS = N_HEADS_MLA * MLA_V_DIM
IN_SECTIONS = (DIFF_Q_COLS, DIFF_Q_COLS, DIFF_V_COLS, FOX_COLS, FOX_COLS, FOX_COLS,
               N_HEADS_FOX, MLA_Q_RANK, MLA_KV_RANK + MLA_ROPE_DIM)

LANES = 128
MLA_PAD_DIM = 2 * LANES
VMEM_LIMIT_BYTES = 56 * 1024 * 1024

OFF_DQ = 0
OFF_DK = OFF_DQ + DIFF_Q_COLS
OFF_DV = OFF_DK + DIFF_Q_COLS
OFF_FQ = OFF_DV + DIFF_V_COLS
OFF_FK = OFF_FQ + FOX_COLS
OFF_FV = OFF_FK + FOX_COLS
MAIN_COLS = OFF_FV + FOX_COLS
TAIL_MQ = 0
TAIL_CKV = TAIL_MQ + MLA_Q_RANK
TAIL_KR = TAIL_CKV + MLA_KV_RANK
TAIL_FF = TAIL_KR + LANES
TAIL_COLS = TAIL_FF + LANES

PV_G_DQ = 0
PV_G_DK = PV_G_DQ + LANES
PV_G_FQ = PV_G_DK + LANES
PV_G_FK = PV_G_FQ + LANES
PV_F_BIAS = PV_G_FK + LANES
PV_G_QA = PV_F_BIAS + LANES
PV_G_KVA = PV_G_QA + MLA_Q_RANK
PV_G_QM = PV_G_KVA + MLA_KV_RANK
PV_G_KM = PV_G_QM + MLA_PAD_DIM
PV_LEN = PV_G_KM + MLA_PAD_DIM

NEG_BIG = -1e30
LOG2E = math.log2(math.e)

BF16 = jnp.bfloat16
F32 = jnp.float32


def _params(*semantics):
    return pltpu.CompilerParams(dimension_semantics=semantics,
                                vmem_limit_bytes=VMEM_LIMIT_BYTES)


def _in_proj_kernel(x_ref, g_ref, w_ref, o_ref, wb_ref):
    @pl.when(pl.program_id(1) == 0)
    def _():
        wb_ref[...] = w_ref[...].astype(BF16)

    x = x_ref[...]
    inv = lax.rsqrt(jnp.mean(x * x, axis=-1, keepdims=True) + EPS)
    h = (x * inv * g_ref[...]).astype(BF16)
    o_ref[...] = jnp.dot(h, wb_ref[...], preferred_element_type=F32)


def _in_proj(x, g, w, layer, n_cols, *, tm, tn, name):
    s, d = x.shape
    return pl.pallas_call(
        _in_proj_kernel,
        grid=(n_cols // tn, s // tm),
        in_specs=[pl.BlockSpec((tm, d), lambda j, i: (i, 0)),
                  pl.BlockSpec((1, d), lambda j, i: (0, 0)),
                  pl.BlockSpec((None, d, tn), lambda j, i: (layer, 0, j))],
        out_specs=pl.BlockSpec((tm, tn), lambda j, i: (i, j)),
        out_shape=jax.ShapeDtypeStruct((s, n_cols), F32),
        scratch_shapes=[pltpu.VMEM((d, tn), BF16)],
        compiler_params=_params("arbitrary", "arbitrary"),
        name=name,
    )(x, g, w)


def _rms(x, n_valid):
    ss = jnp.sum(x * x, axis=-1, keepdims=True)
    return x * lax.rsqrt(ss * (1.0 / n_valid) + EPS)


def _rms_two_halves(x):
    lo = lax.broadcasted_iota(jnp.int32, x.shape, 1) < DIFF_QK_DIM
    x2 = x * x
    s_lo = jnp.sum(jnp.where(lo, x2, 0.0), axis=-1, keepdims=True)
    s_hi = jnp.sum(jnp.where(lo, 0.0, x2), axis=-1, keepdims=True)
    return x * lax.rsqrt(jnp.where(lo, s_lo, s_hi) * (1.0 / DIFF_QK_DIM) + EPS)


def _rope(x, c, s_up, s_dn, half):
    return x * c + pltpu.roll(x, LANES - half, 1) * s_up + pltpu.roll(x, half, 1) * s_dn


def _t_bf16(x):
    return x.T.astype(BF16)


def _prep_kernel(proj_ref, tail_ref, tab_ref, pv_ref, wq_ref, wkv_ref,
                 qd_ref, kd_ref, vd_ref, qf_ref, kf_ref, vf_ref, ck_ref,
                 qm_ref, km_ref, vm_ref, carry_ref, *, tm):
    @pl.when(pl.program_id(0) == 0)
    def _():
        carry_ref[...] = jnp.zeros_like(carry_ref)

    def pv(off, n):
        return pv_ref[:, off:off + n]

    def lanes(ref, off, n=LANES):
        return ref[:, off:off + n]

    cd, sd_up, sd_dn, cm, sm_up, sm_dn = (lanes(tab_ref, k * LANES) for k in range(6))
    half_d = PARTIAL_ROT_DIM // 2
    half_m = MLA_ROPE_DIM // 2

    g_dq, g_dk = pv(PV_G_DQ, LANES), pv(PV_G_DK, LANES)
    for b in range(N_HEADS_DIFF):
        q = _rope(_rms_two_halves(lanes(proj_ref, OFF_DQ + b * LANES)) * g_dq, cd, sd_up, sd_dn, half_d)
        qd_ref[b] = _t_bf16(q * (DIFF_QK_DIM ** -0.5 * LOG2E))
        k = _rope(_rms_two_halves(lanes(proj_ref, OFF_DK + b * LANES)) * g_dk, cd, sd_up, sd_dn, half_d)
        kd_ref[:, b * LANES:(b + 1) * LANES] = k.astype(BF16)
        vd_ref[b] = _t_bf16(lanes(proj_ref, OFF_DV + b * LANES))

    g_fq, g_fk = pv(PV_G_FQ, LANES), pv(PV_G_FK, LANES)
    for h in range(N_HEADS_FOX):
        q = _rms(lanes(proj_ref, OFF_FQ + h * LANES), FOX_HEAD_DIM) * g_fq
        qf_ref[h] = _t_bf16(q * (FOX_HEAD_DIM ** -0.5 * LOG2E))
        k = _rms(lanes(proj_ref, OFF_FK + h * LANES), FOX_HEAD_DIM) * g_fk
        kf_ref[:, h * LANES:(h + 1) * LANES] = k.astype(BF16)
        vf_ref[h] = _t_bf16(lanes(proj_ref, OFF_FV + h * LANES))

    z = lanes(tail_ref, TAIL_FF) + pv(PV_F_BIAS, LANES)
    c = jnp.minimum(z, 0.0) - jnp.log1p(jnp.exp(-jnp.abs(z)))
    row = lax.broadcasted_iota(jnp.int32, c.shape, 0)
    shift = 1
    while shift < tm:
        c = c + jnp.where(row >= shift, pltpu.roll(c, shift, 0), 0.0)
        shift *= 2
    c = c + carry_ref[...]
    carry_ref[...] = c[tm - 1:tm, :]
    c2 = c * LOG2E
    for h in range(N_HEADS_FOX):
        ck_ref[:, h * LANES:(h + 1) * LANES] = jnp.broadcast_to(c2[:, h:h + 1], (tm, LANES))

    q_lat = (_rms(lanes(tail_ref, TAIL_MQ, MLA_Q_RANK), MLA_Q_RANK) * pv(PV_G_QA, MLA_Q_RANK)).astype(BF16)
    q_up = jnp.dot(q_lat, wq_ref[...], preferred_element_type=F32)
    c_kv = (_rms(lanes(tail_ref, TAIL_CKV, MLA_KV_RANK), MLA_KV_RANK) * pv(PV_G_KVA, MLA_KV_RANK)).astype(BF16)
    kv_up = jnp.dot(c_kv, wkv_ref[...], preferred_element_type=F32)
    g_qn, g_qr = pv(PV_G_QM, LANES), pv(PV_G_QM + LANES, LANES)
    g_kn, g_kr = pv(PV_G_KM, LANES), pv(PV_G_KM + LANES, LANES)
    k_rope = _rope(_rms(lanes(tail_ref, TAIL_KR), MLA_ROPE_DIM) * g_kr, cm, sm_up, sm_dn, half_m).astype(BF16)
    mla_scale = MLA_QK_DIM ** -0.5 * LOG2E
    v_off = N_HEADS_MLA * MLA_NOPE_DIM
    for h in range(N_HEADS_MLA):
        base = h * MLA_PAD_DIM
        q_nope = _rms(q_up[:, base:base + LANES], MLA_NOPE_DIM) * g_qn
        q_rope = _rope(_rms(q_up[:, base + LANES:base + 2 * LANES], MLA_ROPE_DIM) * g_qr,
                       cm, sm_up, sm_dn, half_m)
        qm_ref[h, 0:LANES, :] = _t_bf16(q_nope * mla_scale)
        qm_ref[h, LANES:2 * LANES, :] = _t_bf16(q_rope * mla_scale)
        k_nope = _rms(kv_up[:, h * LANES:(h + 1) * LANES], MLA_NOPE_DIM) * g_kn
        km_ref[:, base:base + LANES] = k_nope.astype(BF16)
        km_ref[:, base + LANES:base + 2 * LANES] = k_rope
        vm_ref[h] = _t_bf16(kv_up[:, v_off + h * LANES:v_off + (h + 1) * LANES])


def _prep(proj, tail, tab, pvec, wq, wkv, *, tm):
    s = proj.shape[0]
    row_blk = lambda n: pl.BlockSpec((tm, n), lambda i: (i, 0))
    full = lambda a: pl.BlockSpec(a.shape, lambda i: (0, 0))

    def head_t(n_heads, dim):
        return (jax.ShapeDtypeStruct((n_heads, dim, s), BF16),
                pl.BlockSpec((n_heads, dim, tm), lambda i: (0, 0, i)))

    def rows(n, dtype=BF16):
        return jax.ShapeDtypeStruct((s, n), dtype), row_blk(n)

    outs = [head_t(N_HEADS_DIFF, LANES), rows(DIFF_Q_COLS), head_t(N_HEADS_DIFF, DIFF_V_DIM),
            head_t(N_HEADS_FOX, FOX_HEAD_DIM), rows(FOX_COLS), head_t(N_HEADS_FOX, FOX_HEAD_DIM),
            rows(FOX_COLS, F32),
            head_t(N_HEADS_MLA, MLA_PAD_DIM), rows(N_HEADS_MLA * MLA_PAD_DIM),
            head_t(N_HEADS_MLA, MLA_V_DIM)]
    return pl.pallas_call(
        functools.partial(_prep_kernel, tm=tm),
        grid=(s // tm,),
        in_specs=[row_blk(MAIN_COLS), row_blk(TAIL_COLS), row_blk(6 * LANES),
                  full(pvec), full(wq), full(wkv)],
        out_specs=[o[1] for o in outs],
        out_shape=[o[0] for o in outs],
        scratch_shapes=[pltpu.VMEM((1, LANES), F32)],
        compiler_params=_params("arbitrary"),
        name="prep",
    )(proj, tail, tab, pvec, wq, wkv)


def _attn_kernel(*refs, mode, tq, lambda_init):
    if mode == "diff":
        q_ref, k_ref, v_ref, lam_ref, sub_ref, o_ref, m_ref, l_ref, acc_ref, s_ref = refs
    elif mode == "fox":
        q_ref, k_ref, v_ref, ck_ref, o_ref, m_ref, l_ref, acc_ref, s_ref = refs
    else:
        q_ref, k_ref, v_ref, o_ref, m_ref, l_ref, acc_ref, s_ref = refs
    i = pl.program_id(1)
    reps = tq // LANES

    m_ref[...] = jnp.full(m_ref.shape, NEG_BIG, F32)
    l_ref[...] = jnp.zeros(l_ref.shape, F32)
    acc_ref[...] = jnp.zeros(acc_ref.shape, F32)

    q_t = q_ref[...]
    if mode == "diff":
        lo = lax.broadcasted_iota(jnp.int32, q_t.shape, 0) < DIFF_QK_DIM
        zero = jnp.zeros_like(q_t)
        q_streams = (jnp.where(lo, q_t, zero), jnp.where(lo, zero, q_t))
    else:
        q_streams = (q_t,)

    def scores(j, si, buf):
        start = pl.multiple_of(j * tq, tq)
        s_ref[buf] = jnp.dot(k_ref[pl.ds(start, tq), :], q_streams[si], preferred_element_type=F32)

    def softmax_pv(j, si, buf, masked):
        start = pl.multiple_of(j * tq, tq)
        s_t = s_ref[buf]
        if mode == "fox":
            s_t = s_t - jnp.concatenate([ck_ref[pl.ds(start, tq), :]] * reps, axis=1)
        if masked:
            key = lax.broadcasted_iota(jnp.int32, s_t.shape, 0)
            qry = lax.broadcasted_iota(jnp.int32, s_t.shape, 1)
            s_t = jnp.where(key <= qry, s_t, NEG_BIG)
        m_prev = m_ref[si]
        m_next = jnp.maximum(m_prev, jnp.max(s_t, axis=0, keepdims=True))
        p_t = jnp.exp2(s_t - m_next)
        alpha = jnp.exp2(m_prev - m_next)
        l_ref[si] = alpha * l_ref[si] + jnp.sum(p_t, axis=0, keepdims=True)
        m_ref[si] = m_next
        v_t = v_ref[:, pl.ds(start, tq)]
        acc_ref[si] = acc_ref[si] * alpha + jnp.dot(v_t, p_t.astype(BF16), preferred_element_type=F32)

    scores(0, 0, 0)
    if len(q_streams) == 2:
        def block(j, carry):
            scores(j, 1, 1)
            softmax_pv(j, 0, 0, False)
            scores(j + 1, 0, 0)
            softmax_pv(j, 1, 1, False)
            return carry

        lax.fori_loop(0, i, block, 0)
        scores(i, 1, 1)
        softmax_pv(i, 0, 0, True)
        softmax_pv(i, 1, 1, True)
    else:
        def pair(t, carry):
            scores(2 * t + 1, 0, 1)
            softmax_pv(2 * t, 0, 0, False)
            scores(2 * t + 2, 0, 0)
            softmax_pv(2 * t + 1, 0, 1, False)
            return carry

        lax.fori_loop(0, lax.shift_right_logical(i, 1), pair, 0)
        odd = lax.bitwise_and(i, 1) == 1

        @pl.when(odd)
        def _():
            scores(i, 0, 1)
            softmax_pv(i - 1, 0, 0, False)
            softmax_pv(i, 0, 1, True)

        @pl.when(jnp.logical_not(odd))
        def _():
            softmax_pv(i, 0, 0, True)

    def out_t(si):
        return acc_ref[si] * (1.0 / l_ref[si])

    if mode == "diff":
        lp = lam_ref[...]
        lam = (jnp.exp(jnp.sum(lp[0:1] * lp[1:2], axis=1, keepdims=True))
               - jnp.exp(jnp.sum(lp[2:3] * lp[3:4], axis=1, keepdims=True)) + lambda_init)
        o = (out_t(0) - lam * out_t(1)).T
        o = o * lax.rsqrt(jnp.mean(o * o, axis=-1, keepdims=True) + EPS) * sub_ref[...]
        o_ref[...] = (o * (1.0 - lambda_init)).astype(o_ref.dtype)
    else:
        o_ref[...] = out_t(0).T.astype(o_ref.dtype)


def _attention(q_t, k, v_t, extras, *, mode, tq, lambda_init=0.0):
    n_heads, dk, s = q_t.shape
    dv = v_t.shape[1]
    n_streams = 2 if mode == "diff" else 1
    in_specs = [pl.BlockSpec((None, dk, tq), lambda h, i: (h, 0, i)),
                pl.BlockSpec((s, dk), lambda h, i: (0, h)),
                pl.BlockSpec((None, dv, s), lambda h, i: (h, 0, 0))]
    if mode == "diff":
        lam, sub = extras
        in_specs += [pl.BlockSpec(lam.shape, lambda h, i: (0, 0)),
                     pl.BlockSpec(sub.shape, lambda h, i: (0, 0))]
    elif mode == "fox":
        in_specs += [pl.BlockSpec((s, LANES), lambda h, i: (0, h))]
    return pl.pallas_call(
        functools.partial(_attn_kernel, mode=mode, tq=tq, lambda_init=lambda_init),
        grid=(n_heads, s // tq),
        in_specs=in_specs,
        out_specs=pl.BlockSpec((tq, dv), lambda h, i: (i, h)),
        out_shape=jax.ShapeDtypeStruct((s, n_heads * dv), BF16),
        scratch_shapes=[pltpu.VMEM((n_streams, 1, tq), F32),
                        pltpu.VMEM((n_streams, 1, tq), F32),
                        pltpu.VMEM((n_streams, dv, tq), F32),
                        pltpu.VMEM((2, tq, tq), F32)],
        compiler_params=_params("arbitrary", "arbitrary"),
        name="attn_" + mode,
    )(q_t, k, v_t, *extras)


def _out_proj_kernel(x_ref, oa_ref, ob_ref, oc_ref, w_ref, o_ref):
    a0, a1 = DIFF_V_COLS, DIFF_V_COLS + FOX_COLS
    acc = x_ref[...]
    acc = acc + jnp.dot(oa_ref[...], w_ref[0:a0, :], preferred_element_type=F32)
    acc = acc + jnp.dot(ob_ref[...], w_ref[a0:a1, :], preferred_element_type=F32)
    acc = acc + jnp.dot(oc_ref[...], w_ref[a1:, :], preferred_element_type=F32)
    o_ref[...] = acc


def _out_proj(x, oa, ob, oc, w, *, tm):
    s, d = x.shape
    row_blk = lambda n: pl.BlockSpec((tm, n), lambda i: (i, 0))
    return pl.pallas_call(
        _out_proj_kernel,
        grid=(s // tm,),
        in_specs=[row_blk(d), row_blk(oa.shape[1]), row_blk(ob.shape[1]), row_blk(oc.shape[1]),
                  pl.BlockSpec(w.shape, lambda i: (0, 0))],
        out_specs=row_blk(d),
        out_shape=jax.ShapeDtypeStruct((s, d), F32),
        compiler_params=_params("arbitrary"),
        name="out_proj",
    )(x, oa, ob, oc, w)


def _ffn_kernel(x_ref, g_ref, wu_ref, wd_ref, o_ref, h_ref):
    @pl.when(pl.program_id(1) == 0)
    def _():
        x = x_ref[...]
        inv = lax.rsqrt(jnp.mean(x * x, axis=-1, keepdims=True) + EPS)
        h_ref[...] = (x * inv * g_ref[...]).astype(BF16)
        o_ref[...] = x

    u = jnp.dot(h_ref[...], wu_ref[...], preferred_element_type=F32)
    a = jnp.square(jnp.maximum(u, 0.0)).astype(BF16)
    o_ref[...] += jnp.dot(a, wd_ref[...], preferred_element_type=F32)


def _ffn(x, g, wu, wd, *, tm, tf):
    s, d = x.shape
    f = wu.shape[1]
    return pl.pallas_call(
        _ffn_kernel,
        grid=(s // tm, f // tf),
        in_specs=[pl.BlockSpec((tm, d), lambda i, j: (i, 0)),
                  pl.BlockSpec((1, d), lambda i, j: (0, 0)),
                  pl.BlockSpec((d, tf), lambda i, j: (0, j)),
                  pl.BlockSpec((tf, d), lambda i, j: (j, 0))],
        out_specs=pl.BlockSpec((tm, d), lambda i, j: (i, 0)),
        out_shape=jax.ShapeDtypeStruct((s, d), F32),
        scratch_shapes=[pltpu.VMEM((tm, d), BF16)],
        compiler_params=_params("arbitrary", "arbitrary"),
        name="ffn",
    )(x, g, wu, wd)


def _pack_w_in_tail(w_in):
    ff0 = MAIN_COLS
    mq0 = ff0 + N_HEADS_FOX
    ckv0 = mq0 + MLA_Q_RANK
    kr0 = ckv0 + MLA_KV_RANK
    assert kr0 + MLA_ROPE_DIM == sum(IN_SECTIONS) == w_in.shape[-1]
    zeros = lambda n: jnp.zeros(w_in.shape[:2] + (n,), w_in.dtype)
    cols = [w_in[..., mq0:ckv0], w_in[..., ckv0:kr0], w_in[..., kr0:], zeros(LANES - MLA_ROPE_DIM),
            w_in[..., ff0:mq0], zeros(LANES - N_HEADS_FOX)]
    return jnp.concatenate(cols, axis=-1)


def _pack_w_q_up(w):
    w = w.reshape(MLA_Q_RANK, N_HEADS_MLA, MLA_QK_DIM)
    pad = jnp.zeros((MLA_Q_RANK, N_HEADS_MLA, MLA_PAD_DIM - MLA_QK_DIM), w.dtype)
    w = jnp.concatenate([w[:, :, MLA_ROPE_DIM:], w[:, :, :MLA_ROPE_DIM], pad], axis=2)
    return w.reshape(MLA_Q_RANK, N_HEADS_MLA * MLA_PAD_DIM).astype(BF16)


def _pack_w_kv_up(w):
    w = w.reshape(MLA_KV_RANK, N_HEADS_MLA, MLA_NOPE_DIM + MLA_V_DIM)
    k_nope = w[:, :, :MLA_NOPE_DIM].reshape(MLA_KV_RANK, N_HEADS_MLA * MLA_NOPE_DIM)
    v = w[:, :, MLA_NOPE_DIM:].reshape(MLA_KV_RANK, MLA_V_COLS)
    return jnp.concatenate([k_nope, v], axis=1).astype(BF16)


def _pack_vec(diff_q_norm, diff_k_norm, fox_q_norm, fox_k_norm, fox_forget_bias,
              mla_q_a_norm, mla_kv_a_norm, mla_q_norm, mla_k_norm):
    def nope_rope_pad(g):
        return jnp.concatenate([g[MLA_ROPE_DIM:], g[:MLA_ROPE_DIM],
                                jnp.zeros((MLA_PAD_DIM - MLA_QK_DIM,), g.dtype)])
    parts = [jnp.tile(diff_q_norm, 2), jnp.tile(diff_k_norm, 2), fox_q_norm, fox_k_norm,
             jnp.pad(fox_forget_bias, (0, LANES - N_HEADS_FOX)),
             mla_q_a_norm, mla_kv_a_norm, nope_rope_pad(mla_q_norm), nope_rope_pad(mla_k_norm)]
    return jnp.concatenate(parts).astype(F32).reshape(1, PV_LEN)


def _rope_tables(seq):
    def cos_sin(rot_dim):
        half = rot_dim // 2
        inv_freq = ROPE_THETA ** (-jnp.arange(half, dtype=F32) / half)
        ang = jnp.arange(seq, dtype=F32)[:, None] * inv_freq[None, :]
        return jnp.cos(ang), jnp.sin(ang)

    cos_p, sin_p = cos_sin(PARTIAL_ROT_DIM)
    cos_m, sin_m = cos_sin(MLA_ROPE_DIM)
    hp, hm = PARTIAL_ROT_DIM // 2, MLA_ROPE_DIM // 2
    ones = lambda n: jnp.ones((seq, n), F32)
    zeros = lambda n: jnp.zeros((seq, n), F32)
    rest = DIFF_QK_DIM - PARTIAL_ROT_DIM
    cd = jnp.tile(jnp.concatenate([cos_p, cos_p, ones(rest)], axis=1), (1, 2))
    sd_up = jnp.tile(jnp.concatenate([-sin_p, zeros(hp + rest)], axis=1), (1, 2))
    sd_dn = jnp.tile(jnp.concatenate([zeros(hp), sin_p, zeros(rest)], axis=1), (1, 2))
    pad = LANES - MLA_ROPE_DIM
    cm = jnp.concatenate([cos_m, cos_m, zeros(pad)], axis=1)
    sm_up = jnp.concatenate([-sin_m, zeros(hm + pad)], axis=1)
    sm_dn = jnp.concatenate([zeros(hm), sin_m, zeros(pad)], axis=1)
    return jnp.concatenate([cd, sd_up, sd_dn, cm, sm_up, sm_dn], axis=1)


def _tile(seq, want):
    return min(seq, want)


def kernel(x, norm_mix, w_in, diff_q_norm, diff_k_norm, diff_lambda_q1, diff_lambda_k1,
           diff_lambda_q2, diff_lambda_k2, diff_subln, fox_q_norm, fox_k_norm, fox_forget_bias,
           mla_q_a_norm, mla_kv_a_norm, mla_w_q_up, mla_w_kv_up, mla_q_norm, mla_k_norm,
           w_out, norm_ffn, w_ff_up, w_ff_down):
    batch, seq, d = x.shape
    assert batch == 1 and d == D_MODEL and seq % LANES == 0
    tab = _rope_tables(seq)
    xs = x.reshape(seq, d)
    tq = _tile(seq, 512)
    w_in_tail = _pack_w_in_tail(w_in)
    for l in range(DEPTH):
        g_mix = norm_mix[l].reshape(1, d)
        proj = _in_proj(xs, g_mix, w_in, l, MAIN_COLS, tm=_tile(seq, 512), tn=MAIN_COLS // 3,
                        name="in_proj_main")
        tail = _in_proj(xs, g_mix, w_in_tail, l, TAIL_COLS, tm=_tile(seq, 512), tn=TAIL_COLS,
                        name="in_proj_tail")
        pvec = _pack_vec(diff_q_norm[l], diff_k_norm[l], fox_q_norm[l], fox_k_norm[l],
                         fox_forget_bias[l], mla_q_a_norm[l], mla_kv_a_norm[l],
                         mla_q_norm[l], mla_k_norm[l])
        (qd, kd, vd, qf, kf, vf, ck, qm, km, vm) = _prep(
            proj, tail, tab, pvec, _pack_w_q_up(mla_w_q_up[l]), _pack_w_kv_up(mla_w_kv_up[l]),
            tm=_tile(seq, 256))
        lambda_init = 0.8 - 0.6 * math.exp(-0.3 * l)
        lam = jnp.stack([diff_lambda_q1[l], diff_lambda_k1[l], diff_lambda_q2[l], diff_lambda_k2[l]])
        o_a = _attention(qd, kd, vd, (lam, diff_subln[l].reshape(1, DIFF_V_DIM)), mode="diff",
                         tq=tq, lambda_init=lambda_init)
        o_b = _attention(qf, kf, vf, (ck,), mode="fox", tq=tq)
        o_c = _attention(qm, km, vm, (), mode="mla", tq=tq)
        xs = _out_proj(xs, o_a, o_b, o_c, w_out[l].astype(BF16), tm=_tile(seq, 512))
        xs = _ffn(xs, norm_ffn[l].reshape(1, d), w_ff_up[l].astype(BF16), w_ff_down[l].astype(BF16),
                  tm=_tile(seq, 512), tf=1024)
    return xs.reshape(batch, seq, d)
```

```python
import functools
import math

import jax
import jax.numpy as jnp
from jax import lax
from jax.experimental import pallas as pl
from jax.experimental.pallas import tpu as pltpu

D_MODEL = 2048
DEPTH = 2
N_HEADS_DIFF = 4
DIFF_QK_DIM = 64
DIFF_V_DIM = 128
N_HEADS_FOX = 6
FOX_HEAD_DIM = 128
N_HEADS_MLA = 6
MLA_Q_RANK = 512
MLA_KV_RANK = 256
MLA_NOPE_DIM = 128
MLA_ROPE_DIM = 64
MLA_QK_DIM = MLA_ROPE_DIM + MLA_NOPE_DIM
MLA_V_DIM = 128
D_FF = 4 * D_MODEL
ROPE_THETA = 500000.0
PARTIAL_ROT_DIM = DIFF_QK_DIM // 4
EPS = 1e-6

DIFF_Q_COLS = N_HEADS_DIFF * 2 * DIFF_QK_DIM
DIFF_V_COLS = N_HEADS_DIFF * DIFF_V_DIM
FOX_COLS = N_HEADS_FOX * FOX_HEAD_DIM
MLA_V_COLS = N_HEADS_MLA * MLA_V_DIM
IN_SECTIONS = (DIFF_Q_COLS, DIFF_Q_COLS, DIFF_V_COLS, FOX_COLS, FOX_COLS, FOX_COLS,
               N_HEADS_FOX, MLA_Q_RANK, MLA_KV_RANK + MLA_ROPE_DIM)

LANES = 128
MLA_PAD_DIM = 2 * LANES
VMEM_LIMIT_BYTES = 56 * 1024 * 1024

OFF_DQ = 0
OFF_DK = OFF_DQ + DIFF_Q_COLS
OFF_DV = OFF_DK + DIFF_Q_COLS
OFF_FQ = OFF_DV + DIFF_V_COLS
OFF_FK = OFF_FQ + FOX_COLS
OFF_FV = OFF_FK + FOX_COLS
MAIN_COLS = OFF_FV + FOX_COLS
TAIL_SRC_COLS = N_HEADS_FOX + MLA_Q_RANK + MLA_KV_RANK + MLA_ROPE_DIM
TAIL_MQ = 0
TAIL_CKV = TAIL_MQ + MLA_Q_RANK
TAIL_KR = TAIL_CKV + MLA_KV_RANK
TAIL_FF = TAIL_KR + LANES
TAIL_COLS = TAIL_FF + LANES
IN_TILE = MAIN_COLS // 3
PROJ_COLS = MAIN_COLS + IN_TILE
OFF_TAIL = MAIN_COLS
assert TAIL_COLS <= IN_TILE and IN_TILE % LANES == 0

PV_G_DQ = 0
PV_G_DK = PV_G_DQ + LANES
PV_G_FQ = PV_G_DK + LANES
PV_G_FK = PV_G_FQ + LANES
PV_F_BIAS = PV_G_FK + LANES
PV_G_QA = PV_F_BIAS + LANES
PV_G_KVA = PV_G_QA + MLA_Q_RANK
PV_G_QM = PV_G_KVA + MLA_KV_RANK
PV_G_KM = PV_G_QM + MLA_PAD_DIM
PV_LEN = PV_G_KM + MLA_PAD_DIM

NEG_BIG = -1e30
LOG2E = math.log2(math.e)

BF16 = jnp.bfloat16
F32 = jnp.float32


def _params(*semantics):
    return pltpu.CompilerParams(dimension_semantics=semantics,
                                vmem_limit_bytes=VMEM_LIMIT_BYTES)


def _in_proj_kernel(x_ref, g_ref, w_ref, o_ref, wb_ref):
    j = pl.program_id(0)
    first_row_tile = pl.program_id(1) == 0
    n_main = MAIN_COLS // IN_TILE

    @pl.when(jnp.logical_and(first_row_tile, j < n_main))
    def _():
        wb_ref[...] = w_ref[...].astype(BF16)

    @pl.when(jnp.logical_and(first_row_tile, j == n_main))
    def _():
        width = TAIL_KR + LANES
        w = w_ref[:, 0:width]
        r = pltpu.roll(w, width - N_HEADS_FOX, 1)
        lane = lax.broadcasted_iota(jnp.int32, (w.shape[0], LANES), 1)
        wb_ref[:, TAIL_MQ:TAIL_KR] = r[:, TAIL_MQ:TAIL_KR].astype(BF16)
        wb_ref[:, TAIL_KR:TAIL_FF] = jnp.where(lane < MLA_ROPE_DIM, r[:, TAIL_KR:TAIL_FF], 0.0).astype(BF16)
        wb_ref[:, TAIL_FF:TAIL_COLS] = jnp.where(lane < N_HEADS_FOX, w[:, 0:LANES], 0.0).astype(BF16)
        wb_ref[:, TAIL_COLS:] = jnp.zeros((w.shape[0], IN_TILE - TAIL_COLS), BF16)

    x = x_ref[...]
    inv = lax.rsqrt(jnp.mean(x * x, axis=-1, keepdims=True) + EPS)
    h = (x * inv * g_ref[...]).astype(BF16)
    o_ref[...] = jnp.dot(h, wb_ref[...], preferred_element_type=F32)


def _in_proj(x, g, w_in, layer, *, tm):
    s, d = x.shape
    assert w_in.shape[-1] == MAIN_COLS + TAIL_SRC_COLS
    return pl.pallas_call(
        _in_proj_kernel,
        grid=(PROJ_COLS // IN_TILE, s // tm),
        in_specs=[pl.BlockSpec((tm, d), lambda j, i: (i, 0)),
                  pl.BlockSpec((1, d), lambda j, i: (0, 0)),
                  pl.BlockSpec((None, d, IN_TILE), lambda j, i: (layer, 0, j))],
        out_specs=pl.BlockSpec((tm, IN_TILE), lambda j, i: (i, j)),
        out_shape=jax.ShapeDtypeStruct((s, PROJ_COLS), F32),
        scratch_shapes=[pltpu.VMEM((d, IN_TILE), BF16)],
        compiler_params=_params("arbitrary", "arbitrary"),
        name="in_proj",
    )(x, g, w_in)


def _rms(x, n_valid):
    ss = jnp.sum(x * x, axis=-1, keepdims=True)
    return x * lax.rsqrt(ss * (1.0 / n_valid) + EPS)


def _rms_two_halves(x):
    lo = lax.broadcasted_iota(jnp.int32, x.shape, 1) < DIFF_QK_DIM
    x2 = x * x
    s_lo = jnp.sum(jnp.where(lo, x2, 0.0), axis=-1, keepdims=True)
    s_hi = jnp.sum(jnp.where(lo, 0.0, x2), axis=-1, keepdims=True)
    return x * lax.rsqrt(jnp.where(lo, s_lo, s_hi) * (1.0 / DIFF_QK_DIM) + EPS)


def _rope(x, c, s_up, s_dn, half):
    return x * c + pltpu.roll(x, LANES - half, 1) * s_up + pltpu.roll(x, half, 1) * s_dn


def _t_bf16(x):
    return x.T.astype(BF16)


def _prep_kernel(proj_ref, tail_ref, tab_ref, pv_ref, wq_ref, wkv_ref,
                 qd_ref, kd_ref, vd_ref, qf_ref, kf_ref, vf_ref, ck_ref,
                 qm_ref, km_ref, vm_ref, carry_ref, *, tm):
    @pl.when(pl.program_id(0) == 0)
    def _():
        carry_ref[...] = jnp.zeros_like(carry_ref)

    def pv(off, n):
        return pv_ref[:, off:off + n]

    def lanes(ref, off, n=LANES):
        return ref[:, off:off + n]

    cd, sd_up, sd_dn, cm, sm_up, sm_dn = (lanes(tab_ref, k * LANES) for k in range(6))
    half_d = PARTIAL_ROT_DIM // 2
    half_m = MLA_ROPE_DIM // 2

    g_dq, g_dk = pv(PV_G_DQ, LANES), pv(PV_G_DK, LANES)
    for b in range(N_HEADS_DIFF):
        q = _rope(_rms_two_halves(lanes(proj_ref, OFF_DQ + b * LANES)) * g_dq, cd, sd_up, sd_dn, half_d)
        qd_ref[b] = _t_bf16(q * (DIFF_QK_DIM ** -0.5 * LOG2E))
        k = _rope(_rms_two_halves(lanes(proj_ref, OFF_DK + b * LANES)) * g_dk, cd, sd_up, sd_dn, half_d)
        kd_ref[:, b * LANES:(b + 1) * LANES] = k.astype(BF16)
        vd_ref[b] = _t_bf16(lanes(proj_ref, OFF_DV + b * LANES))

    g_fq, g_fk = pv(PV_G_FQ, LANES), pv(PV_G_FK, LANES)
    for h in range(N_HEADS_FOX):
        q = _rms(lanes(proj_ref, OFF_FQ + h * LANES), FOX_HEAD_DIM) * g_fq
        qf_ref[h] = _t_bf16(q * (FOX_HEAD_DIM ** -0.5 * LOG2E))
        k = _rms(lanes(proj_ref, OFF_FK + h * LANES), FOX_HEAD_DIM) * g_fk
        kf_ref[:, h * LANES:(h + 1) * LANES] = k.astype(BF16)
        vf_ref[h] = _t_bf16(lanes(proj_ref, OFF_FV + h * LANES))

    z = lanes(tail_ref, TAIL_FF) + pv(PV_F_BIAS, LANES)
    c = jnp.minimum(z, 0.0) - jnp.log1p(jnp.exp(-jnp.abs(z)))
    row = lax.broadcasted_iota(jnp.int32, c.shape, 0)
    shift = 1
    while shift < tm:
        c = c + jnp.where(row >= shift, pltpu.roll(c, shift, 0), 0.0)
        shift *= 2
    c = c + carry_ref[...]
    carry_ref[...] = c[tm - 1:tm, :]
    c2 = c * LOG2E
    for h in range(N_HEADS_FOX):
        ck_ref[:, h * LANES:(h + 1) * LANES] = jnp.broadcast_to(c2[:, h:h + 1], (tm, LANES))

    q_lat = (_rms(lanes(tail_ref, TAIL_MQ, MLA_Q_RANK), MLA_Q_RANK) * pv(PV_G_QA, MLA_Q_RANK)).astype(BF16)
    q_up = jnp.dot(q_lat, wq_ref[...], preferred_element_type=F32)
    c_kv = (_rms(lanes(tail_ref, TAIL_CKV, MLA_KV_RANK), MLA_KV_RANK) * pv(PV_G_KVA, MLA_KV_RANK)).astype(BF16)
    kv_up = jnp.dot(c_kv, wkv_ref[...], preferred_element_type=F32)
    g_qn, g_qr = pv(PV_G_QM, LANES), pv(PV_G_QM + LANES, LANES)
    g_kn, g_kr = pv(PV_G_KM, LANES), pv(PV_G_KM + LANES, LANES)
    k_rope = _rope(_rms(lanes(tail_ref, TAIL_KR), MLA_ROPE_DIM) * g_kr, cm, sm_up, sm_dn, half_m).astype(BF16)
    mla_scale = MLA_QK_DIM ** -0.5 * LOG2E
    v_off = N_HEADS_MLA * MLA_NOPE_DIM
    for h in range(N_HEADS_MLA):
        base = h * MLA_PAD_DIM
        q_nope = _rms(q_up[:, base:base + LANES], MLA_NOPE_DIM) * g_qn
        q_rope = _rope(_rms(q_up[:, base + LANES:base + 2 * LANES], MLA_ROPE_DIM) * g_qr,
                       cm, sm_up, sm_dn, half_m)
        qm_ref[h, 0:LANES, :] = _t_bf16(q_nope * mla_scale)
        qm_ref[h, LANES:2 * LANES, :] = _t_bf16(q_rope * mla_scale)
        k_nope = _rms(kv_up[:, h * LANES:(h + 1) * LANES], MLA_NOPE_DIM) * g_kn
        km_ref[:, base:base + LANES] = k_nope.astype(BF16)
        km_ref[:, base + LANES:base + 2 * LANES] = k_rope
        vm_ref[h] = _t_bf16(kv_up[:, v_off + h * LANES:v_off + (h + 1) * LANES])


def _prep(proj, tab, pvec, wq, wkv, *, tm):
    s = proj.shape[0]
    tail_blk = pl.BlockSpec((tm, IN_TILE), lambda i: (i, MAIN_COLS // IN_TILE))
    row_blk = lambda n: pl.BlockSpec((tm, n), lambda i: (i, 0))
    full = lambda a: pl.BlockSpec(a.shape, lambda i: (0, 0))

    def head_t(n_heads, dim):
        return (jax.ShapeDtypeStruct((n_heads, dim, s), BF16),
                pl.BlockSpec((n_heads, dim, tm), lambda i: (0, 0, i)))

    def rows(n, dtype=BF16):
        return jax.ShapeDtypeStruct((s, n), dtype), row_blk(n)

    outs = [head_t(N_HEADS_DIFF, LANES), rows(DIFF_Q_COLS), head_t(N_HEADS_DIFF, DIFF_V_DIM),
            head_t(N_HEADS_FOX, FOX_HEAD_DIM), rows(FOX_COLS), head_t(N_HEADS_FOX, FOX_HEAD_DIM),
            rows(FOX_COLS, F32),
            head_t(N_HEADS_MLA, MLA_PAD_DIM), rows(N_HEADS_MLA * MLA_PAD_DIM),
            head_t(N_HEADS_MLA, MLA_V_DIM)]
    return pl.pallas_call(
        functools.partial(_prep_kernel, tm=tm),
        grid=(s // tm,),
        in_specs=[row_blk(MAIN_COLS), tail_blk, row_blk(6 * LANES),
                  full(pvec), full(wq), full(wkv)],
        out_specs=[o[1] for o in outs],
        out_shape=[o[0] for o in outs],
        scratch_shapes=[pltpu.VMEM((1, LANES), F32)],
        compiler_params=_params("arbitrary"),
        name="prep",
    )(proj, proj, tab, pvec, wq, wkv)


def _attn_kernel(*refs, mode, tq, lambda_init):
    if mode == "diff":
        q_ref, k_ref, v_ref, lam_ref, sub_ref, o_ref, m_ref, l_ref, acc_ref, s_ref = refs
    elif mode == "fox":
        q_ref, k_ref, v_ref, ck_ref, o_ref, m_ref, l_ref, acc_ref, s_ref = refs
    else:
        q_ref, k_ref, v_ref, o_ref, m_ref, l_ref, acc_ref, s_ref = refs
    i = pl.program_id(1)
    reps = tq // LANES

    m_ref[...] = jnp.full(m_ref.shape, NEG_BIG, F32)
    l_ref[...] = jnp.zeros(l_ref.shape, F32)
    acc_ref[...] = jnp.zeros(acc_ref.shape, F32)

    q_t = q_ref[...]
    if mode == "diff":
        lo = lax.broadcasted_iota(jnp.int32, q_t.shape, 0) < DIFF_QK_DIM
        zero = jnp.zeros_like(q_t)
        q_streams = (jnp.where(lo, q_t, zero), jnp.where(lo, zero, q_t))
    else:
        q_streams = (q_t,)

    def scores(j, si, buf):
        start = pl.multiple_of(j * tq, tq)
        s_ref[buf] = jnp.dot(k_ref[pl.ds(start, tq), :], q_streams[si], preferred_element_type=F32)

    def softmax_pv(j, si, buf, masked):
        start = pl.multiple_of(j * tq, tq)
        s_t = s_ref[buf]
        if mode == "fox":
            s_t = s_t - jnp.concatenate([ck_ref[pl.ds(start, tq), :]] * reps, axis=1)
        if masked:
            key = lax.broadcasted_iota(jnp.int32, s_t.shape, 0)
            qry = lax.broadcasted_iota(jnp.int32, s_t.shape, 1)
            s_t = jnp.where(key <= qry, s_t, NEG_BIG)
        m_prev = m_ref[si]
        m_next = jnp.maximum(m_prev, jnp.max(s_t, axis=0, keepdims=True))
        p_t = jnp.exp2(s_t - m_next)
        alpha = jnp.exp2(m_prev - m_next)
        l_ref[si] = alpha * l_ref[si] + jnp.sum(p_t, axis=0, keepdims=True)
        m_ref[si] = m_next
        v_t = v_ref[:, pl.ds(start, tq)]
        acc_ref[si] = acc_ref[si] * alpha + jnp.dot(v_t, p_t.astype(BF16), preferred_element_type=F32)

    scores(0, 0, 0)
    if len(q_streams) == 2:
        def block(j, carry):
            scores(j, 1, 1)
            softmax_pv(j, 0, 0, False)
            scores(j + 1, 0, 0)
            softmax_pv(j, 1, 1, False)
            return carry

        lax.fori_loop(0, i, block, 0)
        scores(i, 1, 1)
        softmax_pv(i, 0, 0, True)
        softmax_pv(i, 1, 1, True)
    else:
        def pair(t, carry):
            scores(2 * t + 1, 0, 1)
            softmax_pv(2 * t, 0, 0, False)
            scores(2 * t + 2, 0, 0)
            softmax_pv(2 * t + 1, 0, 1, False)
            return carry

        lax.fori_loop(0, lax.shift_right_logical(i, 1), pair, 0)
        odd = lax.bitwise_and(i, 1) == 1

        @pl.when(odd)
        def _():
            scores(i, 0, 1)
            softmax_pv(i - 1, 0, 0, False)
            softmax_pv(i, 0, 1, True)

        @pl.when(jnp.logical_not(odd))
        def _():
            softmax_pv(i, 0, 0, True)

    def out_t(si):
        return acc_ref[si] * (1.0 / l_ref[si])

    if mode == "diff":
        lp = lam_ref[...]
        lam = (jnp.exp(jnp.sum(lp[0:1] * lp[1:2], axis=1, keepdims=True))
               - jnp.exp(jnp.sum(lp[2:3] * lp[3:4], axis=1, keepdims=True)) + lambda_init)
        o = (out_t(0) - lam * out_t(1)).T
        o = o * lax.rsqrt(jnp.mean(o * o, axis=-1, keepdims=True) + EPS) * sub_ref[...]
        o_ref[...] = (o * (1.0 - lambda_init)).astype(o_ref.dtype)
    else:
        o_ref[...] = out_t(0).T.astype(o_ref.dtype)


def _attention(q_t, k, v_t, extras, *, mode, tq, lambda_init=0.0):
    n_heads, dk, s = q_t.shape
    dv = v_t.shape[1]
    n_streams = 2 if mode == "diff" else 1
    in_specs = [pl.BlockSpec((None, dk, tq), lambda h, i: (h, 0, i)),
                pl.BlockSpec((s, dk), lambda h, i: (0, h)),
                pl.BlockSpec((None, dv, s), lambda h, i: (h, 0, 0))]
    if mode == "diff":
        lam, sub = extras
        in_specs += [pl.BlockSpec(lam.shape, lambda h, i: (0, 0)),
                     pl.BlockSpec(sub.shape, lambda h, i: (0, 0))]
    elif mode == "fox":
        in_specs += [pl.BlockSpec((s, LANES), lambda h, i: (0, h))]
    return pl.pallas_call(
        functools.partial(_attn_kernel, mode=mode, tq=tq, lambda_init=lambda_init),
        grid=(n_heads, s // tq),
        in_specs=in_specs,
        out_specs=pl.BlockSpec((tq, dv), lambda h, i: (i, h)),
        out_shape=jax.ShapeDtypeStruct((s, n_heads * dv), BF16),
        scratch_shapes=[pltpu.VMEM((n_streams, 1, tq), F32),
                        pltpu.VMEM((n_streams, 1, tq), F32),
                        pltpu.VMEM((n_streams, dv, tq), F32),
                        pltpu.VMEM((2, tq, tq), F32)],
        compiler_params=_params("arbitrary", "arbitrary"),
        name="attn_" + mode,
    )(q_t, k, v_t, *extras)


def _out_proj_kernel(x_ref, oa_ref, ob_ref, oc_ref, w_ref, o_ref):
    a0, a1 = DIFF_V_COLS, DIFF_V_COLS + FOX_COLS
    acc = x_ref[...]
    acc = acc + jnp.dot(oa_ref[...], w_ref[0:a0, :], preferred_element_type=F32)
    acc = acc + jnp.dot(ob_ref[...], w_ref[a0:a1, :], preferred_element_type=F32)
    acc = acc + jnp.dot(oc_ref[...], w_ref[a1:, :], preferred_element_type=F32)
    o_ref[...] = acc


def _out_proj(x, oa, ob, oc, w, layer, *, tm):
    s, d = x.shape
    row_blk = lambda n: pl.BlockSpec((tm, n), lambda i: (i, 0))
    return pl.pallas_call(
        _out_proj_kernel,
        grid=(s // tm,),
        in_specs=[row_blk(d), row_blk(oa.shape[1]), row_blk(ob.shape[1]), row_blk(oc.shape[1]),
                  pl.BlockSpec((None,) + w.shape[1:], lambda i: (layer, 0, 0))],
        out_specs=row_blk(d),
        out_shape=jax.ShapeDtypeStruct((s, d), F32),
        compiler_params=_params("arbitrary"),
        name="out_proj",
    )(x, oa, ob, oc, w)


def _ffn_kernel(x_ref, g_ref, wu_ref, wd_ref, o_ref, h_ref):
    @pl.when(pl.program_id(1) == 0)
    def _():
        x = x_ref[...]
        inv = lax.rsqrt(jnp.mean(x * x, axis=-1, keepdims=True) + EPS)
        h_ref[...] = (x * inv * g_ref[...]).astype(BF16)
        o_ref[...] = x

    u = jnp.dot(h_ref[...], wu_ref[...], preferred_element_type=F32)
    a = jnp.square(jnp.maximum(u, 0.0)).astype(BF16)
    o_ref[...] += jnp.dot(a, wd_ref[...], preferred_element_type=F32)


def _ffn(x, g, wu, wd, layer, *, tm, tf):
    s, d = x.shape
    f = wu.shape[2]
    return pl.pallas_call(
        _ffn_kernel,
        grid=(s // tm, f // tf),
        in_specs=[pl.BlockSpec((tm, d), lambda i, j: (i, 0)),
                  pl.BlockSpec((1, d), lambda i, j: (0, 0)),
                  pl.BlockSpec((None, d, tf), lambda i, j: (layer, 0, j)),
                  pl.BlockSpec((None, tf, d), lambda i, j: (layer, j, 0))],
        out_specs=pl.BlockSpec((tm, d), lambda i, j: (i, 0)),
        out_shape=jax.ShapeDtypeStruct((s, d), F32),
        scratch_shapes=[pltpu.VMEM((tm, d), BF16)],
        compiler_params=_params("arbitrary", "arbitrary"),
        name="ffn",
    )(x, g, wu, wd)


def _pack_w_q_up(w):
    w = w.reshape(MLA_Q_RANK, N_HEADS_MLA, MLA_QK_DIM)
    pad = jnp.zeros((MLA_Q_RANK, N_HEADS_MLA, MLA_PAD_DIM - MLA_QK_DIM), w.dtype)
    w = jnp.concatenate([w[:, :, MLA_ROPE_DIM:], w[:, :, :MLA_ROPE_DIM], pad], axis=2)
    return w.reshape(MLA_Q_RANK, N_HEADS_MLA * MLA_PAD_DIM).astype(BF16)


def _pack_w_kv_up(w):
    w = w.reshape(MLA_KV_RANK, N_HEADS_MLA, MLA_NOPE_DIM + MLA_V_DIM)
    k_nope = w[:, :, :MLA_NOPE_DIM].reshape(MLA_KV_RANK, N_HEADS_MLA * MLA_NOPE_DIM)
    v = w[:, :, MLA_NOPE_DIM:].reshape(MLA_KV_RANK, MLA_V_COLS)
    return jnp.concatenate([k_nope, v], axis=1).astype(BF16)


def _pack_vec(diff_q_norm, diff_k_norm, fox_q_norm, fox_k_norm, fox_forget_bias,
              mla_q_a_norm, mla_kv_a_norm, mla_q_norm, mla_k_norm):
    def nope_rope_pad(g):
        return jnp.concatenate([g[MLA_ROPE_DIM:], g[:MLA_ROPE_DIM],
                                jnp.zeros((MLA_PAD_DIM - MLA_QK_DIM,), g.dtype)])
    parts = [jnp.tile(diff_q_norm, 2), jnp.tile(diff_k_norm, 2), fox_q_norm, fox_k_norm,
             jnp.pad(fox_forget_bias, (0, LANES - N_HEADS_FOX)),
             mla_q_a_norm, mla_kv_a_norm, nope_rope_pad(mla_q_norm), nope_rope_pad(mla_k_norm)]
    return jnp.concatenate(parts).astype(F32).reshape(1, PV_LEN)


def _rope_tables(seq):
    def cos_sin(rot_dim):
        half = rot_dim // 2
        inv_freq = ROPE_THETA ** (-jnp.arange(half, dtype=F32) / half)
        ang = jnp.arange(seq, dtype=F32)[:, None] * inv_freq[None, :]
        return jnp.cos(ang), jnp.sin(ang)

    cos_p, sin_p = cos_sin(PARTIAL_ROT_DIM)
    cos_m, sin_m = cos_sin(MLA_ROPE_DIM)
    hp, hm = PARTIAL_ROT_DIM // 2, MLA_ROPE_DIM // 2
    ones = lambda n: jnp.ones((seq, n), F32)
    zeros = lambda n: jnp.zeros((seq, n), F32)
    rest = DIFF_QK_DIM - PARTIAL_ROT_DIM
    cd = jnp.tile(jnp.concatenate([cos_p, cos_p, ones(rest)], axis=1), (1, 2))
    sd_up = jnp.tile(jnp.concatenate([-sin_p, zeros(hp + rest)], axis=1), (1, 2))
    sd_dn = jnp.tile(jnp.concatenate([zeros(hp), sin_p, zeros(rest)], axis=1), (1, 2))
    pad = LANES - MLA_ROPE_DIM
    cm = jnp.concatenate([cos_m, cos_m, zeros(pad)], axis=1)
    sm_up = jnp.concatenate([-sin_m, zeros(hm + pad)], axis=1)
    sm_dn = jnp.concatenate([zeros(hm), sin_m, zeros(pad)], axis=1)
    return jnp.concatenate([cd, sd_up, sd_dn, cm, sm_up, sm_dn], axis=1)


def _tile(seq, want):
    return min(seq, want)


def kernel(x, norm_mix, w_in, diff_q_norm, diff_k_norm, diff_lambda_q1, diff_lambda_k1,
           diff_lambda_q2, diff_lambda_k2, diff_subln, fox_q_norm, fox_k_norm, fox_forget_bias,
           mla_q_a_norm, mla_kv_a_norm, mla_w_q_up, mla_w_kv_up, mla_q_norm, mla_k_norm,
           w_out, norm_ffn, w_ff_up, w_ff_down):
    batch, seq, d = x.shape
    assert batch == 1 and d == D_MODEL and seq % LANES == 0
    tab = _rope_tables(seq)
    xs = x.reshape(seq, d)
    tq = _tile(seq, 512)
    assert sum(IN_SECTIONS) == w_in.shape[-1]
    w_out_b, w_up_b, w_down_b = w_out.astype(BF16), w_ff_up.astype(BF16), w_ff_down.astype(BF16)
    for l in range(DEPTH):
        proj = _in_proj(xs, norm_mix[l].reshape(1, d), w_in, l, tm=_tile(seq, 512))
        pvec = _pack_vec(diff_q_norm[l], diff_k_norm[l], fox_q_norm[l], fox_k_norm[l],
                         fox_forget_bias[l], mla_q_a_norm[l], mla_kv_a_norm[l],
                         mla_q_norm[l], mla_k_norm[l])
        (qd, kd, vd, qf, kf, vf, ck, qm, km, vm) = _prep(
            proj, tab, pvec, _pack_w_q_up(mla_w_q_up[l]), _pack_w_kv_up(mla_w_kv_up[l]),
            tm=_tile(seq, 256))
        lambda_init = 0.8 - 0.6 * math.exp(-0.3 * l)
        lam = jnp.stack([diff_lambda_q1[l], diff_lambda_k1[l], diff_lambda_q2[l], diff_lambda_k2[l]])
        o_a = _attention(qd, kd, vd, (lam, diff_subln[l].reshape(1, DIFF_V_DIM)), mode="diff",
                         tq=tq, lambda_init=lambda_init)
        o_b = _attention(qf, kf, vf, (ck,), mode="fox", tq=tq)
        o_c = _attention(qm, km, vm, (), mode="mla", tq=tq)
        xs = _out_proj(xs, o_a, o_b, o_c, w_out_b, l, tm=_tile(seq, 512))
        xs = _ffn(xs, norm_ffn[l].reshape(1, d), w_up_b, w_down_b, l, tm=_tile(seq, 512), tf=1024)
    return xs.reshape(batch, seq, d)
```

```python
import functools
import math

import jax
import jax.numpy as jnp
from jax import lax
from jax.experimental import pallas as pl
from jax.experimental.pallas import tpu as pltpu

D_MODEL = 2048
DEPTH = 2
N_HEADS_DIFF = 4
DIFF_QK_DIM = 64
DIFF_V_DIM = 128
N_HEADS_FOX = 6
FOX_HEAD_DIM = 128
N_HEADS_MLA = 6
MLA_Q_RANK = 512
MLA_KV_RANK = 256
MLA_NOPE_DIM = 128
MLA_ROPE_DIM = 64
MLA_QK_DIM = MLA_ROPE_DIM + MLA_NOPE_DIM
MLA_V_DIM = 128
D_FF = 4 * D_MODEL
ROPE_THETA = 500000.0
PARTIAL_ROT_DIM = DIFF_QK_DIM // 4
EPS = 1e-6

DIFF_Q_COLS = N_HEADS_DIFF * 2 * DIFF_QK_DIM
DIFF_V_COLS = N_HEADS_DIFF * DIFF_V_DIM
FOX_COLS = N_HEADS_FOX * FOX_HEAD_DIM
MLA_V_COLS = N_HEADS_MLA * MLA_V_DIM
IN_SECTIONS = (DIFF_Q_COLS, DIFF_Q_COLS, DIFF_V_COLS, FOX_COLS, FOX_COLS, FOX_COLS,
               N_HEADS_FOX, MLA_Q_RANK, MLA_KV_RANK + MLA_ROPE_DIM)

LANES = 128
BF16_ROWS = 16
MXU_COLS = 256
MLA_PAD_DIM = 2 * LANES
VMEM_LIMIT_BYTES = 56 * 1024 * 1024

OFF_DQ = 0
OFF_DK = OFF_DQ + DIFF_Q_COLS
OFF_DV = OFF_DK + DIFF_Q_COLS
OFF_FQ = OFF_DV + DIFF_V_COLS
OFF_FK = OFF_FQ + FOX_COLS
OFF_FV = OFF_FK + FOX_COLS
MAIN_COLS = OFF_FV + FOX_COLS
TAIL_SRC_COLS = N_HEADS_FOX + MLA_Q_RANK + MLA_KV_RANK + MLA_ROPE_DIM
TAIL_MQ = 0
TAIL_CKV = TAIL_MQ + MLA_Q_RANK
TAIL_KR = TAIL_CKV + MLA_KV_RANK
TAIL_FF = TAIL_KR + LANES
TAIL_COLS = TAIL_FF + LANES
IN_TILE = MAIN_COLS // 3
PROJ_COLS = MAIN_COLS + IN_TILE
assert TAIL_COLS <= IN_TILE and IN_TILE % LANES == 0

PV_G_DQ = 0
PV_G_DK = PV_G_DQ + LANES
PV_G_FQ = PV_G_DK + LANES
PV_G_FK = PV_G_FQ + LANES
PV_F_BIAS = PV_G_FK + LANES
PV_G_QA = PV_F_BIAS + LANES
PV_G_KVA = PV_G_QA + MLA_Q_RANK
PV_G_QM = PV_G_KVA + MLA_KV_RANK
PV_G_KM = PV_G_QM + MLA_PAD_DIM
PV_LEN = PV_G_KM + MLA_PAD_DIM

NEG_BIG = -1e30
LOG2E = math.log2(math.e)

BF16 = jnp.bfloat16
F32 = jnp.float32


def _params(*semantics):
    return pltpu.CompilerParams(dimension_semantics=semantics,
                                vmem_limit_bytes=VMEM_LIMIT_BYTES)


def _in_proj_kernel(x_ref, g_ref, w_ref, o_ref, wb_ref):
    j = pl.program_id(0)
    first_row_tile = pl.program_id(1) == 0
    n_main = MAIN_COLS // IN_TILE

    @pl.when(jnp.logical_and(first_row_tile, j < n_main))
    def _():
        wb_ref[...] = w_ref[...].astype(BF16)

    @pl.when(jnp.logical_and(first_row_tile, j == n_main))
    def _():
        width = TAIL_KR + LANES
        w = w_ref[:, 0:width]
        r = pltpu.roll(w, width - N_HEADS_FOX, 1)
        lane = lax.broadcasted_iota(jnp.int32, (w.shape[0], LANES), 1)
        wb_ref[:, TAIL_MQ:TAIL_KR] = r[:, TAIL_MQ:TAIL_KR].astype(BF16)
        wb_ref[:, TAIL_KR:TAIL_FF] = jnp.where(lane < MLA_ROPE_DIM, r[:, TAIL_KR:TAIL_FF], 0.0).astype(BF16)
        wb_ref[:, TAIL_FF:TAIL_COLS] = jnp.where(lane < N_HEADS_FOX, w[:, 0:LANES], 0.0).astype(BF16)
        wb_ref[:, TAIL_COLS:] = jnp.zeros((w.shape[0], IN_TILE - TAIL_COLS), BF16)

    x = x_ref[...]
    inv = lax.rsqrt(jnp.mean(x * x, axis=-1, keepdims=True) + EPS)
    h = (x * inv * g_ref[...]).astype(BF16)
    o_ref[...] = jnp.dot(h, wb_ref[...], preferred_element_type=F32)


def _in_proj(x, g, w_in, layer, *, tm):
    s, d = x.shape
    assert w_in.shape[-1] == MAIN_COLS + TAIL_SRC_COLS
    return pl.pallas_call(
        _in_proj_kernel,
        grid=(PROJ_COLS // IN_TILE, s // tm),
        in_specs=[pl.BlockSpec((tm, d), lambda j, i: (i, 0)),
                  pl.BlockSpec((1, d), lambda j, i: (0, 0)),
                  pl.BlockSpec((None, d, IN_TILE), lambda j, i: (layer, 0, j))],
        out_specs=pl.BlockSpec((tm, IN_TILE), lambda j, i: (i, j)),
        out_shape=jax.ShapeDtypeStruct((s, PROJ_COLS), F32),
        scratch_shapes=[pltpu.VMEM((d, IN_TILE), BF16)],
        compiler_params=_params("arbitrary", "arbitrary"),
        name="in_proj",
    )(x, g, w_in)


def _rms(x, n_valid):
    ss = jnp.sum(x * x, axis=-1, keepdims=True)
    return x * lax.rsqrt(ss * (1.0 / n_valid) + EPS)


def _rms_two_halves(x):
    lo = lax.broadcasted_iota(jnp.int32, x.shape, 1) < DIFF_QK_DIM
    x2 = x * x
    s_lo = jnp.sum(jnp.where(lo, x2, 0.0), axis=-1, keepdims=True)
    s_hi = jnp.sum(jnp.where(lo, 0.0, x2), axis=-1, keepdims=True)
    return x * lax.rsqrt(jnp.where(lo, s_lo, s_hi) * (1.0 / DIFF_QK_DIM) + EPS)


def _rope(x, c, s_up, s_dn, half):
    return x * c + pltpu.roll(x, LANES - half, 1) * s_up + pltpu.roll(x, half, 1) * s_dn


def _t_bf16(x):
    return x.T.astype(BF16)


def _store_v_t(ref, h, v):
    dv = v.shape[1]
    ref[h, 0:dv, :] = _t_bf16(v)
    ref[h, dv:, :] = jnp.ones((BF16_ROWS, v.shape[0]), BF16)


def _prep_kernel(proj_ref, tail_ref, tab_ref, pv_ref, wq_ref, wkv_ref,
                 qd_ref, kd_ref, vd_ref, qf_ref, kf_ref, vf_ref, ck_ref,
                 qm_ref, km_ref, vm_ref, carry_ref, *, tm):
    @pl.when(pl.program_id(0) == 0)
    def _():
        carry_ref[...] = jnp.zeros_like(carry_ref)

    def pv(off, n):
        return pv_ref[:, off:off + n]

    def lanes(ref, off, n=LANES):
        return ref[:, off:off + n]

    cd, sd_up, sd_dn, cm, sm_up, sm_dn = (lanes(tab_ref, k * LANES) for k in range(6))
    half_d = PARTIAL_ROT_DIM // 2
    half_m = MLA_ROPE_DIM // 2

    g_dq, g_dk = pv(PV_G_DQ, LANES), pv(PV_G_DK, LANES)
    for b in range(N_HEADS_DIFF):
        q = _rope(_rms_two_halves(lanes(proj_ref, OFF_DQ + b * LANES)) * g_dq, cd, sd_up, sd_dn, half_d)
        qd_ref[b] = _t_bf16(q * (DIFF_QK_DIM ** -0.5 * LOG2E))
        k = _rope(_rms_two_halves(lanes(proj_ref, OFF_DK + b * LANES)) * g_dk, cd, sd_up, sd_dn, half_d)
        kd_ref[:, b * LANES:(b + 1) * LANES] = k.astype(BF16)
        _store_v_t(vd_ref, b, lanes(proj_ref, OFF_DV + b * LANES))

    g_fq, g_fk = pv(PV_G_FQ, LANES), pv(PV_G_FK, LANES)
    for h in range(N_HEADS_FOX):
        q = _rms(lanes(proj_ref, OFF_FQ + h * LANES), FOX_HEAD_DIM) * g_fq
        qf_ref[h] = _t_bf16(q * (FOX_HEAD_DIM ** -0.5 * LOG2E))
        k = _rms(lanes(proj_ref, OFF_FK + h * LANES), FOX_HEAD_DIM) * g_fk
        kf_ref[:, h * LANES:(h + 1) * LANES] = k.astype(BF16)
        _store_v_t(vf_ref, h, lanes(proj_ref, OFF_FV + h * LANES))

    z = lanes(tail_ref, TAIL_FF) + pv(PV_F_BIAS, LANES)
    c = jnp.minimum(z, 0.0) - jnp.log1p(jnp.exp(-jnp.abs(z)))
    row = lax.broadcasted_iota(jnp.int32, c.shape, 0)
    shift = 1
    while shift < tm:
        c = c + jnp.where(row >= shift, pltpu.roll(c, shift, 0), 0.0)
        shift *= 2
    c = c + carry_ref[...]
    carry_ref[...] = c[tm - 1:tm, :]
    c2 = c * LOG2E
    for h in range(N_HEADS_FOX):
        ck_ref[:, h * LANES:(h + 1) * LANES] = jnp.broadcast_to(c2[:, h:h + 1], (tm, LANES))

    q_lat = (_rms(lanes(tail_ref, TAIL_MQ, MLA_Q_RANK), MLA_Q_RANK) * pv(PV_G_QA, MLA_Q_RANK)).astype(BF16)
    q_up = jnp.dot(q_lat, wq_ref[...], preferred_element_type=F32)
    c_kv = (_rms(lanes(tail_ref, TAIL_CKV, MLA_KV_RANK), MLA_KV_RANK) * pv(PV_G_KVA, MLA_KV_RANK)).astype(BF16)
    kv_up = jnp.dot(c_kv, wkv_ref[...], preferred_element_type=F32)
    g_qn, g_qr = pv(PV_G_QM, LANES), pv(PV_G_QM + LANES, LANES)
    g_kn, g_kr = pv(PV_G_KM, LANES), pv(PV_G_KM + LANES, LANES)
    k_rope = _rope(_rms(lanes(tail_ref, TAIL_KR), MLA_ROPE_DIM) * g_kr, cm, sm_up, sm_dn, half_m).astype(BF16)
    mla_scale = MLA_QK_DIM ** -0.5 * LOG2E
    v_off = N_HEADS_MLA * MLA_NOPE_DIM
    for h in range(N_HEADS_MLA):
        base = h * MLA_PAD_DIM
        q_nope = _rms(q_up[:, base:base + LANES], MLA_NOPE_DIM) * g_qn
        q_rope = _rope(_rms(q_up[:, base + LANES:base + 2 * LANES], MLA_ROPE_DIM) * g_qr,
                       cm, sm_up, sm_dn, half_m)
        qm_ref[h, 0:LANES, :] = _t_bf16(q_nope * mla_scale)
        qm_ref[h, LANES:2 * LANES, :] = _t_bf16(q_rope * mla_scale)
        k_nope = _rms(kv_up[:, h * LANES:(h + 1) * LANES], MLA_NOPE_DIM) * g_kn
        km_ref[:, base:base + LANES] = k_nope.astype(BF16)
        km_ref[:, base + LANES:base + 2 * LANES] = k_rope
        _store_v_t(vm_ref, h, kv_up[:, v_off + h * LANES:v_off + (h + 1) * LANES])


def _prep(proj, tab, pvec, wq, wkv, *, tm):
    s = proj.shape[0]
    tail_blk = pl.BlockSpec((tm, IN_TILE), lambda i: (i, MAIN_COLS // IN_TILE))
    row_blk = lambda n: pl.BlockSpec((tm, n), lambda i: (i, 0))
    full = lambda a: pl.BlockSpec(a.shape, lambda i: (0, 0))

    def head_t(n_heads, dim):
        return (jax.ShapeDtypeStruct((n_heads, dim, s), BF16),
                pl.BlockSpec((n_heads, dim, tm), lambda i: (0, 0, i)))

    def rows(n, dtype=BF16):
        return jax.ShapeDtypeStruct((s, n), dtype), row_blk(n)

    outs = [head_t(N_HEADS_DIFF, LANES), rows(DIFF_Q_COLS), head_t(N_HEADS_DIFF, DIFF_V_DIM + BF16_ROWS),
            head_t(N_HEADS_FOX, FOX_HEAD_DIM), rows(FOX_COLS), head_t(N_HEADS_FOX, FOX_HEAD_DIM + BF16_ROWS),
            rows(FOX_COLS, F32),
            head_t(N_HEADS_MLA, MLA_PAD_DIM), rows(N_HEADS_MLA * MLA_PAD_DIM),
            head_t(N_HEADS_MLA, MLA_V_DIM + BF16_ROWS)]
    return pl.pallas_call(
        functools.partial(_prep_kernel, tm=tm),
        grid=(s // tm,),
        in_specs=[row_blk(MAIN_COLS), tail_blk, row_blk(6 * LANES),
                  full(pvec), full(wq), full(wkv)],
        out_specs=[o[1] for o in outs],
        out_shape=[o[0] for o in outs],
        scratch_shapes=[pltpu.VMEM((1, LANES), F32)],
        compiler_params=_params("arbitrary"),
        name="prep",
    )(proj, proj, tab, pvec, wq, wkv)


def _attn_kernel(*refs, mode, tq, lambda_init):
    if mode == "diff":
        q_ref, k_ref, v_ref, lam_ref, sub_ref, o_ref, m_ref, acc_ref, s_ref = refs
    elif mode == "fox":
        q_ref, k_ref, v_ref, ck_ref, o_ref, m_ref, acc_ref, s_ref = refs
    else:
        q_ref, k_ref, v_ref, o_ref, m_ref, acc_ref, s_ref = refs
    i = pl.program_id(1)
    dv = o_ref.shape[1]
    cw = min(tq, MXU_COLS)

    m_ref[...] = jnp.full(m_ref.shape, NEG_BIG, F32)
    acc_ref[...] = jnp.zeros(acc_ref.shape, F32)

    q_t = q_ref[...]
    if mode == "diff":
        lo = lax.broadcasted_iota(jnp.int32, q_t.shape, 0) < DIFF_QK_DIM
        zero = jnp.zeros_like(q_t)
        q_streams = (jnp.where(lo, q_t, zero), jnp.where(lo, zero, q_t))
    else:
        q_streams = (q_t,)

    def scores(j, si, buf):
        start = pl.multiple_of(j * tq, tq)
        s_ref[buf] = jnp.dot(k_ref[pl.ds(start, tq), :], q_streams[si], preferred_element_type=F32)

    def softmax_pv(j, si, buf, masked):
        start = pl.multiple_of(j * tq, tq)
        v_t = v_ref[:, pl.ds(start, tq)]
        for c0 in range(0, tq, cw):
            cols = slice(c0, c0 + cw)
            s_t = s_ref[buf, :, cols]
            if mode == "fox":
                s_t = s_t - jnp.concatenate([ck_ref[pl.ds(start, tq), :]] * (cw // LANES), axis=1)
            if masked:
                key = lax.broadcasted_iota(jnp.int32, s_t.shape, 0)
                qry = lax.broadcasted_iota(jnp.int32, s_t.shape, 1) + c0
                s_t = jnp.where(key <= qry, s_t, NEG_BIG)
            m_prev = m_ref[si, :, cols]
            m_next = jnp.maximum(m_prev, jnp.max(s_t, axis=0, keepdims=True))
            p_t = jnp.exp2((s_t - m_next).astype(BF16))
            alpha = jnp.exp2(m_prev - m_next)
            m_ref[si, :, cols] = m_next
            acc_ref[si, :, cols] = (acc_ref[si, :, cols] * alpha
                                    + jnp.dot(v_t, p_t, preferred_element_type=F32))

    scores(0, 0, 0)
    if len(q_streams) == 2:
        def block(j, carry):
            scores(j, 1, 1)
            softmax_pv(j, 0, 0, False)
            scores(j + 1, 0, 0)
            softmax_pv(j, 1, 1, False)
            return carry

        lax.fori_loop(0, i, block, 0)
        scores(i, 1, 1)
        softmax_pv(i, 0, 0, True)
        softmax_pv(i, 1, 1, True)
    else:
        def pair(t, carry):
            scores(2 * t + 1, 0, 1)
            softmax_pv(2 * t, 0, 0, False)
            scores(2 * t + 2, 0, 0)
            softmax_pv(2 * t + 1, 0, 1, False)
            return carry

        lax.fori_loop(0, lax.shift_right_logical(i, 1), pair, 0)
        odd = lax.bitwise_and(i, 1) == 1

        @pl.when(odd)
        def _():
            scores(i, 0, 1)
            softmax_pv(i - 1, 0, 0, False)
            softmax_pv(i, 0, 1, True)

        @pl.when(jnp.logical_not(odd))
        def _():
            softmax_pv(i, 0, 0, True)

    def out_t(si):
        return acc_ref[si, 0:dv, :] * (1.0 / acc_ref[si, dv:dv + 1, :])

    if mode == "diff":
        lp = lam_ref[...]
        lam = (jnp.exp(jnp.sum(lp[0:1] * lp[1:2], axis=1, keepdims=True))
               - jnp.exp(jnp.sum(lp[2:3] * lp[3:4], axis=1, keepdims=True)) + lambda_init)
        o = (out_t(0) - lam * out_t(1)).T
        o = o * lax.rsqrt(jnp.mean(o * o, axis=-1, keepdims=True) + EPS) * sub_ref[...]
        o_ref[...] = (o * (1.0 - lambda_init)).astype(o_ref.dtype)
    else:
        o_ref[...] = out_t(0).T.astype(o_ref.dtype)


def _attention(q_t, k, v_t, extras, *, mode, tq, lambda_init=0.0):
    n_heads, dk, s = q_t.shape
    dv_aug = v_t.shape[1]
    dv = dv_aug - BF16_ROWS
    n_streams = 2 if mode == "diff" else 1
    in_specs = [pl.BlockSpec((None, dk, tq), lambda h, i: (h, 0, i)),
                pl.BlockSpec((s, dk), lambda h, i: (0, h)),
                pl.BlockSpec((None, dv_aug, s), lambda h, i: (h, 0, 0))]
    if mode == "diff":
        lam, sub = extras
        in_specs += [pl.BlockSpec(lam.shape, lambda h, i: (0, 0)),
                     pl.BlockSpec(sub.shape, lambda h, i: (0, 0))]
    elif mode == "fox":
        in_specs += [pl.BlockSpec((s, LANES), lambda h, i: (0, h))]
    return pl.pallas_call(
        functools.partial(_attn_kernel, mode=mode, tq=tq, lambda_init=lambda_init),
        grid=(n_heads, s // tq),
        in_specs=in_specs,
        out_specs=pl.BlockSpec((tq, dv), lambda h, i: (i, h)),
        out_shape=jax.ShapeDtypeStruct((s, n_heads * dv), BF16),
        scratch_shapes=[pltpu.VMEM((n_streams, 1, tq), F32),
                        pltpu.VMEM((n_streams, dv_aug, tq), F32),
                        pltpu.VMEM((2, tq, tq), F32)],
        compiler_params=_params("arbitrary", "arbitrary"),
        name="attn_" + mode,
    )(q_t, k, v_t, *extras)


def _out_proj_kernel(x_ref, oa_ref, ob_ref, oc_ref, w_ref, o_ref):
    a0, a1 = DIFF_V_COLS, DIFF_V_COLS + FOX_COLS
    acc = x_ref[...]
    acc = acc + jnp.dot(oa_ref[...], w_ref[0:a0, :], preferred_element_type=F32)
    acc = acc + jnp.dot(ob_ref[...], w_ref[a0:a1, :], preferred_element_type=F32)
    acc = acc + jnp.dot(oc_ref[...], w_ref[a1:, :], preferred_element_type=F32)
    o_ref[...] = acc


def _out_proj(x, oa, ob, oc, w, layer, *, tm):
    s, d = x.shape
    row_blk = lambda n: pl.BlockSpec((tm, n), lambda i: (i, 0))
    return pl.pallas_call(
        _out_proj_kernel,
        grid=(s // tm,),
        in_specs=[row_blk(d), row_blk(oa.shape[1]), row_blk(ob.shape[1]), row_blk(oc.shape[1]),
                  pl.BlockSpec((None,) + w.shape[1:], lambda i: (layer, 0, 0))],
        out_specs=row_blk(d),
        out_shape=jax.ShapeDtypeStruct((s, d), F32),
        compiler_params=_params("arbitrary"),
        name="out_proj",
    )(x, oa, ob, oc, w)


def _ffn_kernel(x_ref, g_ref, wu_ref, wd_ref, o_ref, h_ref):
    @pl.when(pl.program_id(1) == 0)
    def _():
        x = x_ref[...]
        inv = lax.rsqrt(jnp.mean(x * x, axis=-1, keepdims=True) + EPS)
        h_ref[...] = (x * inv * g_ref[...]).astype(BF16)
        o_ref[...] = x

    u = jnp.dot(h_ref[...], wu_ref[...], preferred_element_type=F32)
    a = jnp.square(jnp.maximum(u, 0.0)).astype(BF16)
    o_ref[...] += jnp.dot(a, wd_ref[...], preferred_element_type=F32)


def _ffn(x, g, wu, wd, layer, *, tm, tf):
    s, d = x.shape
    f = wu.shape[2]
    return pl.pallas_call(
        _ffn_kernel,
        grid=(s // tm, f // tf),
        in_specs=[pl.BlockSpec((tm, d), lambda i, j: (i, 0)),
                  pl.BlockSpec((1, d), lambda i, j: (0, 0)),
                  pl.BlockSpec((None, d, tf), lambda i, j: (layer, 0, j)),
                  pl.BlockSpec((None, tf, d), lambda i, j: (layer, j, 0))],
        out_specs=pl.BlockSpec((tm, d), lambda i, j: (i, 0)),
        out_shape=jax.ShapeDtypeStruct((s, d), F32),
        scratch_shapes=[pltpu.VMEM((tm, d), BF16)],
        compiler_params=_params("arbitrary", "arbitrary"),
        name="ffn",
    )(x, g, wu, wd)


def _pack_w_q_up(w):
    w = w.reshape(MLA_Q_RANK, N_HEADS_MLA, MLA_QK_DIM)
    pad = jnp.zeros((MLA_Q_RANK, N_HEADS_MLA, MLA_PAD_DIM - MLA_QK_DIM), w.dtype)
    w = jnp.concatenate([w[:, :, MLA_ROPE_DIM:], w[:, :, :MLA_ROPE_DIM], pad], axis=2)
    return w.reshape(MLA_Q_RANK, N_HEADS_MLA * MLA_PAD_DIM).astype(BF16)


def _pack_w_kv_up(w):
    w = w.reshape(MLA_KV_RANK, N_HEADS_MLA, MLA_NOPE_DIM + MLA_V_DIM)
    k_nope = w[:, :, :MLA_NOPE_DIM].reshape(MLA_KV_RANK, N_HEADS_MLA * MLA_NOPE_DIM)
    v = w[:, :, MLA_NOPE_DIM:].reshape(MLA_KV_RANK, MLA_V_COLS)
    return jnp.concatenate([k_nope, v], axis=1).astype(BF16)


def _pack_vec(diff_q_norm, diff_k_norm, fox_q_norm, fox_k_norm, fox_forget_bias,
              mla_q_a_norm, mla_kv_a_norm, mla_q_norm, mla_k_norm):
    def nope_rope_pad(g):
        return jnp.concatenate([g[MLA_ROPE_DIM:], g[:MLA_ROPE_DIM],
                                jnp.zeros((MLA_PAD_DIM - MLA_QK_DIM,), g.dtype)])
    parts = [jnp.tile(diff_q_norm, 2), jnp.tile(diff_k_norm, 2), fox_q_norm, fox_k_norm,
             jnp.pad(fox_forget_bias, (0, LANES - N_HEADS_FOX)),
             mla_q_a_norm, mla_kv_a_norm, nope_rope_pad(mla_q_norm), nope_rope_pad(mla_k_norm)]
    return jnp.concatenate(parts).astype(F32).reshape(1, PV_LEN)


def _rope_tables(seq):
    def cos_sin(rot_dim):
        half = rot_dim // 2
        inv_freq = ROPE_THETA ** (-jnp.arange(half, dtype=F32) / half)
        ang = jnp.arange(seq, dtype=F32)[:, None] * inv_freq[None, :]
        return jnp.cos(ang), jnp.sin(ang)

    cos_p, sin_p = cos_sin(PARTIAL_ROT_DIM)
    cos_m, sin_m = cos_sin(MLA_ROPE_DIM)
    hp, hm = PARTIAL_ROT_DIM // 2, MLA_ROPE_DIM // 2
    ones = lambda n: jnp.ones((seq, n), F32)
    zeros = lambda n: jnp.zeros((seq, n), F32)
    rest = DIFF_QK_DIM - PARTIAL_ROT_DIM
    cd = jnp.tile(jnp.concatenate([cos_p, cos_p, ones(rest)], axis=1), (1, 2))
    sd_up = jnp.tile(jnp.concatenate([-sin_p, zeros(hp + rest)], axis=1), (1, 2))
    sd_dn = jnp.tile(jnp.concatenate([zeros(hp), sin_p, zeros(rest)], axis=1), (1, 2))
    pad = LANES - MLA_ROPE_DIM
    cm = jnp.concatenate([cos_m, cos_m, zeros(pad)], axis=1)
    sm_up = jnp.concatenate([-sin_m, zeros(hm + pad)], axis=1)
    sm_dn = jnp.concatenate([zeros(hm), sin_m, zeros(pad)], axis=1)
    return jnp.concatenate([cd, sd_up, sd_dn, cm, sm_up, sm_dn], axis=1)


def _tile(seq, want):
    return min(seq, want)


def kernel(x, norm_mix, w_in, diff_q_norm, diff_k_norm, diff_lambda_q1, diff_lambda_k1,
           diff_lambda_q2, diff_lambda_k2, diff_subln, fox_q_norm, fox_k_norm, fox_forget_bias,
           mla_q_a_norm, mla_kv_a_norm, mla_w_q_up, mla_w_kv_up, mla_q_norm, mla_k_norm,
           w_out, norm_ffn, w_ff_up, w_ff_down):
    batch, seq, d = x.shape
    assert batch == 1 and d == D_MODEL and seq % LANES == 0
    tab = _rope_tables(seq)
    xs = x.reshape(seq, d)
    tq = _tile(seq, 512)
    assert sum(IN_SECTIONS) == w_in.shape[-1]
    w_out_b, w_up_b, w_down_b = w_out.astype(BF16), w_ff_up.astype(BF16), w_ff_down.astype(BF16)
    for l in range(DEPTH):
        proj = _in_proj(xs, norm_mix[l].reshape(1, d), w_in, l, tm=_tile(seq, 512))
        pvec = _pack_vec(diff_q_norm[l], diff_k_norm[l], fox_q_norm[l], fox_k_norm[l],
                         fox_forget_bias[l], mla_q_a_norm[l], mla_kv_a_norm[l],
                         mla_q_norm[l], mla_k_norm[l])
        (qd, kd, vd, qf, kf, vf, ck, qm, km, vm) = _prep(
            proj, tab, pvec, _pack_w_q_up(mla_w_q_up[l]), _pack_w_kv_up(mla_w_kv_up[l]),
            tm=_tile(seq, 256))
        lambda_init = 0.8 - 0.6 * math.exp(-0.3 * l)
        lam = jnp.stack([diff_lambda_q1[l], diff_lambda_k1[l], diff_lambda_q2[l], diff_lambda_k2[l]])
        o_a = _attention(qd, kd, vd, (lam, diff_subln[l].reshape(1, DIFF_V_DIM)), mode="diff",
                         tq=tq, lambda_init=lambda_init)
        o_b = _attention(qf, kf, vf, (ck,), mode="fox", tq=tq)
        o_c = _attention(qm, km, vm, (), mode="mla", tq=tq)
        xs = _out_proj(xs, o_a, o_b, o_c, w_out_b, l, tm=_tile(seq, 512))
        xs = _ffn(xs, norm_ffn[l].reshape(1, d), w_up_b, w_down_b, l, tm=_tile(seq, 512), tf=1024)
    return xs.reshape(batch, seq, d)
```

```python
import functools
import math

import jax
import jax.numpy as jnp
from jax import lax
from jax.experimental import pallas as pl
from jax.experimental.pallas import tpu as pltpu

D_MODEL = 2048
DEPTH = 2
N_HEADS_DIFF = 4
DIFF_QK_DIM = 64
DIFF_V_DIM = 128
N_HEADS_FOX = 6
FOX_HEAD_DIM = 128
N_HEADS_MLA = 6
MLA_Q_RANK = 512
MLA_KV_RANK = 256
MLA_NOPE_DIM = 128
MLA_ROPE_DIM = 64
MLA_QK_DIM = MLA_ROPE_DIM + MLA_NOPE_DIM
MLA_V_DIM = 128
D_FF = 4 * D_MODEL
ROPE_THETA = 500000.0
PARTIAL_ROT_DIM = DIFF_QK_DIM // 4
EPS = 1e-6

DIFF_Q_COLS = N_HEADS_DIFF * 2 * DIFF_QK_DIM
DIFF_V_COLS = N_HEADS_DIFF * DIFF_V_DIM
FOX_COLS = N_HEADS_FOX * FOX_HEAD_DIM
MLA_V_COLS = N_HEADS_MLA * MLA_V_DIM
IN_SECTIONS = (DIFF_Q_COLS, DIFF_Q_COLS, DIFF_V_COLS, FOX_COLS, FOX_COLS, FOX_COLS,
               N_HEADS_FOX, MLA_Q_RANK, MLA_KV_RANK + MLA_ROPE_DIM)

LANES = 128
BF16_ROWS = 16
MXU_COLS = 256
UNITS_PER_TRIP = 4
MLA_PAD_DIM = 2 * LANES
VMEM_LIMIT_BYTES = 56 * 1024 * 1024

OFF_DQ = 0
OFF_DK = OFF_DQ + DIFF_Q_COLS
OFF_DV = OFF_DK + DIFF_Q_COLS
OFF_FQ = OFF_DV + DIFF_V_COLS
OFF_FK = OFF_FQ + FOX_COLS
OFF_FV = OFF_FK + FOX_COLS
MAIN_COLS = OFF_FV + FOX_COLS
TAIL_SRC_COLS = N_HEADS_FOX + MLA_Q_RANK + MLA_KV_RANK + MLA_ROPE_DIM
TAIL_MQ = 0
TAIL_CKV = TAIL_MQ + MLA_Q_RANK
TAIL_KR = TAIL_CKV + MLA_KV_RANK
TAIL_FF = TAIL_KR + LANES
TAIL_COLS = TAIL_FF + LANES
IN_TILE = MAIN_COLS // 3
PROJ_COLS = MAIN_COLS + IN_TILE
assert TAIL_COLS <= IN_TILE and IN_TILE % LANES == 0

PV_G_DQ = 0
PV_G_DK = PV_G_DQ + LANES
PV_G_FQ = PV_G_DK + LANES
PV_G_FK = PV_G_FQ + LANES
PV_F_BIAS = PV_G_FK + LANES
PV_G_QA = PV_F_BIAS + LANES
PV_G_KVA = PV_G_QA + MLA_Q_RANK
PV_G_QM = PV_G_KVA + MLA_KV_RANK
PV_G_KM = PV_G_QM + MLA_PAD_DIM
PV_LEN = PV_G_KM + MLA_PAD_DIM

NEG_BIG = -1e30
LOG2E = math.log2(math.e)

BF16 = jnp.bfloat16
F32 = jnp.float32


def _params(*semantics):
    return pltpu.CompilerParams(dimension_semantics=semantics,
                                vmem_limit_bytes=VMEM_LIMIT_BYTES)


def _in_proj_kernel(x_ref, g_ref, w_ref, o_ref, wb_ref):
    j = pl.program_id(0)
    first_row_tile = pl.program_id(1) == 0
    n_main = MAIN_COLS // IN_TILE

    @pl.when(jnp.logical_and(first_row_tile, j < n_main))
    def _():
        wb_ref[...] = w_ref[...].astype(BF16)

    @pl.when(jnp.logical_and(first_row_tile, j == n_main))
    def _():
        width = TAIL_KR + LANES
        w = w_ref[:, 0:width]
        r = pltpu.roll(w, width - N_HEADS_FOX, 1)
        lane = lax.broadcasted_iota(jnp.int32, (w.shape[0], LANES), 1)
        wb_ref[:, TAIL_MQ:TAIL_KR] = r[:, TAIL_MQ:TAIL_KR].astype(BF16)
        wb_ref[:, TAIL_KR:TAIL_FF] = jnp.where(lane < MLA_ROPE_DIM, r[:, TAIL_KR:TAIL_FF], 0.0).astype(BF16)
        wb_ref[:, TAIL_FF:TAIL_COLS] = jnp.where(lane < N_HEADS_FOX, w[:, 0:LANES], 0.0).astype(BF16)
        wb_ref[:, TAIL_COLS:] = jnp.zeros((w.shape[0], IN_TILE - TAIL_COLS), BF16)

    x = x_ref[...]
    inv = lax.rsqrt(jnp.mean(x * x, axis=-1, keepdims=True) + EPS)
    h = (x * inv * g_ref[...]).astype(BF16)
    o_ref[...] = jnp.dot(h, wb_ref[...], preferred_element_type=F32)


def _in_proj(x, g, w_in, layer, *, tm):
    s, d = x.shape
    assert w_in.shape[-1] == MAIN_COLS + TAIL_SRC_COLS
    return pl.pallas_call(
        _in_proj_kernel,
        grid=(PROJ_COLS // IN_TILE, s // tm),
        in_specs=[pl.BlockSpec((tm, d), lambda j, i: (i, 0)),
                  pl.BlockSpec((1, d), lambda j, i: (0, 0)),
                  pl.BlockSpec((None, d, IN_TILE), lambda j, i: (layer, 0, j))],
        out_specs=pl.BlockSpec((tm, IN_TILE), lambda j, i: (i, j)),
        out_shape=jax.ShapeDtypeStruct((s, PROJ_COLS), F32),
        scratch_shapes=[pltpu.VMEM((d, IN_TILE), BF16)],
        compiler_params=_params("arbitrary", "arbitrary"),
        name="in_proj",
    )(x, g, w_in)


def _rms(x, n_valid):
    ss = jnp.sum(x * x, axis=-1, keepdims=True)
    return x * lax.rsqrt(ss * (1.0 / n_valid) + EPS)


def _rms_two_halves(x):
    lo = lax.broadcasted_iota(jnp.int32, x.shape, 1) < DIFF_QK_DIM
    x2 = x * x
    s_lo = jnp.sum(jnp.where(lo, x2, 0.0), axis=-1, keepdims=True)
    s_hi = jnp.sum(jnp.where(lo, 0.0, x2), axis=-1, keepdims=True)
    return x * lax.rsqrt(jnp.where(lo, s_lo, s_hi) * (1.0 / DIFF_QK_DIM) + EPS)


def _rope(x, c, s_up, s_dn, half):
    return x * c + pltpu.roll(x, LANES - half, 1) * s_up + pltpu.roll(x, half, 1) * s_dn


def _t_bf16(x):
    return x.T.astype(BF16)


def _store_v_t(ref, h, v):
    dv = v.shape[1]
    ref[h, 0:dv, :] = _t_bf16(v)
    if ref.shape[1] > dv:
        ref[h, dv:, :] = jnp.ones((ref.shape[1] - dv, v.shape[0]), BF16)


def _prep_kernel(proj_ref, tail_ref, tab_ref, pv_ref, wq_ref, wkv_ref,
                 qd_ref, kd_ref, vd_ref, qf_ref, kf_ref, vf_ref, ck_ref,
                 qm_ref, km_ref, vm_ref, carry_ref, *, tm):
    @pl.when(pl.program_id(0) == 0)
    def _():
        carry_ref[...] = jnp.zeros_like(carry_ref)

    def pv(off, n):
        return pv_ref[:, off:off + n]

    def lanes(ref, off, n=LANES):
        return ref[:, off:off + n]

    cd, sd_up, sd_dn, cm, sm_up, sm_dn = (lanes(tab_ref, k * LANES) for k in range(6))
    half_d = PARTIAL_ROT_DIM // 2
    half_m = MLA_ROPE_DIM // 2

    g_dq, g_dk = pv(PV_G_DQ, LANES), pv(PV_G_DK, LANES)
    for b in range(N_HEADS_DIFF):
        q = _rope(_rms_two_halves(lanes(proj_ref, OFF_DQ + b * LANES)) * g_dq, cd, sd_up, sd_dn, half_d)
        qd_ref[b] = _t_bf16(q * (DIFF_QK_DIM ** -0.5 * LOG2E))
        k = _rope(_rms_two_halves(lanes(proj_ref, OFF_DK + b * LANES)) * g_dk, cd, sd_up, sd_dn, half_d)
        kd_ref[:, b * LANES:(b + 1) * LANES] = k.astype(BF16)
        _store_v_t(vd_ref, b, lanes(proj_ref, OFF_DV + b * LANES))

    g_fq, g_fk = pv(PV_G_FQ, LANES), pv(PV_G_FK, LANES)
    for h in range(N_HEADS_FOX):
        q = _rms(lanes(proj_ref, OFF_FQ + h * LANES), FOX_HEAD_DIM) * g_fq
        qf_ref[h] = _t_bf16(q * (FOX_HEAD_DIM ** -0.5 * LOG2E))
        k = _rms(lanes(proj_ref, OFF_FK + h * LANES), FOX_HEAD_DIM) * g_fk
        kf_ref[:, h * LANES:(h + 1) * LANES] = k.astype(BF16)
        _store_v_t(vf_ref, h, lanes(proj_ref, OFF_FV + h * LANES))

    z = lanes(tail_ref, TAIL_FF) + pv(PV_F_BIAS, LANES)
    c = jnp.minimum(z, 0.0) - jnp.log1p(jnp.exp(-jnp.abs(z)))
    row = lax.broadcasted_iota(jnp.int32, c.shape, 0)
    shift = 1
    while shift < tm:
        c = c + jnp.where(row >= shift, pltpu.roll(c, shift, 0), 0.0)
        shift *= 2
    c = c + carry_ref[...]
    carry_ref[...] = c[tm - 1:tm, :]
    c2 = c * LOG2E
    for h in range(N_HEADS_FOX):
        ck_ref[:, h * LANES:(h + 1) * LANES] = jnp.broadcast_to(c2[:, h:h + 1], (tm, LANES))

    q_lat = (_rms(lanes(tail_ref, TAIL_MQ, MLA_Q_RANK), MLA_Q_RANK) * pv(PV_G_QA, MLA_Q_RANK)).astype(BF16)
    q_up = jnp.dot(q_lat, wq_ref[...], preferred_element_type=F32)
    c_kv = (_rms(lanes(tail_ref, TAIL_CKV, MLA_KV_RANK), MLA_KV_RANK) * pv(PV_G_KVA, MLA_KV_RANK)).astype(BF16)
    kv_up = jnp.dot(c_kv, wkv_ref[...], preferred_element_type=F32)
    g_qn, g_qr = pv(PV_G_QM, LANES), pv(PV_G_QM + LANES, LANES)
    g_kn, g_kr = pv(PV_G_KM, LANES), pv(PV_G_KM + LANES, LANES)
    k_rope = _rope(_rms(lanes(tail_ref, TAIL_KR), MLA_ROPE_DIM) * g_kr, cm, sm_up, sm_dn, half_m).astype(BF16)
    mla_scale = MLA_QK_DIM ** -0.5 * LOG2E
    v_off = N_HEADS_MLA * MLA_NOPE_DIM
    for h in range(N_HEADS_MLA):
        base = h * MLA_PAD_DIM
        q_nope = _rms(q_up[:, base:base + LANES], MLA_NOPE_DIM) * g_qn
        q_rope = _rope(_rms(q_up[:, base + LANES:base + 2 * LANES], MLA_ROPE_DIM) * g_qr,
                       cm, sm_up, sm_dn, half_m)
        qm_ref[h, 0:LANES, :] = _t_bf16(q_nope * mla_scale)
        qm_ref[h, LANES:2 * LANES, :] = _t_bf16(q_rope * mla_scale)
        k_nope = _rms(kv_up[:, h * LANES:(h + 1) * LANES], MLA_NOPE_DIM) * g_kn
        km_ref[:, base:base + LANES] = k_nope.astype(BF16)
        km_ref[:, base + LANES:base + 2 * LANES] = k_rope
        _store_v_t(vm_ref, h, kv_up[:, v_off + h * LANES:v_off + (h + 1) * LANES])


def _prep(proj, tab, pvec, wq, wkv, *, tm):
    s = proj.shape[0]
    tail_blk = pl.BlockSpec((tm, IN_TILE), lambda i: (i, MAIN_COLS // IN_TILE))
    row_blk = lambda n: pl.BlockSpec((tm, n), lambda i: (i, 0))
    full = lambda a: pl.BlockSpec(a.shape, lambda i: (0, 0))

    def head_t(n_heads, dim):
        return (jax.ShapeDtypeStruct((n_heads, dim, s), BF16),
                pl.BlockSpec((n_heads, dim, tm), lambda i: (0, 0, i)))

    def rows(n, dtype=BF16):
        return jax.ShapeDtypeStruct((s, n), dtype), row_blk(n)

    outs = [head_t(N_HEADS_DIFF, LANES), rows(DIFF_Q_COLS), head_t(N_HEADS_DIFF, DIFF_V_DIM),
            head_t(N_HEADS_FOX, FOX_HEAD_DIM), rows(FOX_COLS), head_t(N_HEADS_FOX, FOX_HEAD_DIM + BF16_ROWS),
            rows(FOX_COLS, F32),
            head_t(N_HEADS_MLA, MLA_PAD_DIM), rows(N_HEADS_MLA * MLA_PAD_DIM),
            head_t(N_HEADS_MLA, MLA_V_DIM)]
    return pl.pallas_call(
        functools.partial(_prep_kernel, tm=tm),
        grid=(s // tm,),
        in_specs=[row_blk(MAIN_COLS), tail_blk, row_blk(6 * LANES),
                  full(pvec), full(wq), full(wkv)],
        out_specs=[o[1] for o in outs],
        out_shape=[o[0] for o in outs],
        scratch_shapes=[pltpu.VMEM((1, LANES), F32)],
        compiler_params=_params("arbitrary"),
        name="prep",
    )(proj, proj, tab, pvec, wq, wkv)


def _attn_kernel(*refs, mode, tq, lambda_init, fused_row_sum):
    if mode == "diff":
        q_ref, k_ref, v_ref, lam_ref, sub_ref, o_ref, m_ref, l_ref, acc_ref, s_ref = refs
    elif mode == "fox":
        q_ref, k_ref, v_ref, ck_ref, o_ref, m_ref, l_ref, acc_ref, s_ref = refs
    else:
        q_ref, k_ref, v_ref, o_ref, m_ref, l_ref, acc_ref, s_ref = refs
    n = pl.program_id(1)
    dv = o_ref.shape[1]
    cw = min(tq, MXU_COLS)

    m_ref[...] = jnp.full(m_ref.shape, NEG_BIG, F32)
    l_ref[...] = jnp.zeros(l_ref.shape, F32)
    acc_ref[...] = jnp.zeros(acc_ref.shape, F32)

    q_t = q_ref[...]
    if mode == "diff":
        lo = lax.broadcasted_iota(jnp.int32, q_t.shape, 0) < DIFF_QK_DIM
        zero = jnp.zeros_like(q_t)
        q_streams = (jnp.where(lo, q_t, zero), jnp.where(lo, zero, q_t))
    else:
        q_streams = (q_t,)

    def scores(j, si, buf):
        start = pl.multiple_of(j * tq, tq)
        s_ref[buf] = jnp.dot(k_ref[pl.ds(start, tq), :], q_streams[si], preferred_element_type=F32)

    def softmax_pv(j, si, buf, masked):
        start = pl.multiple_of(j * tq, tq)
        v_t = v_ref[:, pl.ds(start, tq)]
        for c0 in range(0, tq, cw):
            cols = slice(c0, c0 + cw)
            s_t = s_ref[buf, :, cols]
            if mode == "fox":
                s_t = s_t - jnp.concatenate([ck_ref[pl.ds(start, tq), :]] * (cw // LANES), axis=1)
            if masked:
                key = lax.broadcasted_iota(jnp.int32, s_t.shape, 0)
                qry = lax.broadcasted_iota(jnp.int32, s_t.shape, 1) + c0
                s_t = jnp.where(key <= qry, s_t, NEG_BIG)
            m_prev = m_ref[si, :, cols]
            m_next = jnp.maximum(m_prev, jnp.max(s_t, axis=0, keepdims=True))
            alpha = jnp.exp2(m_prev - m_next)
            m_ref[si, :, cols] = m_next
            if fused_row_sum:
                p_t = jnp.exp2((s_t - m_next).astype(BF16))
            else:
                p_f32 = jnp.exp2(s_t - m_next)
                l_ref[si, :, cols] = alpha * l_ref[si, :, cols] + jnp.sum(p_f32, axis=0, keepdims=True)
                p_t = p_f32.astype(BF16)
            acc_ref[si, :, cols] = (acc_ref[si, :, cols] * alpha
                                    + jnp.dot(v_t, p_t, preferred_element_type=F32))

    def run(units, following):
        seq = units + ([following] if following is not None else [])
        for pos, (j, si, masked) in enumerate(units):
            if pos + 1 < len(seq):
                scores(seq[pos + 1][0], seq[pos + 1][1], (pos + 1) % 2)
            softmax_pv(j, si, pos % 2, masked)

    n_streams = len(q_streams)
    blocks_per_trip = UNITS_PER_TRIP // n_streams
    shift = blocks_per_trip.bit_length() - 1

    def block_units(j, masked):
        return [(j, si, masked) for si in range(n_streams)]

    def trip(t, carry):
        base = t * blocks_per_trip
        units = [u for b in range(blocks_per_trip) for u in block_units(base + b, False)]
        run(units, (base + blocks_per_trip, 0, False))
        return carry

    scores(0, 0, 0)
    lax.fori_loop(0, lax.shift_right_logical(n, shift), trip, 0)
    left = lax.bitwise_and(n, blocks_per_trip - 1)
    for r in range(blocks_per_trip):
        @pl.when(left == r)
        def _():
            units = [u for b in range(r) for u in block_units(n - r + b, False)]
            run(units + block_units(n, True), None)

    def out_t(si):
        row_sum = acc_ref[si, dv:dv + 1, :] if fused_row_sum else l_ref[si]
        return acc_ref[si, 0:dv, :] * (1.0 / row_sum)

    if mode == "diff":
        lp = lam_ref[...]
        lam = (jnp.exp(jnp.sum(lp[0:1] * lp[1:2], axis=1, keepdims=True))
               - jnp.exp(jnp.sum(lp[2:3] * lp[3:4], axis=1, keepdims=True)) + lambda_init)
        o = (out_t(0) - lam * out_t(1)).T
        o = o * lax.rsqrt(jnp.mean(o * o, axis=-1, keepdims=True) + EPS) * sub_ref[...]
        o_ref[...] = (o * (1.0 - lambda_init)).astype(o_ref.dtype)
    else:
        o_ref[...] = out_t(0).T.astype(o_ref.dtype)


def _attention(q_t, k, v_t, extras, *, mode, tq, lambda_init=0.0):
    n_heads, dk, s = q_t.shape
    dv_aug = v_t.shape[1]
    dv = DIFF_V_DIM
    fused_row_sum = dv_aug != dv
    assert dv_aug in (dv, dv + BF16_ROWS)
    n_streams = 2 if mode == "diff" else 1
    in_specs = [pl.BlockSpec((None, dk, tq), lambda h, i: (h, 0, i)),
                pl.BlockSpec((s, dk), lambda h, i: (0, h)),
                pl.BlockSpec((None, dv_aug, s), lambda h, i: (h, 0, 0))]
    if mode == "diff":
        lam, sub = extras
        in_specs += [pl.BlockSpec(lam.shape, lambda h, i: (0, 0)),
                     pl.BlockSpec(sub.shape, lambda h, i: (0, 0))]
    elif mode == "fox":
        in_specs += [pl.BlockSpec((s, LANES), lambda h, i: (0, h))]
    return pl.pallas_call(
        functools.partial(_attn_kernel, mode=mode, tq=tq, lambda_init=lambda_init,
                          fused_row_sum=fused_row_sum),
        grid=(n_heads, s // tq),
        in_specs=in_specs,
        out_specs=pl.BlockSpec((tq, dv), lambda h, i: (i, h)),
        out_shape=jax.ShapeDtypeStruct((s, n_heads * dv), BF16),
        scratch_shapes=[pltpu.VMEM((n_streams, 1, tq), F32),
                        pltpu.VMEM((n_streams, 1, tq), F32),
                        pltpu.VMEM((n_streams, dv_aug, tq), F32),
                        pltpu.VMEM((2, tq, tq), F32)],
        compiler_params=_params("arbitrary", "arbitrary"),
        name="attn_" + mode,
    )(q_t, k, v_t, *extras)


def _out_proj_kernel(x_ref, oa_ref, ob_ref, oc_ref, w_ref, o_ref):
    a0, a1 = DIFF_V_COLS, DIFF_V_COLS + FOX_COLS
    acc = x_ref[...]
    acc = acc + jnp.dot(oa_ref[...], w_ref[0:a0, :], preferred_element_type=F32)
    acc = acc + jnp.dot(ob_ref[...], w_ref[a0:a1, :], preferred_element_type=F32)
    acc = acc + jnp.dot(oc_ref[...], w_ref[a1:, :], preferred_element_type=F32)
    o_ref[...] = acc


def _out_proj(x, oa, ob, oc, w, layer, *, tm):
    s, d = x.shape
    row_blk = lambda n: pl.BlockSpec((tm, n), lambda i: (i, 0))
    return pl.pallas_call(
        _out_proj_kernel,
        grid=(s // tm,),
        in_specs=[row_blk(d), row_blk(oa.shape[1]), row_blk(ob.shape[1]), row_blk(oc.shape[1]),
                  pl.BlockSpec((None,) + w.shape[1:], lambda i: (layer, 0, 0))],
        out_specs=row_blk(d),
        out_shape=jax.ShapeDtypeStruct((s, d), F32),
        compiler_params=_params("arbitrary"),
        name="out_proj",
    )(x, oa, ob, oc, w)


def _ffn_kernel(x_ref, g_ref, wu_ref, wd_ref, o_ref, h_ref):
    @pl.when(pl.program_id(1) == 0)
    def _():
        x = x_ref[...]
        inv = lax.rsqrt(jnp.mean(x * x, axis=-1, keepdims=True) + EPS)
        h_ref[...] = (x * inv * g_ref[...]).astype(BF16)
        o_ref[...] = x

    u = jnp.dot(h_ref[...], wu_ref[...], preferred_element_type=F32)
    a = jnp.square(jnp.maximum(u, 0.0)).astype(BF16)
    o_ref[...] += jnp.dot(a, wd_ref[...], preferred_element_type=F32)


def _ffn(x, g, wu, wd, layer, *, tm, tf):
    s, d = x.shape
    f = wu.shape[2]
    return pl.pallas_call(
        _ffn_kernel,
        grid=(s // tm, f // tf),
        in_specs=[pl.BlockSpec((tm, d), lambda i, j: (i, 0)),
                  pl.BlockSpec((1, d), lambda i, j: (0, 0)),
                  pl.BlockSpec((None, d, tf), lambda i, j: (layer, 0, j)),
                  pl.BlockSpec((None, tf, d), lambda i, j: (layer, j, 0))],
        out_specs=pl.BlockSpec((tm, d), lambda i, j: (i, 0)),
        out_shape=jax.ShapeDtypeStruct((s, d), F32),
        scratch_shapes=[pltpu.VMEM((tm, d), BF16)],
        compiler_params=_params("arbitrary", "arbitrary"),
        name="ffn",
    )(x, g, wu, wd)


def _pack_w_q_up(w):
    w = w.reshape(MLA_Q_RANK, N_HEADS_MLA, MLA_QK_DIM)
    pad = jnp.zeros((MLA_Q_RANK, N_HEADS_MLA, MLA_PAD_DIM - MLA_QK_DIM), w.dtype)
    w = jnp.concatenate([w[:, :, MLA_ROPE_DIM:], w[:, :, :MLA_ROPE_DIM], pad], axis=2)
    return w.reshape(MLA_Q_RANK, N_HEADS_MLA * MLA_PAD_DIM).astype(BF16)


def _pack_w_kv_up(w):
    w = w.reshape(MLA_KV_RANK, N_HEADS_MLA, MLA_NOPE_DIM + MLA_V_DIM)
    k_nope = w[:, :, :MLA_NOPE_DIM].reshape(MLA_KV_RANK, N_HEADS_MLA * MLA_NOPE_DIM)
    v = w[:, :, MLA_NOPE_DIM:].reshape(MLA_KV_RANK, MLA_V_COLS)
    return jnp.concatenate([k_nope, v], axis=1).astype(BF16)


def _pack_vec(diff_q_norm, diff_k_norm, fox_q_norm, fox_k_norm, fox_forget_bias,
              mla_q_a_norm, mla_kv_a_norm, mla_q_norm, mla_k_norm):
    def nope_rope_pad(g):
        return jnp.concatenate([g[MLA_ROPE_DIM:], g[:MLA_ROPE_DIM],
                                jnp.zeros((MLA_PAD_DIM - MLA_QK_DIM,), g.dtype)])
    parts = [jnp.tile(diff_q_norm, 2), jnp.tile(diff_k_norm, 2), fox_q_norm, fox_k_norm,
             jnp.pad(fox_forget_bias, (0, LANES - N_HEADS_FOX)),
             mla_q_a_norm, mla_kv_a_norm, nope_rope_pad(mla_q_norm), nope_rope_pad(mla_k_norm)]
    return jnp.concatenate(parts).astype(F32).reshape(1, PV_LEN)


def _rope_tables(seq):
    def cos_sin(rot_dim):
        half = rot_dim // 2
        inv_freq = ROPE_THETA ** (-jnp.arange(half, dtype=F32) / half)
        ang = jnp.arange(seq, dtype=F32)[:, None] * inv_freq[None, :]
        return jnp.cos(ang), jnp.sin(ang)

    cos_p, sin_p = cos_sin(PARTIAL_ROT_DIM)
    cos_m, sin_m = cos_sin(MLA_ROPE_DIM)
    hp, hm = PARTIAL_ROT_DIM // 2, MLA_ROPE_DIM // 2
    ones = lambda n: jnp.ones((seq, n), F32)
    zeros = lambda n: jnp.zeros((seq, n), F32)
    rest = DIFF_QK_DIM - PARTIAL_ROT_DIM
    cd = jnp.tile(jnp.concatenate([cos_p, cos_p, ones(rest)], axis=1), (1, 2))
    sd_up = jnp.tile(jnp.concatenate([-sin_p, zeros(hp + rest)], axis=1), (1, 2))
    sd_dn = jnp.tile(jnp.concatenate([zeros(hp), sin_p, zeros(rest)], axis=1), (1, 2))
    pad = LANES - MLA_ROPE_DIM
    cm = jnp.concatenate([cos_m, cos_m, zeros(pad)], axis=1)
    sm_up = jnp.concatenate([-sin_m, zeros(hm + pad)], axis=1)
    sm_dn = jnp.concatenate([zeros(hm), sin_m, zeros(pad)], axis=1)
    return jnp.concatenate([cd, sd_up, sd_dn, cm, sm_up, sm_dn], axis=1)


def _tile(seq, want):
    return min(seq, want)


def kernel(x, norm_mix, w_in, diff_q_norm, diff_k_norm, diff_lambda_q1, diff_lambda_k1,
           diff_lambda_q2, diff_lambda_k2, diff_subln, fox_q_norm, fox_k_norm, fox_forget_bias,
           mla_q_a_norm, mla_kv_a_norm, mla_w_q_up, mla_w_kv_up, mla_q_norm, mla_k_norm,
           w_out, norm_ffn, w_ff_up, w_ff_down):
    batch, seq, d = x.shape
    assert batch == 1 and d == D_MODEL and seq % LANES == 0
    tab = _rope_tables(seq)
    xs = x.reshape(seq, d)
    tq = _tile(seq, 512)
    assert sum(IN_SECTIONS) == w_in.shape[-1]
    w_out_b, w_up_b, w_down_b = w_out.astype(BF16), w_ff_up.astype(BF16), w_ff_down.astype(BF16)
    for l in range(DEPTH):
        proj = _in_proj(xs, norm_mix[l].reshape(1, d), w_in, l, tm=_tile(seq, 512))
        pvec = _pack_vec(diff_q_norm[l], diff_k_norm[l], fox_q_norm[l], fox_k_norm[l],
                         fox_forget_bias[l], mla_q_a_norm[l], mla_kv_a_norm[l],
                         mla_q_norm[l], mla_k_norm[l])
        (qd, kd, vd, qf, kf, vf, ck, qm, km, vm) = _prep(
            proj, tab, pvec, _pack_w_q_up(mla_w_q_up[l]), _pack_w_kv_up(mla_w_kv_up[l]),
            tm=_tile(seq, 256))
        lambda_init = 0.8 - 0.6 * math.exp(-0.3 * l)
        lam = jnp.stack([diff_lambda_q1[l], diff_lambda_k1[l], diff_lambda_q2[l], diff_lambda_k2[l]])
        o_a = _attention(qd, kd, vd, (lam, diff_subln[l].reshape(1, DIFF_V_DIM)), mode="diff",
                         tq=tq, lambda_init=lambda_init)
        o_b = _attention(qf, kf, vf, (ck,), mode="fox", tq=tq)
        o_c = _attention(qm, km, vm, (), mode="mla", tq=tq)
        xs = _out_proj(xs, o_a, o_b, o_c, w_out_b, l, tm=_tile(seq, 512))
        xs = _ffn(xs, norm_ffn[l].reshape(1, d), w_up_b, w_down_b, l, tm=_tile(seq, 512), tf=1024)
    return xs.reshape(batch, seq, d)
```

```python
import functools
import math

import jax
import jax.numpy as jnp
from jax import lax
from jax.experimental import pallas as pl
from jax.experimental.pallas import tpu as pltpu

D_MODEL = 2048
DEPTH = 2
N_HEADS_DIFF = 4
DIFF_QK_DIM = 64
DIFF_V_DIM = 128
N_HEADS_FOX = 6
FOX_HEAD_DIM = 128
N_HEADS_MLA = 6
MLA_Q_RANK = 512
MLA_KV_RANK = 256
MLA_NOPE_DIM = 128
MLA_ROPE_DIM = 64
MLA_QK_DIM = MLA_ROPE_DIM + MLA_NOPE_DIM
MLA_V_DIM = 128
D_FF = 4 * D_MODEL
ROPE_THETA = 500000.0
PARTIAL_ROT_DIM = DIFF_QK_DIM // 4
EPS = 1e-6

DIFF_Q_COLS = N_HEADS_DIFF * 2 * DIFF_QK_DIM
DIFF_V_COLS = N_HEADS_DIFF * DIFF_V_DIM
FOX_COLS = N_HEADS_FOX * FOX_HEAD_DIM
MLA_V_COLS = N_HEADS_MLA * MLA_V_DIM
IN_SECTIONS = (DIFF_Q_COLS, DIFF_Q_COLS, DIFF_V_COLS, FOX_COLS, FOX_COLS, FOX_COLS,
               N_HEADS_FOX, MLA_Q_RANK, MLA_KV_RANK + MLA_ROPE_DIM)

LANES = 128
BF16_ROWS = 16
MXU_COLS = 256
UNITS_PER_TRIP = 4
MLA_PAD_DIM = 2 * LANES
VMEM_LIMIT_BYTES = 56 * 1024 * 1024

OFF_DQ = 0
OFF_DK = OFF_DQ + DIFF_Q_COLS
OFF_DV = OFF_DK + DIFF_Q_COLS
OFF_FQ = OFF_DV + DIFF_V_COLS
OFF_FK = OFF_FQ + FOX_COLS
OFF_FV = OFF_FK + FOX_COLS
MAIN_COLS = OFF_FV + FOX_COLS
TAIL_SRC_COLS = N_HEADS_FOX + MLA_Q_RANK + MLA_KV_RANK + MLA_ROPE_DIM
TAIL_MQ = 0
TAIL_CKV = TAIL_MQ + MLA_Q_RANK
TAIL_KR = TAIL_CKV + MLA_KV_RANK
TAIL_FF = TAIL_KR + LANES
TAIL_COLS = TAIL_FF + LANES
IN_TILE = MAIN_COLS // 3
PROJ_COLS = MAIN_COLS + IN_TILE
assert TAIL_COLS <= IN_TILE and IN_TILE % LANES == 0

PV_G_DQ = 0
PV_G_DK = PV_G_DQ + LANES
PV_G_FQ = PV_G_DK + LANES
PV_G_FK = PV_G_FQ + LANES
PV_F_BIAS = PV_G_FK + LANES
PV_G_QA = PV_F_BIAS + LANES
PV_G_KVA = PV_G_QA + MLA_Q_RANK
PV_G_QM = PV_G_KVA + MLA_KV_RANK
PV_G_KM = PV_G_QM + MLA_PAD_DIM
PV_LEN = PV_G_KM + MLA_PAD_DIM

NEG_BIG = -1e30
LOG2E = math.log2(math.e)

BF16 = jnp.bfloat16
F32 = jnp.float32


def _params(*semantics):
    return pltpu.CompilerParams(dimension_semantics=semantics,
                                vmem_limit_bytes=VMEM_LIMIT_BYTES)


def _in_proj_kernel(x_ref, g_ref, w_ref, o_ref, wb_ref):
    j = pl.program_id(0)
    first_row_tile = pl.program_id(1) == 0
    n_main = MAIN_COLS // IN_TILE

    @pl.when(jnp.logical_and(first_row_tile, j < n_main))
    def _():
        wb_ref[...] = w_ref[...].astype(BF16)

    @pl.when(jnp.logical_and(first_row_tile, j == n_main))
    def _():
        width = TAIL_KR + LANES
        w = w_ref[:, 0:width]
        r = pltpu.roll(w, width - N_HEADS_FOX, 1)
        lane = lax.broadcasted_iota(jnp.int32, (w.shape[0], LANES), 1)
        wb_ref[:, TAIL_MQ:TAIL_KR] = r[:, TAIL_MQ:TAIL_KR].astype(BF16)
        wb_ref[:, TAIL_KR:TAIL_FF] = jnp.where(lane < MLA_ROPE_DIM, r[:, TAIL_KR:TAIL_FF], 0.0).astype(BF16)
        wb_ref[:, TAIL_FF:TAIL_COLS] = jnp.where(lane < N_HEADS_FOX, w[:, 0:LANES], 0.0).astype(BF16)
        wb_ref[:, TAIL_COLS:] = jnp.zeros((w.shape[0], IN_TILE - TAIL_COLS), BF16)

    x = x_ref[...]
    inv = lax.rsqrt(jnp.mean(x * x, axis=-1, keepdims=True) + EPS)
    h = (x * inv * g_ref[...]).astype(BF16)
    o_ref[...] = jnp.dot(h, wb_ref[...], preferred_element_type=F32)


def _in_proj(x, g, w_in, layer, *, tm):
    s, d = x.shape
    assert w_in.shape[-1] == MAIN_COLS + TAIL_SRC_COLS
    return pl.pallas_call(
        _in_proj_kernel,
        grid=(PROJ_COLS // IN_TILE, s // tm),
        in_specs=[pl.BlockSpec((tm, d), lambda j, i: (i, 0)),
                  pl.BlockSpec((1, d), lambda j, i: (0, 0)),
                  pl.BlockSpec((None, d, IN_TILE), lambda j, i: (layer, 0, j))],
        out_specs=pl.BlockSpec((tm, IN_TILE), lambda j, i: (i, j)),
        out_shape=jax.ShapeDtypeStruct((s, PROJ_COLS), F32),
        scratch_shapes=[pltpu.VMEM((d, IN_TILE), BF16)],
        compiler_params=_params("arbitrary", "arbitrary"),
        name="in_proj",
    )(x, g, w_in)


def _rms(x, n_valid):
    ss = jnp.sum(x * x, axis=-1, keepdims=True)
    return x * lax.rsqrt(ss * (1.0 / n_valid) + EPS)


def _rms_two_halves(x):
    lo = lax.broadcasted_iota(jnp.int32, x.shape, 1) < DIFF_QK_DIM
    x2 = x * x
    s_lo = jnp.sum(jnp.where(lo, x2, 0.0), axis=-1, keepdims=True)
    s_hi = jnp.sum(jnp.where(lo, 0.0, x2), axis=-1, keepdims=True)
    return x * lax.rsqrt(jnp.where(lo, s_lo, s_hi) * (1.0 / DIFF_QK_DIM) + EPS)


def _rope(x, c, s_up, s_dn, half):
    return x * c + pltpu.roll(x, LANES - half, 1) * s_up + pltpu.roll(x, half, 1) * s_dn


def _t_bf16(x):
    return x.T.astype(BF16)


def _store_v_t(ref, h, v):
    dv = v.shape[1]
    ref[h, 0:dv, :] = _t_bf16(v)
    if ref.shape[1] > dv:
        ref[h, dv:, :] = jnp.ones((ref.shape[1] - dv, v.shape[0]), BF16)


def _prep_kernel(proj_ref, tail_ref, cd_ref, sd_up_ref, sd_dn_ref, cm_ref, sm_up_ref, sm_dn_ref,
                 pv_ref, wq_ref, wkv_ref,
                 qd_ref, kd_ref, vd_ref, qf_ref, kf_ref, vf_ref, ck_ref,
                 qm_ref, km_ref, vm_ref, carry_ref, *, tm):
    @pl.when(pl.program_id(0) == 0)
    def _():
        carry_ref[...] = jnp.zeros_like(carry_ref)

    def pv(off, n):
        return pv_ref[:, off:off + n]

    def lanes(ref, off, n=LANES):
        return ref[:, off:off + n]

    cd, sd_up, sd_dn = cd_ref[...], sd_up_ref[...], sd_dn_ref[...]
    cm, sm_up, sm_dn = cm_ref[...], sm_up_ref[...], sm_dn_ref[...]
    half_d = PARTIAL_ROT_DIM // 2
    half_m = MLA_ROPE_DIM // 2

    g_dq, g_dk = pv(PV_G_DQ, LANES), pv(PV_G_DK, LANES)
    for b in range(N_HEADS_DIFF):
        q = _rope(_rms_two_halves(lanes(proj_ref, OFF_DQ + b * LANES)) * g_dq, cd, sd_up, sd_dn, half_d)
        qd_ref[b] = _t_bf16(q * (DIFF_QK_DIM ** -0.5 * LOG2E))
        k = _rope(_rms_two_halves(lanes(proj_ref, OFF_DK + b * LANES)) * g_dk, cd, sd_up, sd_dn, half_d)
        kd_ref[:, b * LANES:(b + 1) * LANES] = k.astype(BF16)
        _store_v_t(vd_ref, b, lanes(proj_ref, OFF_DV + b * LANES))

    g_fq, g_fk = pv(PV_G_FQ, LANES), pv(PV_G_FK, LANES)
    for h in range(N_HEADS_FOX):
        q = _rms(lanes(proj_ref, OFF_FQ + h * LANES), FOX_HEAD_DIM) * g_fq
        qf_ref[h] = _t_bf16(q * (FOX_HEAD_DIM ** -0.5 * LOG2E))
        k = _rms(lanes(proj_ref, OFF_FK + h * LANES), FOX_HEAD_DIM) * g_fk
        kf_ref[:, h * LANES:(h + 1) * LANES] = k.astype(BF16)
        _store_v_t(vf_ref, h, lanes(proj_ref, OFF_FV + h * LANES))

    z = lanes(tail_ref, TAIL_FF) + pv(PV_F_BIAS, LANES)
    c = jnp.minimum(z, 0.0) - jnp.log1p(jnp.exp(-jnp.abs(z)))
    row = lax.broadcasted_iota(jnp.int32, c.shape, 0)
    shift = 1
    while shift < tm:
        c = c + jnp.where(row >= shift, pltpu.roll(c, shift, 0), 0.0)
        shift *= 2
    c = c + carry_ref[...]
    carry_ref[...] = c[tm - 1:tm, :]
    c2 = c * LOG2E
    for h in range(N_HEADS_FOX):
        ck_ref[:, h * LANES:(h + 1) * LANES] = jnp.broadcast_to(c2[:, h:h + 1], (tm, LANES))

    q_lat = (_rms(lanes(tail_ref, TAIL_MQ, MLA_Q_RANK), MLA_Q_RANK) * pv(PV_G_QA, MLA_Q_RANK)).astype(BF16)
    q_up = jnp.dot(q_lat, wq_ref[...], preferred_element_type=F32)
    c_kv = (_rms(lanes(tail_ref, TAIL_CKV, MLA_KV_RANK), MLA_KV_RANK) * pv(PV_G_KVA, MLA_KV_RANK)).astype(BF16)
    kv_up = jnp.dot(c_kv, wkv_ref[...], preferred_element_type=F32)
    g_qn, g_qr = pv(PV_G_QM, LANES), pv(PV_G_QM + LANES, LANES)
    g_kn, g_kr = pv(PV_G_KM, LANES), pv(PV_G_KM + LANES, LANES)
    k_rope = _rope(_rms(lanes(tail_ref, TAIL_KR), MLA_ROPE_DIM) * g_kr, cm, sm_up, sm_dn, half_m).astype(BF16)
    mla_scale = MLA_QK_DIM ** -0.5 * LOG2E
    v_off = N_HEADS_MLA * MLA_NOPE_DIM
    for h in range(N_HEADS_MLA):
        base = h * MLA_PAD_DIM
        q_nope = _rms(q_up[:, base:base + LANES], MLA_NOPE_DIM) * g_qn
        q_rope = _rope(_rms(q_up[:, base + LANES:base + 2 * LANES], MLA_ROPE_DIM) * g_qr,
                       cm, sm_up, sm_dn, half_m)
        qm_ref[h, 0:LANES, :] = _t_bf16(q_nope * mla_scale)
        qm_ref[h, LANES:2 * LANES, :] = _t_bf16(q_rope * mla_scale)
        k_nope = _rms(kv_up[:, h * LANES:(h + 1) * LANES], MLA_NOPE_DIM) * g_kn
        km_ref[:, base:base + LANES] = k_nope.astype(BF16)
        km_ref[:, base + LANES:base + 2 * LANES] = k_rope
        _store_v_t(vm_ref, h, kv_up[:, v_off + h * LANES:v_off + (h + 1) * LANES])


def _prep(proj, tabs, pvec, wq, wkv, *, tm):
    s = proj.shape[0]
    tail_blk = pl.BlockSpec((tm, IN_TILE), lambda i: (i, MAIN_COLS // IN_TILE))
    row_blk = lambda n: pl.BlockSpec((tm, n), lambda i: (i, 0))
    full = lambda a: pl.BlockSpec(a.shape, lambda i: (0, 0))

    def head_t(n_heads, dim):
        return (jax.ShapeDtypeStruct((n_heads, dim, s), BF16),
                pl.BlockSpec((n_heads, dim, tm), lambda i: (0, 0, i)))

    def rows(n, dtype=BF16):
        return jax.ShapeDtypeStruct((s, n), dtype), row_blk(n)

    outs = [head_t(N_HEADS_DIFF, LANES), rows(DIFF_Q_COLS), head_t(N_HEADS_DIFF, DIFF_V_DIM),
            head_t(N_HEADS_FOX, FOX_HEAD_DIM), rows(FOX_COLS), head_t(N_HEADS_FOX, FOX_HEAD_DIM + BF16_ROWS),
            rows(FOX_COLS, F32),
            head_t(N_HEADS_MLA, MLA_PAD_DIM), rows(N_HEADS_MLA * MLA_PAD_DIM),
            head_t(N_HEADS_MLA, MLA_V_DIM)]
    return pl.pallas_call(
        functools.partial(_prep_kernel, tm=tm),
        grid=(s // tm,),
        in_specs=[row_blk(MAIN_COLS), tail_blk] + [row_blk(LANES)] * len(tabs)
                 + [full(pvec), full(wq), full(wkv)],
        out_specs=[o[1] for o in outs],
        out_shape=[o[0] for o in outs],
        scratch_shapes=[pltpu.VMEM((1, LANES), F32)],
        compiler_params=_params("arbitrary"),
        name="prep",
    )(proj, proj, *tabs, pvec, wq, wkv)


def _attn_kernel(*refs, mode, tq, tiles, lambda_init, fused_row_sum):
    if mode == "diff":
        q_ref, k_ref, v_ref, lam_ref, sub_ref, o_ref, m_ref, l_ref, acc_ref, s_ref = refs
    elif mode == "fox":
        q_ref, k_ref, v_ref, ck_ref, o_ref, m_ref, l_ref, acc_ref, s_ref = refs
    else:
        q_ref, k_ref, v_ref, o_ref, m_ref, l_ref, acc_ref, s_ref = refs
    n = pl.program_id(1) * tiles
    dv = o_ref.shape[1]
    cw = min(tq, MXU_COLS)
    n_sub = 2 if mode == "diff" else 1
    streams = [(t, sub) for t in range(tiles) for sub in range(n_sub)]

    m_ref[...] = jnp.full(m_ref.shape, NEG_BIG, F32)
    l_ref[...] = jnp.zeros(l_ref.shape, F32)
    acc_ref[...] = jnp.zeros(acc_ref.shape, F32)

    def q_stream(st):
        t, sub = streams[st]
        q_t = q_ref[:, t * tq:(t + 1) * tq]
        if mode == "diff":
            lo = lax.broadcasted_iota(jnp.int32, q_t.shape, 0) < DIFF_QK_DIM
            zero = jnp.zeros_like(q_t)
            q_t = jnp.where(lo, q_t, zero) if sub == 0 else jnp.where(lo, zero, q_t)
        return q_t

    def scores(j, st, buf):
        start = pl.multiple_of(j * tq, tq)
        s_ref[buf] = jnp.dot(k_ref[pl.ds(start, tq), :], q_stream(st), preferred_element_type=F32)

    def softmax_pv(j, st, buf, masked):
        start = pl.multiple_of(j * tq, tq)
        v_t = v_ref[:, pl.ds(start, tq)]
        for c0 in range(0, tq, cw):
            cols = slice(c0, c0 + cw)
            s_t = s_ref[buf, :, cols]
            if mode == "fox":
                s_t = s_t - jnp.concatenate([ck_ref[pl.ds(start, tq), :]] * (cw // LANES), axis=1)
            if masked:
                key = lax.broadcasted_iota(jnp.int32, s_t.shape, 0)
                qry = lax.broadcasted_iota(jnp.int32, s_t.shape, 1) + c0
                s_t = jnp.where(key <= qry, s_t, NEG_BIG)
            m_prev = m_ref[st, :, cols]
            m_next = jnp.maximum(m_prev, jnp.max(s_t, axis=0, keepdims=True))
            alpha = jnp.exp2(m_prev - m_next)
            m_ref[st, :, cols] = m_next
            if fused_row_sum:
                p_t = jnp.exp2((s_t - m_next).astype(BF16))
            else:
                p_f32 = jnp.exp2(s_t - m_next)
                l_ref[st, :, cols] = alpha * l_ref[st, :, cols] + jnp.sum(p_f32, axis=0, keepdims=True)
                p_t = p_f32.astype(BF16)
            acc_ref[st, :, cols] = (acc_ref[st, :, cols] * alpha
                                    + jnp.dot(v_t, p_t, preferred_element_type=F32))

    def run(units, following):
        assert len(units) % 2 == 0 or following is None
        seq = units + ([following] if following is not None else [])
        for pos, (j, st, masked) in enumerate(units):
            if pos + 1 < len(seq):
                scores(seq[pos + 1][0], seq[pos + 1][1], (pos + 1) % 2)
            softmax_pv(j, st, pos % 2, masked)

    def full_block(j):
        return [(j, st, False) for st in range(len(streams))]

    blocks_per_trip = max(1, UNITS_PER_TRIP // len(streams))
    assert blocks_per_trip & (blocks_per_trip - 1) == 0

    def trip(t, carry):
        base = t * blocks_per_trip
        units = [u for b in range(blocks_per_trip) for u in full_block(base + b)]
        run(units, (base + blocks_per_trip, 0, False))
        return carry

    tail = [(n + b, st, b == streams[st][0])
            for b in range(tiles) for st in range(len(streams)) if streams[st][0] >= b]

    scores(0, 0, 0)
    lax.fori_loop(0, lax.shift_right_logical(n, blocks_per_trip.bit_length() - 1), trip, 0)
    left = lax.bitwise_and(n, blocks_per_trip - 1)
    for r in range(0, blocks_per_trip, math.gcd(tiles, blocks_per_trip)):
        @pl.when(left == r)
        def _():
            run([u for b in range(r) for u in full_block(n - r + b)] + tail, None)

    def out_t(st):
        row_sum = acc_ref[st, dv:dv + 1, :] if fused_row_sum else l_ref[st]
        return acc_ref[st, 0:dv, :] * (1.0 / row_sum)

    if mode == "diff":
        lp = lam_ref[...]
        lam = (jnp.exp(jnp.sum(lp[0:1] * lp[1:2], axis=1, keepdims=True))
               - jnp.exp(jnp.sum(lp[2:3] * lp[3:4], axis=1, keepdims=True)) + lambda_init)
    for t in range(tiles):
        rows = slice(t * tq, (t + 1) * tq)
        if mode == "diff":
            o = (out_t(2 * t) - lam * out_t(2 * t + 1)).T
            o = o * lax.rsqrt(jnp.mean(o * o, axis=-1, keepdims=True) + EPS) * sub_ref[...]
            o_ref[rows, :] = (o * (1.0 - lambda_init)).astype(o_ref.dtype)
        else:
            o_ref[rows, :] = out_t(t).T.astype(o_ref.dtype)


def _attention(q_t, k, v_t, extras, *, mode, tq, lambda_init=0.0):
    n_heads, dk, s = q_t.shape
    dv_aug = v_t.shape[1]
    dv = DIFF_V_DIM
    fused_row_sum = dv_aug != dv
    assert dv_aug in (dv, dv + BF16_ROWS)
    n_tiles = s // tq
    tiles = 2 if n_tiles % 2 == 0 else 1
    n_streams = tiles * (2 if mode == "diff" else 1)
    in_specs = [pl.BlockSpec((None, dk, tiles * tq), lambda h, i: (h, 0, i)),
                pl.BlockSpec((s, dk), lambda h, i: (0, h)),
                pl.BlockSpec((None, dv_aug, s), lambda h, i: (h, 0, 0))]
    if mode == "diff":
        lam, sub = extras
        in_specs += [pl.BlockSpec(lam.shape, lambda h, i: (0, 0)),
                     pl.BlockSpec(sub.shape, lambda h, i: (0, 0))]
    elif mode == "fox":
        in_specs += [pl.BlockSpec((s, LANES), lambda h, i: (0, h))]
    return pl.pallas_call(
        functools.partial(_attn_kernel, mode=mode, tq=tq, tiles=tiles, lambda_init=lambda_init,
                          fused_row_sum=fused_row_sum),
        grid=(n_heads, n_tiles // tiles),
        in_specs=in_specs,
        out_specs=pl.BlockSpec((tiles * tq, dv), lambda h, i: (i, h)),
        out_shape=jax.ShapeDtypeStruct((s, n_heads * dv), BF16),
        scratch_shapes=[pltpu.VMEM((n_streams, 1, tq), F32),
                        pltpu.VMEM((n_streams, 1, tq), F32),
                        pltpu.VMEM((n_streams, dv_aug, tq), F32),
                        pltpu.VMEM((2, tq, tq), F32)],
        compiler_params=_params("arbitrary", "arbitrary"),
        name="attn_" + mode,
    )(q_t, k, v_t, *extras)


def _out_proj_kernel(x_ref, oa_ref, ob_ref, oc_ref, w_ref, o_ref):
    a0, a1 = DIFF_V_COLS, DIFF_V_COLS + FOX_COLS
    acc = x_ref[...]
    acc = acc + jnp.dot(oa_ref[...], w_ref[0:a0, :], preferred_element_type=F32)
    acc = acc + jnp.dot(ob_ref[...], w_ref[a0:a1, :], preferred_element_type=F32)
    acc = acc + jnp.dot(oc_ref[...], w_ref[a1:, :], preferred_element_type=F32)
    o_ref[...] = acc


def _out_proj(x, oa, ob, oc, w, layer, *, tm):
    s, d = x.shape
    row_blk = lambda n: pl.BlockSpec((tm, n), lambda i: (i, 0))
    return pl.pallas_call(
        _out_proj_kernel,
        grid=(s // tm,),
        in_specs=[row_blk(d), row_blk(oa.shape[1]), row_blk(ob.shape[1]), row_blk(oc.shape[1]),
                  pl.BlockSpec((None,) + w.shape[1:], lambda i: (layer, 0, 0))],
        out_specs=row_blk(d),
        out_shape=jax.ShapeDtypeStruct((s, d), F32),
        compiler_params=_params("arbitrary"),
        name="out_proj",
    )(x, oa, ob, oc, w)


def _ffn_kernel(x_ref, g_ref, wu_ref, wd_ref, o_ref, h_ref):
    @pl.when(pl.program_id(1) == 0)
    def _():
        x = x_ref[...]
        inv = lax.rsqrt(jnp.mean(x * x, axis=-1, keepdims=True) + EPS)
        h_ref[...] = (x * inv * g_ref[...]).astype(BF16)
        o_ref[...] = x

    u = jnp.dot(h_ref[...], wu_ref[...], preferred_element_type=F32)
    a = jnp.square(jnp.maximum(u, 0.0)).astype(BF16)
    o_ref[...] += jnp.dot(a, wd_ref[...], preferred_element_type=F32)


def _ffn(x, g, wu, wd, layer, *, tm, tf):
    s, d = x.shape
    f = wu.shape[2]
    return pl.pallas_call(
        _ffn_kernel,
        grid=(s // tm, f // tf),
        in_specs=[pl.BlockSpec((tm, d), lambda i, j: (i, 0)),
                  pl.BlockSpec((1, d), lambda i, j: (0, 0)),
                  pl.BlockSpec((None, d, tf), lambda i, j: (layer, 0, j)),
                  pl.BlockSpec((None, tf, d), lambda i, j: (layer, j, 0))],
        out_specs=pl.BlockSpec((tm, d), lambda i, j: (i, 0)),
        out_shape=jax.ShapeDtypeStruct((s, d), F32),
        scratch_shapes=[pltpu.VMEM((tm, d), BF16)],
        compiler_params=_params("arbitrary", "arbitrary"),
        name="ffn",
    )(x, g, wu, wd)


def _pack_w_q_up(w):
    w = w.reshape(MLA_Q_RANK, N_HEADS_MLA, MLA_QK_DIM)
    pad = jnp.zeros((MLA_Q_RANK, N_HEADS_MLA, MLA_PAD_DIM - MLA_QK_DIM), w.dtype)
    w = jnp.concatenate([w[:, :, MLA_ROPE_DIM:], w[:, :, :MLA_ROPE_DIM], pad], axis=2)
    return w.reshape(MLA_Q_RANK, N_HEADS_MLA * MLA_PAD_DIM).astype(BF16)


def _pack_w_kv_up(w):
    w = w.reshape(MLA_KV_RANK, N_HEADS_MLA, MLA_NOPE_DIM + MLA_V_DIM)
    k_nope = w[:, :, :MLA_NOPE_DIM].reshape(MLA_KV_RANK, N_HEADS_MLA * MLA_NOPE_DIM)
    v = w[:, :, MLA_NOPE_DIM:].reshape(MLA_KV_RANK, MLA_V_COLS)
    return jnp.concatenate([k_nope, v], axis=1).astype(BF16)


def _pack_vec(diff_q_norm, diff_k_norm, fox_q_norm, fox_k_norm, fox_forget_bias,
              mla_q_a_norm, mla_kv_a_norm, mla_q_norm, mla_k_norm):
    def nope_rope_pad(g):
        return jnp.concatenate([g[MLA_ROPE_DIM:], g[:MLA_ROPE_DIM],
                                jnp.zeros((MLA_PAD_DIM - MLA_QK_DIM,), g.dtype)])
    parts = [jnp.tile(diff_q_norm, 2), jnp.tile(diff_k_norm, 2), fox_q_norm, fox_k_norm,
             jnp.pad(fox_forget_bias, (0, LANES - N_HEADS_FOX)),
             mla_q_a_norm, mla_kv_a_norm, nope_rope_pad(mla_q_norm), nope_rope_pad(mla_k_norm)]
    return jnp.concatenate(parts).astype(F32).reshape(1, PV_LEN)


def _rope_tables(seq):
    def cos_sin(rot_dim):
        half = rot_dim // 2
        inv_freq = ROPE_THETA ** (-jnp.arange(half, dtype=F32) / half)
        ang = jnp.arange(seq, dtype=F32)[:, None] * inv_freq[None, :]
        return jnp.cos(ang), jnp.sin(ang)

    cos_p, sin_p = cos_sin(PARTIAL_ROT_DIM)
    cos_m, sin_m = cos_sin(MLA_ROPE_DIM)
    hp, hm = PARTIAL_ROT_DIM // 2, MLA_ROPE_DIM // 2
    ones = lambda n: jnp.ones((seq, n), F32)
    zeros = lambda n: jnp.zeros((seq, n), F32)
    rest = DIFF_QK_DIM - PARTIAL_ROT_DIM
    cd = jnp.tile(jnp.concatenate([cos_p, cos_p, ones(rest)], axis=1), (1, 2))
    sd_up = jnp.tile(jnp.concatenate([-sin_p, zeros(hp + rest)], axis=1), (1, 2))
    sd_dn = jnp.tile(jnp.concatenate([zeros(hp), sin_p, zeros(rest)], axis=1), (1, 2))
    pad = LANES - MLA_ROPE_DIM
    cm = jnp.concatenate([cos_m, cos_m, zeros(pad)], axis=1)
    sm_up = jnp.concatenate([-sin_m, zeros(hm + pad)], axis=1)
    sm_dn = jnp.concatenate([zeros(hm), sin_m, zeros(pad)], axis=1)
    return cd, sd_up, sd_dn, cm, sm_up, sm_dn


def _tile(seq, want):
    return min(seq, want)


def kernel(x, norm_mix, w_in, diff_q_norm, diff_k_norm, diff_lambda_q1, diff_lambda_k1,
           diff_lambda_q2, diff_lambda_k2, diff_subln, fox_q_norm, fox_k_norm, fox_forget_bias,
           mla_q_a_norm, mla_kv_a_norm, mla_w_q_up, mla_w_kv_up, mla_q_norm, mla_k_norm,
           w_out, norm_ffn, w_ff_up, w_ff_down):
    batch, seq, d = x.shape
    assert batch == 1 and d == D_MODEL and seq % LANES == 0
    tabs = _rope_tables(seq)
    xs = x.reshape(seq, d)
    tq = _tile(seq, 512)
    assert sum(IN_SECTIONS) == w_in.shape[-1]
    w_out_b, w_up_b, w_down_b = w_out.astype(BF16), w_ff_up.astype(BF16), w_ff_down.astype(BF16)
    for l in range(DEPTH):
        proj = _in_proj(xs, norm_mix[l].reshape(1, d), w_in, l, tm=_tile(seq, 512))
        pvec = _pack_vec(diff_q_norm[l], diff_k_norm[l], fox_q_norm[l], fox_k_norm[l],
                         fox_forget_bias[l], mla_q_a_norm[l], mla_kv_a_norm[l],
                         mla_q_norm[l], mla_k_norm[l])
        (qd, kd, vd, qf, kf, vf, ck, qm, km, vm) = _prep(
            proj, tabs, pvec, _pack_w_q_up(mla_w_q_up[l]), _pack_w_kv_up(mla_w_kv_up[l]),
            tm=_tile(seq, 256))
        lambda_init = 0.8 - 0.6 * math.exp(-0.3 * l)
        lam = jnp.stack([diff_lambda_q1[l], diff_lambda_k1[l], diff_lambda_q2[l], diff_lambda_k2[l]])
        o_a = _attention(qd, kd, vd, (lam, diff_subln[l].reshape(1, DIFF_V_DIM)), mode="diff",
                         tq=tq, lambda_init=lambda_init)
        o_b = _attention(qf, kf, vf, (ck,), mode="fox", tq=tq)
        o_c = _attention(qm, km, vm, (), mode="mla", tq=tq)
        xs = _out_proj(xs, o_a, o_b, o_c, w_out_b, l, tm=_tile(seq, 512))
        xs = _ffn(xs, norm_ffn[l].reshape(1, d), w_up_b, w_down_b, l, tm=_tile(seq, 512), tf=1024)
    return xs.reshape(batch, seq, d)
```

```python
import functools
import math

import jax
import jax.numpy as jnp
from jax import lax
from jax.experimental import pallas as pl
from jax.experimental.pallas import tpu as pltpu

D_MODEL = 2048
DEPTH = 2
N_HEADS_DIFF = 4
DIFF_QK_DIM = 64
DIFF_V_DIM = 128
N_HEADS_FOX = 6
FOX_HEAD_DIM = 128
N_HEADS_MLA = 6
MLA_Q_RANK = 512
MLA_KV_RANK = 256
MLA_NOPE_DIM = 128
MLA_ROPE_DIM = 64
MLA_QK_DIM = MLA_ROPE_DIM + MLA_NOPE_DIM
MLA_V_DIM = 128
D_FF = 4 * D_MODEL
ROPE_THETA = 500000.0
PARTIAL_ROT_DIM = DIFF_QK_DIM // 4
EPS = 1e-6

DIFF_Q_COLS = N_HEADS_DIFF * 2 * DIFF_QK_DIM
DIFF_V_COLS = N_HEADS_DIFF * DIFF_V_DIM
FOX_COLS = N_HEADS_FOX * FOX_HEAD_DIM
MLA_V_COLS = N_HEADS_MLA * MLA_V_DIM
IN_SECTIONS = (DIFF_Q_COLS, DIFF_Q_COLS, DIFF_V_COLS, FOX_COLS, FOX_COLS, FOX_COLS,
               N_HEADS_FOX, MLA_Q_RANK, MLA_KV_RANK + MLA_ROPE_DIM)

LANES = 128
FOX_PAD_DIM = 2 * LANES
DECAY_PIECES = 3
MXU_COLS = 256
UNITS_PER_TRIP = 8
MLA_PAD_DIM = 2 * LANES
VMEM_LIMIT_BYTES = 56 * 1024 * 1024

OFF_DQ = 0
OFF_DK = OFF_DQ + DIFF_Q_COLS
OFF_DV = OFF_DK + DIFF_Q_COLS
OFF_FQ = OFF_DV + DIFF_V_COLS
OFF_FK = OFF_FQ + FOX_COLS
OFF_FV = OFF_FK + FOX_COLS
MAIN_COLS = OFF_FV + FOX_COLS
TAIL_SRC_COLS = N_HEADS_FOX + MLA_Q_RANK + MLA_KV_RANK + MLA_ROPE_DIM
TAIL_MQ = 0
TAIL_CKV = TAIL_MQ + MLA_Q_RANK
TAIL_KR = TAIL_CKV + MLA_KV_RANK
TAIL_FF = TAIL_KR + LANES
TAIL_COLS = TAIL_FF + LANES
IN_TILE = MAIN_COLS // 3
PROJ_COLS = MAIN_COLS + IN_TILE
assert TAIL_COLS <= IN_TILE and IN_TILE % LANES == 0

PV_G_DQ = 0
PV_G_DK = PV_G_DQ + LANES
PV_G_FQ = PV_G_DK + LANES
PV_G_FK = PV_G_FQ + LANES
PV_F_BIAS = PV_G_FK + LANES
PV_G_QA = PV_F_BIAS + LANES
PV_G_KVA = PV_G_QA + MLA_Q_RANK
PV_G_QM = PV_G_KVA + MLA_KV_RANK
PV_G_KM = PV_G_QM + MLA_PAD_DIM
PV_LEN = PV_G_KM + MLA_PAD_DIM

NEG_BIG = -1e30
LOG2E = math.log2(math.e)

BF16 = jnp.bfloat16
F32 = jnp.float32


def _params(*semantics):
    return pltpu.CompilerParams(dimension_semantics=semantics,
                                vmem_limit_bytes=VMEM_LIMIT_BYTES)


def _in_proj_kernel(x_ref, g_ref, w_ref, o_ref, wb_ref):
    j = pl.program_id(0)
    first_row_tile = pl.program_id(1) == 0
    n_main = MAIN_COLS // IN_TILE

    @pl.when(jnp.logical_and(first_row_tile, j < n_main))
    def _():
        wb_ref[...] = w_ref[...].astype(BF16)

    @pl.when(jnp.logical_and(first_row_tile, j == n_main))
    def _():
        width = TAIL_KR + LANES
        w = w_ref[:, 0:width]
        r = pltpu.roll(w, width - N_HEADS_FOX, 1)
        lane = lax.broadcasted_iota(jnp.int32, (w.shape[0], LANES), 1)
        wb_ref[:, TAIL_MQ:TAIL_KR] = r[:, TAIL_MQ:TAIL_KR].astype(BF16)
        wb_ref[:, TAIL_KR:TAIL_FF] = jnp.where(lane < MLA_ROPE_DIM, r[:, TAIL_KR:TAIL_FF], 0.0).astype(BF16)
        wb_ref[:, TAIL_FF:TAIL_COLS] = jnp.where(lane < N_HEADS_FOX, w[:, 0:LANES], 0.0).astype(BF16)
        wb_ref[:, TAIL_COLS:] = jnp.zeros((w.shape[0], IN_TILE - TAIL_COLS), BF16)

    x = x_ref[...]
    inv = lax.rsqrt(jnp.mean(x * x, axis=-1, keepdims=True) + EPS)
    h = (x * inv * g_ref[...]).astype(BF16)
    o_ref[...] = jnp.dot(h, wb_ref[...], preferred_element_type=F32)


def _in_proj(x, g, w_in, layer, *, tm):
    s, d = x.shape
    assert w_in.shape[-1] == MAIN_COLS + TAIL_SRC_COLS
    return pl.pallas_call(
        _in_proj_kernel,
        grid=(PROJ_COLS // IN_TILE, s // tm),
        in_specs=[pl.BlockSpec((tm, d), lambda j, i: (i, 0)),
                  pl.BlockSpec((1, d), lambda j, i: (0, 0)),
                  pl.BlockSpec((None, d, IN_TILE), lambda j, i: (layer, 0, j))],
        out_specs=pl.BlockSpec((tm, IN_TILE), lambda j, i: (i, j)),
        out_shape=jax.ShapeDtypeStruct((s, PROJ_COLS), F32),
        scratch_shapes=[pltpu.VMEM((d, IN_TILE), BF16)],
        compiler_params=_params("arbitrary", "arbitrary"),
        name="in_proj",
    )(x, g, w_in)


def _rms(x, n_valid):
    ss = jnp.sum(x * x, axis=-1, keepdims=True)
    return x * lax.rsqrt(ss * (1.0 / n_valid) + EPS)


def _rms_two_halves(x):
    lo = lax.broadcasted_iota(jnp.int32, x.shape, 1) < DIFF_QK_DIM
    x2 = x * x
    s_lo = jnp.sum(jnp.where(lo, x2, 0.0), axis=-1, keepdims=True)
    s_hi = jnp.sum(jnp.where(lo, 0.0, x2), axis=-1, keepdims=True)
    return x * lax.rsqrt(jnp.where(lo, s_lo, s_hi) * (1.0 / DIFF_QK_DIM) + EPS)


def _rope(x, c, s_up, s_dn, half):
    return x * c + pltpu.roll(x, LANES - half, 1) * s_up + pltpu.roll(x, half, 1) * s_dn


def _t_bf16(x):
    return x.T.astype(BF16)


def _prep_kernel(proj_ref, tail_ref, cd_ref, sd_up_ref, sd_dn_ref, cm_ref, sm_up_ref, sm_dn_ref,
                 pv_ref, wq_ref, wkv_ref,
                 qd_ref, kd_ref, vd_ref, qf_ref, kf_ref, vf_ref,
                 qm_ref, km_ref, vm_ref, carry_ref, *, tm):
    @pl.when(pl.program_id(0) == 0)
    def _():
        carry_ref[...] = jnp.zeros_like(carry_ref)

    def pv(off, n):
        return pv_ref[:, off:off + n]

    def lanes(ref, off, n=LANES):
        return ref[:, off:off + n]

    cd, sd_up, sd_dn = cd_ref[...], sd_up_ref[...], sd_dn_ref[...]
    cm, sm_up, sm_dn = cm_ref[...], sm_up_ref[...], sm_dn_ref[...]
    half_d = PARTIAL_ROT_DIM // 2
    half_m = MLA_ROPE_DIM // 2

    g_dq, g_dk = pv(PV_G_DQ, LANES), pv(PV_G_DK, LANES)
    for b in range(N_HEADS_DIFF):
        q = _rope(_rms_two_halves(lanes(proj_ref, OFF_DQ + b * LANES)) * g_dq, cd, sd_up, sd_dn, half_d)
        qd_ref[b] = _t_bf16(q * (DIFF_QK_DIM ** -0.5 * LOG2E))
        k = _rope(_rms_two_halves(lanes(proj_ref, OFF_DK + b * LANES)) * g_dk, cd, sd_up, sd_dn, half_d)
        kd_ref[:, b * LANES:(b + 1) * LANES] = k.astype(BF16)
        vd_ref[b] = _t_bf16(lanes(proj_ref, OFF_DV + b * LANES))

    z = lanes(tail_ref, TAIL_FF) + pv(PV_F_BIAS, LANES)
    c = jnp.minimum(z, 0.0) - jnp.log1p(jnp.exp(-jnp.abs(z)))
    row = lax.broadcasted_iota(jnp.int32, c.shape, 0)
    shift = 1
    while shift < tm:
        c = c + jnp.where(row >= shift, pltpu.roll(c, shift, 0), 0.0)
        shift *= 2
    c = c + carry_ref[...]
    carry_ref[...] = c[tm - 1:tm, :]
    c2 = c * LOG2E

    g_fq, g_fk = pv(PV_G_FQ, LANES), pv(PV_G_FK, LANES)
    lane = lax.broadcasted_iota(jnp.int32, (tm, LANES), 1)
    ones_rows = jnp.where(lax.broadcasted_iota(jnp.int32, (LANES, tm), 0) < DECAY_PIECES, 1.0, 0.0)
    for h in range(N_HEADS_FOX):
        base = h * FOX_PAD_DIM
        q = _rms(lanes(proj_ref, OFF_FQ + h * LANES), FOX_HEAD_DIM) * g_fq
        qf_ref[h, 0:LANES, :] = _t_bf16(q * (FOX_HEAD_DIM ** -0.5 * LOG2E))
        qf_ref[h, LANES:, :] = ones_rows.astype(BF16)
        k = _rms(lanes(proj_ref, OFF_FK + h * LANES), FOX_HEAD_DIM) * g_fk
        kf_ref[:, base:base + LANES] = k.astype(BF16)
        rest = -jnp.broadcast_to(c2[:, h:h + 1], (tm, LANES))
        decay = jnp.zeros((tm, LANES), F32)
        for piece in range(DECAY_PIECES):
            part = rest.astype(BF16).astype(F32)
            decay = jnp.where(lane == piece, part, decay)
            rest = rest - part
        kf_ref[:, base + LANES:base + FOX_PAD_DIM] = decay.astype(BF16)
        vf_ref[h] = _t_bf16(lanes(proj_ref, OFF_FV + h * LANES))

    q_lat = (_rms(lanes(tail_ref, TAIL_MQ, MLA_Q_RANK), MLA_Q_RANK) * pv(PV_G_QA, MLA_Q_RANK)).astype(BF16)
    q_up = jnp.dot(q_lat, wq_ref[...], preferred_element_type=F32)
    c_kv = (_rms(lanes(tail_ref, TAIL_CKV, MLA_KV_RANK), MLA_KV_RANK) * pv(PV_G_KVA, MLA_KV_RANK)).astype(BF16)
    kv_up = jnp.dot(c_kv, wkv_ref[...], preferred_element_type=F32)
    g_qn, g_qr = pv(PV_G_QM, LANES), pv(PV_G_QM + LANES, LANES)
    g_kn, g_kr = pv(PV_G_KM, LANES), pv(PV_G_KM + LANES, LANES)
    k_rope = _rope(_rms(lanes(tail_ref, TAIL_KR), MLA_ROPE_DIM) * g_kr, cm, sm_up, sm_dn, half_m).astype(BF16)
    mla_scale = MLA_QK_DIM ** -0.5 * LOG2E
    v_off = N_HEADS_MLA * MLA_NOPE_DIM
    for h in range(N_HEADS_MLA):
        base = h * MLA_PAD_DIM
        q_nope = _rms(q_up[:, base:base + LANES], MLA_NOPE_DIM) * g_qn
        q_rope = _rope(_rms(q_up[:, base + LANES:base + 2 * LANES], MLA_ROPE_DIM) * g_qr,
                       cm, sm_up, sm_dn, half_m)
        qm_ref[h, 0:LANES, :] = _t_bf16(q_nope * mla_scale)
        qm_ref[h, LANES:2 * LANES, :] = _t_bf16(q_rope * mla_scale)
        k_nope = _rms(kv_up[:, h * LANES:(h + 1) * LANES], MLA_NOPE_DIM) * g_kn
        km_ref[:, base:base + LANES] = k_nope.astype(BF16)
        km_ref[:, base + LANES:base + 2 * LANES] = k_rope
        vm_ref[h] = _t_bf16(kv_up[:, v_off + h * LANES:v_off + (h + 1) * LANES])


def _prep(proj, tabs, pvec, wq, wkv, *, tm):
    s = proj.shape[0]
    tail_blk = pl.BlockSpec((tm, IN_TILE), lambda i: (i, MAIN_COLS // IN_TILE))
    row_blk = lambda n: pl.BlockSpec((tm, n), lambda i: (i, 0))
    full = lambda a: pl.BlockSpec(a.shape, lambda i: (0, 0))

    def head_t(n_heads, dim):
        return (jax.ShapeDtypeStruct((n_heads, dim, s), BF16),
                pl.BlockSpec((n_heads, dim, tm), lambda i: (0, 0, i)))

    def rows(n, dtype=BF16):
        return jax.ShapeDtypeStruct((s, n), dtype), row_blk(n)

    outs = [head_t(N_HEADS_DIFF, LANES), rows(DIFF_Q_COLS), head_t(N_HEADS_DIFF, DIFF_V_DIM),
            head_t(N_HEADS_FOX, FOX_PAD_DIM), rows(N_HEADS_FOX * FOX_PAD_DIM),
            head_t(N_HEADS_FOX, FOX_HEAD_DIM),
            head_t(N_HEADS_MLA, MLA_PAD_DIM), rows(N_HEADS_MLA * MLA_PAD_DIM),
            head_t(N_HEADS_MLA, MLA_V_DIM)]
    return pl.pallas_call(
        functools.partial(_prep_kernel, tm=tm),
        grid=(s // tm,),
        in_specs=[row_blk(MAIN_COLS), tail_blk] + [row_blk(LANES)] * len(tabs)
                 + [full(pvec), full(wq), full(wkv)],
        out_specs=[o[1] for o in outs],
        out_shape=[o[0] for o in outs],
        scratch_shapes=[pltpu.VMEM((1, LANES), F32)],
        compiler_params=_params("arbitrary"),
        name="prep",
    )(proj, proj, *tabs, pvec, wq, wkv)


def _attn_kernel(*refs, mode, tq, tiles, lambda_init):
    if mode == "diff":
        q_ref, k_ref, v_ref, lam_ref, sub_ref, o_ref, m_ref, l_ref, acc_ref, s_ref = refs
    else:
        q_ref, k_ref, v_ref, o_ref, m_ref, l_ref, acc_ref, s_ref = refs
    n = pl.program_id(1) * tiles
    dv = o_ref.shape[1]
    cw = min(tq, MXU_COLS)
    n_sub = 2 if mode == "diff" else 1
    streams = [(t, sub) for t in range(tiles) for sub in range(n_sub)]

    m_ref[...] = jnp.full(m_ref.shape, NEG_BIG, F32)
    l_ref[...] = jnp.zeros(l_ref.shape, F32)
    acc_ref[...] = jnp.zeros(acc_ref.shape, F32)

    def q_stream(st):
        t, sub = streams[st]
        q_t = q_ref[:, t * tq:(t + 1) * tq]
        if mode == "diff":
            lo = lax.broadcasted_iota(jnp.int32, q_t.shape, 0) < DIFF_QK_DIM
            zero = jnp.zeros_like(q_t)
            q_t = jnp.where(lo, q_t, zero) if sub == 0 else jnp.where(lo, zero, q_t)
        return q_t

    def scores(j, st, buf):
        start = pl.multiple_of(j * tq, tq)
        s_ref[buf] = jnp.dot(k_ref[pl.ds(start, tq), :], q_stream(st), preferred_element_type=F32)

    def softmax_pv(j, st, buf, masked):
        start = pl.multiple_of(j * tq, tq)
        v_t = v_ref[:, pl.ds(start, tq)]
        for c0 in range(0, tq, cw):
            cols = slice(c0, c0 + cw)
            s_t = s_ref[buf, :, cols]
            if masked:
                key = lax.broadcasted_iota(jnp.int32, s_t.shape, 0)
                qry = lax.broadcasted_iota(jnp.int32, s_t.shape, 1) + c0
                s_t = jnp.where(key <= qry, s_t, NEG_BIG)
            m_prev = m_ref[st, :, cols]
            m_next = jnp.maximum(m_prev, jnp.max(s_t, axis=0, keepdims=True))
            alpha = jnp.exp2(m_prev - m_next)
            m_ref[st, :, cols] = m_next
            p_t = jnp.exp2(s_t - m_next)
            l_ref[st, :, cols] = alpha * l_ref[st, :, cols] + jnp.sum(p_t, axis=0, keepdims=True)
            acc_ref[st, :, cols] = (acc_ref[st, :, cols] * alpha
                                    + jnp.dot(v_t, p_t.astype(BF16), preferred_element_type=F32))

    def run(units, following):
        assert len(units) % 2 == 0 or following is None
        seq = units + ([following] if following is not None else [])
        for pos, (j, st, masked) in enumerate(units):
            if pos + 1 < len(seq):
                scores(seq[pos + 1][0], seq[pos + 1][1], (pos + 1) % 2)
            softmax_pv(j, st, pos % 2, masked)

    def full_block(j):
        return [(j, st, False) for st in range(len(streams))]

    blocks_per_trip = max(1, UNITS_PER_TRIP // len(streams))
    assert blocks_per_trip & (blocks_per_trip - 1) == 0

    def trip(t, carry):
        base = t * blocks_per_trip
        units = [u for b in range(blocks_per_trip) for u in full_block(base + b)]
        run(units, (base + blocks_per_trip, 0, False))
        return carry

    tail = [(n + b, st, b == streams[st][0])
            for b in range(tiles) for st in range(len(streams)) if streams[st][0] >= b]

    scores(0, 0, 0)
    lax.fori_loop(0, lax.shift_right_logical(n, blocks_per_trip.bit_length() - 1), trip, 0)
    left = lax.bitwise_and(n, blocks_per_trip - 1)
    for r in range(0, blocks_per_trip, math.gcd(tiles, blocks_per_trip)):
        @pl.when(left == r)
        def _():
            run([u for b in range(r) for u in full_block(n - r + b)] + tail, None)

    def out_t(st):
        return acc_ref[st] * (1.0 / l_ref[st])

    if mode == "diff":
        lp = lam_ref[...]
        lam = (jnp.exp(jnp.sum(lp[0:1] * lp[1:2], axis=1, keepdims=True))
               - jnp.exp(jnp.sum(lp[2:3] * lp[3:4], axis=1, keepdims=True)) + lambda_init)
    for t in range(tiles):
        rows = slice(t * tq, (t + 1) * tq)
        if mode == "diff":
            o = (out_t(2 * t) - lam * out_t(2 * t + 1)).T
            o = o * lax.rsqrt(jnp.mean(o * o, axis=-1, keepdims=True) + EPS) * sub_ref[...]
            o_ref[rows, :] = (o * (1.0 - lambda_init)).astype(o_ref.dtype)
        else:
            o_ref[rows, :] = out_t(t).T.astype(o_ref.dtype)


def _attention(q_t, k, v_t, extras, *, mode, tq, lambda_init=0.0):
    n_heads, dk, s = q_t.shape
    dv = v_t.shape[1]
    n_tiles = s // tq
    tiles = 2 if n_tiles % 2 == 0 else 1
    n_streams = tiles * (2 if mode == "diff" else 1)
    in_specs = [pl.BlockSpec((None, dk, tiles * tq), lambda h, i: (h, 0, i)),
                pl.BlockSpec((s, dk), lambda h, i: (0, h)),
                pl.BlockSpec((None, dv, s), lambda h, i: (h, 0, 0))]
    if mode == "diff":
        lam, sub = extras
        in_specs += [pl.BlockSpec(lam.shape, lambda h, i: (0, 0)),
                     pl.BlockSpec(sub.shape, lambda h, i: (0, 0))]
    return pl.pallas_call(
        functools.partial(_attn_kernel, mode=mode, tq=tq, tiles=tiles, lambda_init=lambda_init),
        grid=(n_heads, n_tiles // tiles),
        in_specs=in_specs,
        out_specs=pl.BlockSpec((tiles * tq, dv), lambda h, i: (i, h)),
        out_shape=jax.ShapeDtypeStruct((s, n_heads * dv), BF16),
        scratch_shapes=[pltpu.VMEM((n_streams, 1, tq), F32),
                        pltpu.VMEM((n_streams, 1, tq), F32),
                        pltpu.VMEM((n_streams, dv, tq), F32),
                        pltpu.VMEM((2, tq, tq), F32)],
        compiler_params=_params("arbitrary", "arbitrary"),
        name="attn_" + mode,
    )(q_t, k, v_t, *extras)


def _out_proj_kernel(x_ref, oa_ref, ob_ref, oc_ref, w_ref, o_ref):
    a0, a1 = DIFF_V_COLS, DIFF_V_COLS + FOX_COLS
    acc = x_ref[...]
    acc = acc + jnp.dot(oa_ref[...], w_ref[0:a0, :], preferred_element_type=F32)
    acc = acc + jnp.dot(ob_ref[...], w_ref[a0:a1, :], preferred_element_type=F32)
    acc = acc + jnp.dot(oc_ref[...], w_ref[a1:, :], preferred_element_type=F32)
    o_ref[...] = acc


def _out_proj(x, oa, ob, oc, w, layer, *, tm):
    s, d = x.shape
    row_blk = lambda n: pl.BlockSpec((tm, n), lambda i: (i, 0))
    return pl.pallas_call(
        _out_proj_kernel,
        grid=(s // tm,),
        in_specs=[row_blk(d), row_blk(oa.shape[1]), row_blk(ob.shape[1]), row_blk(oc.shape[1]),
                  pl.BlockSpec((None,) + w.shape[1:], lambda i: (layer, 0, 0))],
        out_specs=row_blk(d),
        out_shape=jax.ShapeDtypeStruct((s, d), F32),
        compiler_params=_params("arbitrary"),
        name="out_proj",
    )(x, oa, ob, oc, w)


def _ffn_kernel(x_ref, g_ref, wu_ref, wd_ref, o_ref, h_ref):
    @pl.when(pl.program_id(1) == 0)
    def _():
        x = x_ref[...]
        inv = lax.rsqrt(jnp.mean(x * x, axis=-1, keepdims=True) + EPS)
        h_ref[...] = (x * inv * g_ref[...]).astype(BF16)
        o_ref[...] = x

    u = jnp.dot(h_ref[...], wu_ref[...], preferred_element_type=F32)
    a = jnp.square(jnp.maximum(u, 0.0)).astype(BF16)
    o_ref[...] += jnp.dot(a, wd_ref[...], preferred_element_type=F32)


def _ffn(x, g, wu, wd, layer, *, tm, tf):
    s, d = x.shape
    f = wu.shape[2]
    return pl.pallas_call(
        _ffn_kernel,
        grid=(s // tm, f // tf),
        in_specs=[pl.BlockSpec((tm, d), lambda i, j: (i, 0)),
                  pl.BlockSpec((1, d), lambda i, j: (0, 0)),
                  pl.BlockSpec((None, d, tf), lambda i, j: (layer, 0, j)),
                  pl.BlockSpec((None, tf, d), lambda i, j: (layer, j, 0))],
        out_specs=pl.BlockSpec((tm, d), lambda i, j: (i, 0)),
        out_shape=jax.ShapeDtypeStruct((s, d), F32),
        scratch_shapes=[pltpu.VMEM((tm, d), BF16)],
        compiler_params=_params("arbitrary", "arbitrary"),
        name="ffn",
    )(x, g, wu, wd)


def _pack_w_q_up(w):
    w = w.reshape(MLA_Q_RANK, N_HEADS_MLA, MLA_QK_DIM)
    pad = jnp.zeros((MLA_Q_RANK, N_HEADS_MLA, MLA_PAD_DIM - MLA_QK_DIM), w.dtype)
    w = jnp.concatenate([w[:, :, MLA_ROPE_DIM:], w[:, :, :MLA_ROPE_DIM], pad], axis=2)
    return w.reshape(MLA_Q_RANK, N_HEADS_MLA * MLA_PAD_DIM).astype(BF16)


def _pack_w_kv_up(w):
    w = w.reshape(MLA_KV_RANK, N_HEADS_MLA, MLA_NOPE_DIM + MLA_V_DIM)
    k_nope = w[:, :, :MLA_NOPE_DIM].reshape(MLA_KV_RANK, N_HEADS_MLA * MLA_NOPE_DIM)
    v = w[:, :, MLA_NOPE_DIM:].reshape(MLA_KV_RANK, MLA_V_COLS)
    return jnp.concatenate([k_nope, v], axis=1).astype(BF16)


def _pack_vec(diff_q_norm, diff_k_norm, fox_q_norm, fox_k_norm, fox_forget_bias,
              mla_q_a_norm, mla_kv_a_norm, mla_q_norm, mla_k_norm):
    def nope_rope_pad(g):
        return jnp.concatenate([g[MLA_ROPE_DIM:], g[:MLA_ROPE_DIM],
                                jnp.zeros((MLA_PAD_DIM - MLA_QK_DIM,), g.dtype)])
    parts = [jnp.tile(diff_q_norm, 2), jnp.tile(diff_k_norm, 2), fox_q_norm, fox_k_norm,
             jnp.pad(fox_forget_bias, (0, LANES - N_HEADS_FOX)),
             mla_q_a_norm, mla_kv_a_norm, nope_rope_pad(mla_q_norm), nope_rope_pad(mla_k_norm)]
    return jnp.concatenate(parts).astype(F32).reshape(1, PV_LEN)


def _rope_tables(seq):
    def cos_sin(rot_dim):
        half = rot_dim // 2
        inv_freq = ROPE_THETA ** (-jnp.arange(half, dtype=F32) / half)
        ang = jnp.arange(seq, dtype=F32)[:, None] * inv_freq[None, :]
        return jnp.cos(ang), jnp.sin(ang)

    cos_p, sin_p = cos_sin(PARTIAL_ROT_DIM)
    cos_m, sin_m = cos_sin(MLA_ROPE_DIM)
    hp, hm = PARTIAL_ROT_DIM // 2, MLA_ROPE_DIM // 2
    ones = lambda n: jnp.ones((seq, n), F32)
    zeros = lambda n: jnp.zeros((seq, n), F32)
    rest = DIFF_QK_DIM - PARTIAL_ROT_DIM
    cd = jnp.tile(jnp.concatenate([cos_p, cos_p, ones(rest)], axis=1), (1, 2))
    sd_up = jnp.tile(jnp.concatenate([-sin_p, zeros(hp + rest)], axis=1), (1, 2))
    sd_dn = jnp.tile(jnp.concatenate([zeros(hp), sin_p, zeros(rest)], axis=1), (1, 2))
    pad = LANES - MLA_ROPE_DIM
    cm = jnp.concatenate([cos_m, cos_m, zeros(pad)], axis=1)
    sm_up = jnp.concatenate([-sin_m, zeros(hm + pad)], axis=1)
    sm_dn = jnp.concatenate([zeros(hm), sin_m, zeros(pad)], axis=1)
    return cd, sd_up, sd_dn, cm, sm_up, sm_dn


def _tile(seq, want):
    return min(seq, want)


def kernel(x, norm_mix, w_in, diff_q_norm, diff_k_norm, diff_lambda_q1, diff_lambda_k1,
           diff_lambda_q2, diff_lambda_k2, diff_subln, fox_q_norm, fox_k_norm, fox_forget_bias,
           mla_q_a_norm, mla_kv_a_norm, mla_w_q_up, mla_w_kv_up, mla_q_norm, mla_k_norm,
           w_out, norm_ffn, w_ff_up, w_ff_down):
    batch, seq, d = x.shape
    assert batch == 1 and d == D_MODEL and seq % LANES == 0
    tabs = _rope_tables(seq)
    xs = x.reshape(seq, d)
    tq = _tile(seq, 512)
    assert sum(IN_SECTIONS) == w_in.shape[-1]
    w_out_b, w_up_b, w_down_b = w_out.astype(BF16), w_ff_up.astype(BF16), w_ff_down.astype(BF16)
    for l in range(DEPTH):
        proj = _in_proj(xs, norm_mix[l].reshape(1, d), w_in, l, tm=_tile(seq, 512))
        pvec = _pack_vec(diff_q_norm[l], diff_k_norm[l], fox_q_norm[l], fox_k_norm[l],
                         fox_forget_bias[l], mla_q_a_norm[l], mla_kv_a_norm[l],
                         mla_q_norm[l], mla_k_norm[l])
        (qd, kd, vd, qf, kf, vf, qm, km, vm) = _prep(
            proj, tabs, pvec, _pack_w_q_up(mla_w_q_up[l]), _pack_w_kv_up(mla_w_kv_up[l]),
            tm=_tile(seq, 256))
        lambda_init = 0.8 - 0.6 * math.exp(-0.3 * l)
        lam = jnp.stack([diff_lambda_q1[l], diff_lambda_k1[l], diff_lambda_q2[l], diff_lambda_k2[l]])
        o_a = _attention(qd, kd, vd, (lam, diff_subln[l].reshape(1, DIFF_V_DIM)), mode="diff",
                         tq=tq, lambda_init=lambda_init)
        o_b = _attention(qf, kf, vf, (), mode="fox", tq=tq)
        o_c = _attention(qm, km, vm, (), mode="mla", tq=tq)
        xs = _out_proj(xs, o_a, o_b, o_c, w_out_b, l, tm=_tile(seq, 512))
        xs = _ffn(xs, norm_ffn[l].reshape(1, d), w_up_b, w_down_b, l, tm=_tile(seq, 512), tf=1024)
    return xs.reshape(batch, seq, d)
```

```python
import functools
import math

import jax
import jax.numpy as jnp
from jax import lax
from jax.experimental import pallas as pl
from jax.experimental.pallas import tpu as pltpu

D_MODEL = 2048
DEPTH = 2
N_HEADS_DIFF = 4
DIFF_QK_DIM = 64
DIFF_V_DIM = 128
N_HEADS_FOX = 6
FOX_HEAD_DIM = 128
N_HEADS_MLA = 6
MLA_Q_RANK = 512
MLA_KV_RANK = 256
MLA_NOPE_DIM = 128
MLA_ROPE_DIM = 64
MLA_QK_DIM = MLA_ROPE_DIM + MLA_NOPE_DIM
MLA_V_DIM = 128
D_FF = 4 * D_MODEL
ROPE_THETA = 500000.0
PARTIAL_ROT_DIM = DIFF_QK_DIM // 4
EPS = 1e-6

DIFF_Q_COLS = N_HEADS_DIFF * 2 * DIFF_QK_DIM
DIFF_V_COLS = N_HEADS_DIFF * DIFF_V_DIM
FOX_COLS = N_HEADS_FOX * FOX_HEAD_DIM
MLA_V_COLS = N_HEADS_MLA * MLA_V_DIM
IN_SECTIONS = (DIFF_Q_COLS, DIFF_Q_COLS, DIFF_V_COLS, FOX_COLS, FOX_COLS, FOX_COLS,
               N_HEADS_FOX, MLA_Q_RANK, MLA_KV_RANK + MLA_ROPE_DIM)

LANES = 128
FOX_PAD_DIM = 2 * LANES
DECAY_PIECES = 3
BF16_ROWS = 16
MXU_COLS = 256
UNITS_PER_TRIP = 8
IN_ROW_CHUNKS = 4
MLA_PAD_DIM = 2 * LANES
VMEM_LIMIT_BYTES = 56 * 1024 * 1024

OFF_DQ = 0
OFF_DK = OFF_DQ + DIFF_Q_COLS
OFF_DV = OFF_DK + DIFF_Q_COLS
OFF_FQ = OFF_DV + DIFF_V_COLS
OFF_FK = OFF_FQ + FOX_COLS
OFF_FV = OFF_FK + FOX_COLS
MAIN_COLS = OFF_FV + FOX_COLS
TAIL_SRC_COLS = N_HEADS_FOX + MLA_Q_RANK + MLA_KV_RANK + MLA_ROPE_DIM
TAIL_MQ = 0
TAIL_CKV = TAIL_MQ + MLA_Q_RANK
TAIL_KR = TAIL_CKV + MLA_KV_RANK
TAIL_FF = TAIL_KR + LANES
TAIL_COLS = TAIL_FF + LANES
IN_TILE = MAIN_COLS // 3
PROJ_COLS = MAIN_COLS + IN_TILE
assert TAIL_COLS <= IN_TILE and IN_TILE % LANES == 0

PV_G_DQ = 0
PV_G_DK = PV_G_DQ + LANES
PV_G_FQ = PV_G_DK + LANES
PV_G_FK = PV_G_FQ + LANES
PV_F_BIAS = PV_G_FK + LANES
PV_G_QA = PV_F_BIAS + LANES
PV_G_KVA = PV_G_QA + MLA_Q_RANK
PV_G_QM = PV_G_KVA + MLA_KV_RANK
PV_G_KM = PV_G_QM + MLA_PAD_DIM
PV_LEN = PV_G_KM + MLA_PAD_DIM

NEG_BIG = -1e30
LOG2E = math.log2(math.e)

BF16 = jnp.bfloat16
F32 = jnp.float32


def _params(*semantics):
    return pltpu.CompilerParams(dimension_semantics=semantics,
                                vmem_limit_bytes=VMEM_LIMIT_BYTES)


def _in_proj_kernel(x_ref, g_ref, w_ref, o_ref, wb_ref):
    j = pl.program_id(0)
    first_row_tile = pl.program_id(1) == 0
    n_main = MAIN_COLS // IN_TILE

    @pl.when(jnp.logical_and(first_row_tile, j < n_main))
    def _():
        wb_ref[...] = w_ref[...].astype(BF16)

    @pl.when(jnp.logical_and(first_row_tile, j == n_main))
    def _():
        width = TAIL_KR + LANES
        w = w_ref[:, 0:width]
        r = pltpu.roll(w, width - N_HEADS_FOX, 1)
        lane = lax.broadcasted_iota(jnp.int32, (w.shape[0], LANES), 1)
        wb_ref[:, TAIL_MQ:TAIL_KR] = r[:, TAIL_MQ:TAIL_KR].astype(BF16)
        wb_ref[:, TAIL_KR:TAIL_FF] = jnp.where(lane < MLA_ROPE_DIM, r[:, TAIL_KR:TAIL_FF], 0.0).astype(BF16)
        wb_ref[:, TAIL_FF:TAIL_COLS] = jnp.where(lane < N_HEADS_FOX, w[:, 0:LANES], 0.0).astype(BF16)
        wb_ref[:, TAIL_COLS:] = jnp.zeros((w.shape[0], IN_TILE - TAIL_COLS), BF16)

    rows = x_ref.shape[0]
    chunk = rows // IN_ROW_CHUNKS if rows % (8 * IN_ROW_CHUNKS) == 0 else rows
    for r0 in range(0, rows, chunk):
        x = x_ref[r0:r0 + chunk, :]
        inv = lax.rsqrt(jnp.mean(x * x, axis=-1, keepdims=True) + EPS)
        h = (x * inv * g_ref[...]).astype(BF16)
        o_ref[r0:r0 + chunk, :] = jnp.dot(h, wb_ref[...], preferred_element_type=F32)


def _in_proj(x, g, w_in, layer, *, tm):
    s, d = x.shape
    assert w_in.shape[-1] == MAIN_COLS + TAIL_SRC_COLS
    return pl.pallas_call(
        _in_proj_kernel,
        grid=(PROJ_COLS // IN_TILE, s // tm),
        in_specs=[pl.BlockSpec((tm, d), lambda j, i: (i, 0)),
                  pl.BlockSpec((1, d), lambda j, i: (0, 0)),
                  pl.BlockSpec((None, d, IN_TILE), lambda j, i: (layer, 0, j))],
        out_specs=pl.BlockSpec((tm, IN_TILE), lambda j, i: (i, j)),
        out_shape=jax.ShapeDtypeStruct((s, PROJ_COLS), F32),
        scratch_shapes=[pltpu.VMEM((d, IN_TILE), BF16)],
        compiler_params=_params("arbitrary", "arbitrary"),
        name="in_proj",
    )(x, g, w_in)


def _rms(x, n_valid):
    ss = jnp.sum(x * x, axis=-1, keepdims=True)
    return x * lax.rsqrt(ss * (1.0 / n_valid) + EPS)


def _rms_two_halves(x):
    lo = lax.broadcasted_iota(jnp.int32, x.shape, 1) < DIFF_QK_DIM
    x2 = x * x
    s_lo = jnp.sum(jnp.where(lo, x2, 0.0), axis=-1, keepdims=True)
    s_hi = jnp.sum(jnp.where(lo, 0.0, x2), axis=-1, keepdims=True)
    return x * lax.rsqrt(jnp.where(lo, s_lo, s_hi) * (1.0 / DIFF_QK_DIM) + EPS)


def _rope(x, c, s_up, s_dn, half):
    return x * c + pltpu.roll(x, LANES - half, 1) * s_up + pltpu.roll(x, half, 1) * s_dn


def _identity_bf16(n):
    row = lax.broadcasted_iota(jnp.int32, (n, n), 0)
    col = lax.broadcasted_iota(jnp.int32, (n, n), 1)
    return jnp.where(row == col, 1.0, 0.0).astype(BF16)


def _t_bf16(x, eye):
    return lax.dot_general(eye, x.astype(BF16), (((1,), (1,)), ((), ())),
                           preferred_element_type=F32).astype(BF16)


def _prep_kernel(proj_ref, tail_ref, cd_ref, sd_up_ref, sd_dn_ref, cm_ref, sm_up_ref, sm_dn_ref,
                 pv_ref, wq_ref, wkv_ref,
                 qd_ref, kd_ref, vd_ref, qf_ref, kf_ref, vf_ref,
                 qm_ref, km_ref, vm_ref, carry_ref, *, tm):
    @pl.when(pl.program_id(0) == 0)
    def _():
        carry_ref[...] = jnp.zeros_like(carry_ref)

    def pv(off, n):
        return pv_ref[:, off:off + n]

    def lanes(ref, off, n=LANES):
        return ref[:, off:off + n]

    cd, sd_up, sd_dn = cd_ref[...], sd_up_ref[...], sd_dn_ref[...]
    cm, sm_up, sm_dn = cm_ref[...], sm_up_ref[...], sm_dn_ref[...]
    half_d = PARTIAL_ROT_DIM // 2
    half_m = MLA_ROPE_DIM // 2
    eye = _identity_bf16(LANES)

    g_dq, g_dk = pv(PV_G_DQ, LANES), pv(PV_G_DK, LANES)
    for b in range(N_HEADS_DIFF):
        q = _rope(_rms_two_halves(lanes(proj_ref, OFF_DQ + b * LANES)) * g_dq, cd, sd_up, sd_dn, half_d)
        qd_ref[b] = _t_bf16(q * (DIFF_QK_DIM ** -0.5 * LOG2E), eye)
        k = _rope(_rms_two_halves(lanes(proj_ref, OFF_DK + b * LANES)) * g_dk, cd, sd_up, sd_dn, half_d)
        kd_ref[:, b * LANES:(b + 1) * LANES] = k.astype(BF16)
        vd_ref[b] = _t_bf16(lanes(proj_ref, OFF_DV + b * LANES), eye)

    z = lanes(tail_ref, TAIL_FF) + pv(PV_F_BIAS, LANES)
    c = jnp.minimum(z, 0.0) - jnp.log1p(jnp.exp(-jnp.abs(z)))
    row = lax.broadcasted_iota(jnp.int32, c.shape, 0)
    shift = 1
    while shift < tm:
        c = c + jnp.where(row >= shift, pltpu.roll(c, shift, 0), 0.0)
        shift *= 2
    c = c + carry_ref[...]
    carry_ref[...] = c[tm - 1:tm, :]
    c2 = c * LOG2E

    g_fq, g_fk = pv(PV_G_FQ, LANES), pv(PV_G_FK, LANES)
    lane = lax.broadcasted_iota(jnp.int32, (tm, LANES), 1)
    ones_rows = jnp.where(lax.broadcasted_iota(jnp.int32, (LANES, tm), 0) < DECAY_PIECES, 1.0, 0.0)
    for h in range(N_HEADS_FOX):
        base = h * FOX_PAD_DIM
        q = _rms(lanes(proj_ref, OFF_FQ + h * LANES), FOX_HEAD_DIM) * g_fq
        qf_ref[h, 0:LANES, :] = _t_bf16(q * (FOX_HEAD_DIM ** -0.5 * LOG2E), eye)
        qf_ref[h, LANES:, :] = ones_rows.astype(BF16)
        k = _rms(lanes(proj_ref, OFF_FK + h * LANES), FOX_HEAD_DIM) * g_fk
        kf_ref[:, base:base + LANES] = k.astype(BF16)
        rest = -jnp.broadcast_to(c2[:, h:h + 1], (tm, LANES))
        decay = jnp.zeros((tm, LANES), F32)
        for piece in range(DECAY_PIECES):
            part = rest.astype(BF16).astype(F32)
            decay = jnp.where(lane == piece, part, decay)
            rest = rest - part
        kf_ref[:, base + LANES:base + FOX_PAD_DIM] = decay.astype(BF16)
        vf_ref[h] = _t_bf16(lanes(proj_ref, OFF_FV + h * LANES), eye)

    q_lat = (_rms(lanes(tail_ref, TAIL_MQ, MLA_Q_RANK), MLA_Q_RANK) * pv(PV_G_QA, MLA_Q_RANK)).astype(BF16)
    q_up = jnp.dot(q_lat, wq_ref[...], preferred_element_type=F32)
    c_kv = (_rms(lanes(tail_ref, TAIL_CKV, MLA_KV_RANK), MLA_KV_RANK) * pv(PV_G_KVA, MLA_KV_RANK)).astype(BF16)
    kv_up = jnp.dot(c_kv, wkv_ref[...], preferred_element_type=F32)
    g_qn, g_qr = pv(PV_G_QM, LANES), pv(PV_G_QM + LANES, LANES)
    g_kn, g_kr = pv(PV_G_KM, LANES), pv(PV_G_KM + LANES, LANES)
    k_rope = _rope(_rms(lanes(tail_ref, TAIL_KR), MLA_ROPE_DIM) * g_kr, cm, sm_up, sm_dn, half_m).astype(BF16)
    mla_scale = MLA_QK_DIM ** -0.5 * LOG2E
    v_off = N_HEADS_MLA * MLA_NOPE_DIM
    for h in range(N_HEADS_MLA):
        base = h * MLA_PAD_DIM
        q_nope = _rms(q_up[:, base:base + LANES], MLA_NOPE_DIM) * g_qn
        q_rope = _rope(_rms(q_up[:, base + LANES:base + 2 * LANES], MLA_ROPE_DIM) * g_qr,
                       cm, sm_up, sm_dn, half_m)
        qm_ref[h, 0:LANES, :] = _t_bf16(q_nope * mla_scale, eye)
        qm_ref[h, LANES:2 * LANES, :] = _t_bf16(q_rope * mla_scale, eye)
        k_nope = _rms(kv_up[:, h * LANES:(h + 1) * LANES], MLA_NOPE_DIM) * g_kn
        km_ref[:, base:base + LANES] = k_nope.astype(BF16)
        km_ref[:, base + LANES:base + 2 * LANES] = k_rope
        vm_ref[h] = _t_bf16(kv_up[:, v_off + h * LANES:v_off + (h + 1) * LANES], eye)


def _prep(proj, tabs, pvec, wq, wkv, *, tm):
    s = proj.shape[0]
    tail_blk = pl.BlockSpec((tm, IN_TILE), lambda i: (i, MAIN_COLS // IN_TILE))
    row_blk = lambda n: pl.BlockSpec((tm, n), lambda i: (i, 0))
    full = lambda a: pl.BlockSpec(a.shape, lambda i: (0, 0))

    def head_t(n_heads, dim):
        return (jax.ShapeDtypeStruct((n_heads, dim, s), BF16),
                pl.BlockSpec((n_heads, dim, tm), lambda i: (0, 0, i)))

    def rows(n, dtype=BF16):
        return jax.ShapeDtypeStruct((s, n), dtype), row_blk(n)

    outs = [head_t(N_HEADS_DIFF, LANES), rows(DIFF_Q_COLS), head_t(N_HEADS_DIFF, DIFF_V_DIM),
            head_t(N_HEADS_FOX, FOX_PAD_DIM), rows(N_HEADS_FOX * FOX_PAD_DIM),
            head_t(N_HEADS_FOX, FOX_HEAD_DIM),
            head_t(N_HEADS_MLA, MLA_PAD_DIM), rows(N_HEADS_MLA * MLA_PAD_DIM),
            head_t(N_HEADS_MLA, MLA_V_DIM)]
    return pl.pallas_call(
        functools.partial(_prep_kernel, tm=tm),
        grid=(s // tm,),
        in_specs=[row_blk(MAIN_COLS), tail_blk] + [row_blk(LANES)] * len(tabs)
                 + [full(pvec), full(wq), full(wkv)],
        out_specs=[o[1] for o in outs],
        out_shape=[o[0] for o in outs],
        scratch_shapes=[pltpu.VMEM((1, LANES), F32)],
        compiler_params=_params("arbitrary"),
        name="prep",
    )(proj, proj, *tabs, pvec, wq, wkv)


def _attn_kernel(*refs, mode, tq, tiles, lambda_init, n_cast):
    n_in = 5 if mode == "diff" else 3
    q_ref, k_ref, v_ref = refs[:3]
    if mode == "diff":
        lam_ref, sub_ref = refs[3:5]
    cast_src = refs[n_in:n_in + n_cast]
    o_ref = refs[n_in + n_cast]
    cast_dst = refs[n_in + n_cast + 1:n_in + 2 * n_cast + 1]
    m_ref, l_ref, acc_ref, s_ref = refs[n_in + 2 * n_cast + 1:]
    for src, dst in zip(cast_src, cast_dst):
        dst[...] = src[...].astype(dst.dtype)
    n = pl.program_id(1) * tiles
    dv = o_ref.shape[1]
    cw = min(tq, MXU_COLS)
    n_sub = 2 if mode == "diff" else 1
    streams = [(t, sub) for t in range(tiles) for sub in range(n_sub)]

    m_ref[...] = jnp.full(m_ref.shape, NEG_BIG, F32)
    l_ref[...] = jnp.zeros(l_ref.shape, F32)
    acc_ref[...] = jnp.zeros(acc_ref.shape, F32)

    def q_stream(st):
        t, sub = streams[st]
        q_t = q_ref[:, t * tq:(t + 1) * tq]
        if mode == "diff":
            lo = lax.broadcasted_iota(jnp.int32, q_t.shape, 0) < DIFF_QK_DIM
            zero = jnp.zeros_like(q_t)
            q_t = jnp.where(lo, q_t, zero) if sub == 0 else jnp.where(lo, zero, q_t)
        return q_t

    def scores(j, st, buf):
        start = pl.multiple_of(j * tq, tq)
        s_ref[buf] = jnp.dot(k_ref[pl.ds(start, tq), :], q_stream(st), preferred_element_type=F32)

    def softmax_pv(j, st, buf, masked):
        start = pl.multiple_of(j * tq, tq)
        v_t = v_ref[:, pl.ds(start, tq)]
        for c0 in range(0, tq, cw):
            cols = slice(c0, c0 + cw)
            s_t = s_ref[buf, :, cols]
            if masked:
                key = lax.broadcasted_iota(jnp.int32, s_t.shape, 0)
                qry = lax.broadcasted_iota(jnp.int32, s_t.shape, 1) + c0
                s_t = jnp.where(key <= qry, s_t, NEG_BIG)
            m_prev = m_ref[st, :, cols]
            m_next = jnp.maximum(m_prev, jnp.max(s_t, axis=0, keepdims=True))
            alpha = jnp.exp2(m_prev - m_next)
            m_ref[st, :, cols] = m_next
            p_t = jnp.exp2(s_t - m_next)
            l_ref[st, :, cols] = alpha * l_ref[st, :, cols] + jnp.sum(p_t, axis=0, keepdims=True)
            acc_ref[st, :, cols] = (acc_ref[st, :, cols] * alpha
                                    + jnp.dot(v_t, p_t.astype(BF16), preferred_element_type=F32))

    def run(units, following):
        assert len(units) % 2 == 0 or following is None
        seq = units + ([following] if following is not None else [])
        for pos, (j, st, masked) in enumerate(units):
            if pos + 1 < len(seq):
                scores(seq[pos + 1][0], seq[pos + 1][1], (pos + 1) % 2)
            softmax_pv(j, st, pos % 2, masked)

    def full_block(j):
        return [(j, st, False) for st in range(len(streams))]

    blocks_per_trip = max(1, UNITS_PER_TRIP // len(streams))
    assert blocks_per_trip & (blocks_per_trip - 1) == 0

    def trip(t, carry):
        base = t * blocks_per_trip
        units = [u for b in range(blocks_per_trip) for u in full_block(base + b)]
        run(units, (base + blocks_per_trip, 0, False))
        return carry

    tail = [(n + b, st, b == streams[st][0])
            for b in range(tiles) for st in range(len(streams)) if streams[st][0] >= b]

    scores(0, 0, 0)
    lax.fori_loop(0, lax.shift_right_logical(n, blocks_per_trip.bit_length() - 1), trip, 0)
    left = lax.bitwise_and(n, blocks_per_trip - 1)
    for r in range(0, blocks_per_trip, math.gcd(tiles, blocks_per_trip)):
        @pl.when(left == r)
        def _():
            run([u for b in range(r) for u in full_block(n - r + b)] + tail, None)

    def out_t(st):
        return acc_ref[st] * (1.0 / l_ref[st])

    if mode == "diff":
        lp = lam_ref[...]
        lam = (jnp.exp(jnp.sum(lp[0:1] * lp[1:2], axis=1, keepdims=True))
               - jnp.exp(jnp.sum(lp[2:3] * lp[3:4], axis=1, keepdims=True)) + lambda_init)
    for t in range(tiles):
        rows = slice(t * tq, (t + 1) * tq)
        if mode == "diff":
            o = (out_t(2 * t) - lam * out_t(2 * t + 1)).T
            o = o * lax.rsqrt(jnp.mean(o * o, axis=-1, keepdims=True) + EPS) * sub_ref[...]
            o_ref[rows, :] = (o * (1.0 - lambda_init)).astype(o_ref.dtype)
        else:
            o_ref[rows, :] = out_t(t).T.astype(o_ref.dtype)


def _attn_grid(n_heads, s, tq):
    n_tiles = s // tq
    tiles = 2 if n_tiles % 2 == 0 else 1
    return tiles, (n_heads, n_tiles // tiles)


def _can_ride(weights, n_steps):
    return all(w.shape[1] % (n_steps * BF16_ROWS) == 0 for w in weights)


def _attention(q_t, k, v_t, extras, *, mode, tq, lambda_init=0.0, cast=(), cast_layer=0):
    n_heads, dk, s = q_t.shape
    dv = v_t.shape[1]
    tiles, grid = _attn_grid(n_heads, s, tq)
    n_streams = tiles * (2 if mode == "diff" else 1)
    in_specs = [pl.BlockSpec((None, dk, tiles * tq), lambda h, i: (h, 0, i)),
                pl.BlockSpec((s, dk), lambda h, i: (0, h)),
                pl.BlockSpec((None, dv, s), lambda h, i: (h, 0, 0))]
    if mode == "diff":
        lam, sub = extras
        in_specs += [pl.BlockSpec(lam.shape, lambda h, i: (0, 0)),
                     pl.BlockSpec(sub.shape, lambda h, i: (0, 0))]
    out_specs = [pl.BlockSpec((tiles * tq, dv), lambda h, i: (i, h))]
    out_shape = [jax.ShapeDtypeStruct((s, n_heads * dv), BF16)]
    n_steps = grid[0] * grid[1]
    assert _can_ride(cast, n_steps)
    for w in cast:
        rows = w.shape[1] // n_steps
        in_specs.append(pl.BlockSpec((None, rows, w.shape[2]), lambda h, i: (cast_layer, h * grid[1] + i, 0)))
        out_specs.append(pl.BlockSpec((rows, w.shape[2]), lambda h, i: (h * grid[1] + i, 0)))
        out_shape.append(jax.ShapeDtypeStruct(w.shape[1:], BF16))
    outs = pl.pallas_call(
        functools.partial(_attn_kernel, mode=mode, tq=tq, tiles=tiles, lambda_init=lambda_init,
                          n_cast=len(cast)),
        grid=grid,
        in_specs=in_specs,
        out_specs=out_specs,
        out_shape=out_shape,
        scratch_shapes=[pltpu.VMEM((n_streams, 1, tq), F32),
                        pltpu.VMEM((n_streams, 1, tq), F32),
                        pltpu.VMEM((n_streams, dv, tq), F32),
                        pltpu.VMEM((2, tq, tq), F32)],
        compiler_params=_params("arbitrary", "arbitrary"),
        name="attn_" + mode,
    )(q_t, k, v_t, *extras, *cast)
    return outs if cast else outs[0]


def _out_proj_kernel(x_ref, oa_ref, ob_ref, oc_ref, w_ref, o_ref):
    a0, a1 = DIFF_V_COLS, DIFF_V_COLS + FOX_COLS
    acc = x_ref[...]
    acc = acc + jnp.dot(oa_ref[...], w_ref[0:a0, :], preferred_element_type=F32)
    acc = acc + jnp.dot(ob_ref[...], w_ref[a0:a1, :], preferred_element_type=F32)
    acc = acc + jnp.dot(oc_ref[...], w_ref[a1:, :], preferred_element_type=F32)
    o_ref[...] = acc


def _out_proj(x, oa, ob, oc, w, layer, *, tm):
    s, d = x.shape
    row_blk = lambda n: pl.BlockSpec((tm, n), lambda i: (i, 0))
    return pl.pallas_call(
        _out_proj_kernel,
        grid=(s // tm,),
        in_specs=[row_blk(d), row_blk(oa.shape[1]), row_blk(ob.shape[1]), row_blk(oc.shape[1]),
                  pl.BlockSpec((None,) + w.shape[1:], lambda i: (layer, 0, 0))],
        out_specs=row_blk(d),
        out_shape=jax.ShapeDtypeStruct((s, d), F32),
        compiler_params=_params("arbitrary"),
        name="out_proj",
    )(x, oa, ob, oc, w)


def _ffn_kernel(x_ref, g_ref, wu_ref, wd_ref, o_ref, h_ref):
    @pl.when(pl.program_id(1) == 0)
    def _():
        x = x_ref[...]
        inv = lax.rsqrt(jnp.mean(x * x, axis=-1, keepdims=True) + EPS)
        h_ref[...] = (x * inv * g_ref[...]).astype(BF16)
        o_ref[...] = x

    u = jnp.dot(h_ref[...], wu_ref[...], preferred_element_type=F32)
    a = jnp.square(jnp.maximum(u, 0.0)).astype(BF16)
    o_ref[...] += jnp.dot(a, wd_ref[...], preferred_element_type=F32)


def _ffn(x, g, wu, wd, layer, *, tm, tf):
    s, d = x.shape
    f = wu.shape[2]
    return pl.pallas_call(
        _ffn_kernel,
        grid=(s // tm, f // tf),
        in_specs=[pl.BlockSpec((tm, d), lambda i, j: (i, 0)),
                  pl.BlockSpec((1, d), lambda i, j: (0, 0)),
                  pl.BlockSpec((None, d, tf), lambda i, j: (layer, 0, j)),
                  pl.BlockSpec((None, tf, d), lambda i, j: (layer, j, 0))],
        out_specs=pl.BlockSpec((tm, d), lambda i, j: (i, 0)),
        out_shape=jax.ShapeDtypeStruct((s, d), F32),
        scratch_shapes=[pltpu.VMEM((tm, d), BF16)],
        compiler_params=_params("arbitrary", "arbitrary"),
        name="ffn",
    )(x, g, wu, wd)


def _pack_w_q_up(w):
    w = w.reshape(MLA_Q_RANK, N_HEADS_MLA, MLA_QK_DIM)
    pad = jnp.zeros((MLA_Q_RANK, N_HEADS_MLA, MLA_PAD_DIM - MLA_QK_DIM), w.dtype)
    w = jnp.concatenate([w[:, :, MLA_ROPE_DIM:], w[:, :, :MLA_ROPE_DIM], pad], axis=2)
    return w.reshape(MLA_Q_RANK, N_HEADS_MLA * MLA_PAD_DIM).astype(BF16)


def _pack_w_kv_up(w):
    w = w.reshape(MLA_KV_RANK, N_HEADS_MLA, MLA_NOPE_DIM + MLA_V_DIM)
    k_nope = w[:, :, :MLA_NOPE_DIM].reshape(MLA_KV_RANK, N_HEADS_MLA * MLA_NOPE_DIM)
    v = w[:, :, MLA_NOPE_DIM:].reshape(MLA_KV_RANK, MLA_V_COLS)
    return jnp.concatenate([k_nope, v], axis=1).astype(BF16)


def _pack_vec(diff_q_norm, diff_k_norm, fox_q_norm, fox_k_norm, fox_forget_bias,
              mla_q_a_norm, mla_kv_a_norm, mla_q_norm, mla_k_norm):
    def nope_rope_pad(g):
        return jnp.concatenate([g[MLA_ROPE_DIM:], g[:MLA_ROPE_DIM],
                                jnp.zeros((MLA_PAD_DIM - MLA_QK_DIM,), g.dtype)])
    parts = [jnp.tile(diff_q_norm, 2), jnp.tile(diff_k_norm, 2), fox_q_norm, fox_k_norm,
             jnp.pad(fox_forget_bias, (0, LANES - N_HEADS_FOX)),
             mla_q_a_norm, mla_kv_a_norm, nope_rope_pad(mla_q_norm), nope_rope_pad(mla_k_norm)]
    return jnp.concatenate(parts).astype(F32).reshape(1, PV_LEN)


def _rope_tables(seq):
    def cos_sin(rot_dim):
        half = rot_dim // 2
        inv_freq = ROPE_THETA ** (-jnp.arange(half, dtype=F32) / half)
        ang = jnp.arange(seq, dtype=F32)[:, None] * inv_freq[None, :]
        return jnp.cos(ang), jnp.sin(ang)

    cos_p, sin_p = cos_sin(PARTIAL_ROT_DIM)
    cos_m, sin_m = cos_sin(MLA_ROPE_DIM)
    hp, hm = PARTIAL_ROT_DIM // 2, MLA_ROPE_DIM // 2
    ones = lambda n: jnp.ones((seq, n), F32)
    zeros = lambda n: jnp.zeros((seq, n), F32)
    rest = DIFF_QK_DIM - PARTIAL_ROT_DIM
    cd = jnp.tile(jnp.concatenate([cos_p, cos_p, ones(rest)], axis=1), (1, 2))
    sd_up = jnp.tile(jnp.concatenate([-sin_p, zeros(hp + rest)], axis=1), (1, 2))
    sd_dn = jnp.tile(jnp.concatenate([zeros(hp), sin_p, zeros(rest)], axis=1), (1, 2))
    pad = LANES - MLA_ROPE_DIM
    cm = jnp.concatenate([cos_m, cos_m, zeros(pad)], axis=1)
    sm_up = jnp.concatenate([-sin_m, zeros(hm + pad)], axis=1)
    sm_dn = jnp.concatenate([zeros(hm), sin_m, zeros(pad)], axis=1)
    return cd, sd_up, sd_dn, cm, sm_up, sm_dn


def _tile(seq, want):
    return min(seq, want)


def kernel(x, norm_mix, w_in, diff_q_norm, diff_k_norm, diff_lambda_q1, diff_lambda_k1,
           diff_lambda_q2, diff_lambda_k2, diff_subln, fox_q_norm, fox_k_norm, fox_forget_bias,
           mla_q_a_norm, mla_kv_a_norm, mla_w_q_up, mla_w_kv_up, mla_q_norm, mla_k_norm,
           w_out, norm_ffn, w_ff_up, w_ff_down):
    batch, seq, d = x.shape
    assert batch == 1 and d == D_MODEL and seq % LANES == 0
    tabs = _rope_tables(seq)
    xs = x.reshape(seq, d)
    tq = _tile(seq, 512)
    assert sum(IN_SECTIONS) == w_in.shape[-1]
    w_out_b = w_out.astype(BF16)
    _, diff_grid = _attn_grid(N_HEADS_DIFF, seq, tq)
    ride = _can_ride((w_ff_up, w_ff_down), diff_grid[0] * diff_grid[1])
    for l in range(DEPTH):
        proj = _in_proj(xs, norm_mix[l].reshape(1, d), w_in, l, tm=_tile(seq, 512))
        pvec = _pack_vec(diff_q_norm[l], diff_k_norm[l], fox_q_norm[l], fox_k_norm[l],
                         fox_forget_bias[l], mla_q_a_norm[l], mla_kv_a_norm[l],
                         mla_q_norm[l], mla_k_norm[l])
        (qd, kd, vd, qf, kf, vf, qm, km, vm) = _prep(
            proj, tabs, pvec, _pack_w_q_up(mla_w_q_up[l]), _pack_w_kv_up(mla_w_kv_up[l]),
            tm=_tile(seq, 256))
        lambda_init = 0.8 - 0.6 * math.exp(-0.3 * l)
        lam = jnp.stack([diff_lambda_q1[l], diff_lambda_k1[l], diff_lambda_q2[l], diff_lambda_k2[l]])
        diff_extras = (lam, diff_subln[l].reshape(1, DIFF_V_DIM))
        if ride:
            o_a, w_up_b, w_down_b = _attention(qd, kd, vd, diff_extras, mode="diff", tq=tq,
                                               lambda_init=lambda_init,
                                               cast=(w_ff_up, w_ff_down), cast_layer=l)
        else:
            o_a = _attention(qd, kd, vd, diff_extras, mode="diff", tq=tq, lambda_init=lambda_init)
            w_up_b, w_down_b = w_ff_up[l].astype(BF16), w_ff_down[l].astype(BF16)
        o_b = _attention(qf, kf, vf, (), mode="fox", tq=tq)
        o_c = _attention(qm, km, vm, (), mode="mla", tq=tq)
        xs = _out_proj(xs, o_a, o_b, o_c, w_out_b, l, tm=_tile(seq, 512))
        xs = _ffn(xs, norm_ffn[l].reshape(1, d), w_up_b[None], w_down_b[None], 0,
                  tm=_tile(seq, 512), tf=1024)
    return xs.reshape(batch, seq, d)
```

```python
import functools
import math

import jax
import jax.numpy as jnp
from jax import lax
from jax.experimental import pallas as pl
from jax.experimental.pallas import tpu as pltpu

D_MODEL = 2048
DEPTH = 2
N_HEADS_DIFF = 4
DIFF_QK_DIM = 64
DIFF_V_DIM = 128
N_HEADS_FOX = 6
FOX_HEAD_DIM = 128
N_HEADS_MLA = 6
MLA_Q_RANK = 512
MLA_KV_RANK = 256
MLA_NOPE_DIM = 128
MLA_ROPE_DIM = 64
MLA_QK_DIM = MLA_ROPE_DIM + MLA_NOPE_DIM
MLA_V_DIM = 128
D_FF = 4 * D_MODEL
ROPE_THETA = 500000.0
PARTIAL_ROT_DIM = DIFF_QK_DIM // 4
EPS = 1e-6

DIFF_Q_COLS = N_HEADS_DIFF * 2 * DIFF_QK_DIM
DIFF_V_COLS = N_HEADS_DIFF * DIFF_V_DIM
FOX_COLS = N_HEADS_FOX * FOX_HEAD_DIM
MLA_V_COLS = N_HEADS_MLA * MLA_V_DIM
IN_SECTIONS = (DIFF_Q_COLS, DIFF_Q_COLS, DIFF_V_COLS, FOX_COLS, FOX_COLS, FOX_COLS,
               N_HEADS_FOX, MLA_Q_RANK, MLA_KV_RANK + MLA_ROPE_DIM)

LANES = 128
FOX_PAD_DIM = 2 * LANES
DECAY_PIECES = 3
BF16_ROWS = 16
MXU_COLS = 256
UNITS_PER_TRIP = 8
IN_ROW_CHUNKS = 2
TILES_PER_STEP = 4
MLA_PAD_DIM = 2 * LANES
VMEM_LIMIT_BYTES = 56 * 1024 * 1024

OFF_DQ = 0
OFF_DK = OFF_DQ + DIFF_Q_COLS
OFF_DV = OFF_DK + DIFF_Q_COLS
OFF_FQ = OFF_DV + DIFF_V_COLS
OFF_FK = OFF_FQ + FOX_COLS
OFF_FV = OFF_FK + FOX_COLS
MAIN_COLS = OFF_FV + FOX_COLS
TAIL_SRC_COLS = N_HEADS_FOX + MLA_Q_RANK + MLA_KV_RANK + MLA_ROPE_DIM
TAIL_MQ = 0
TAIL_CKV = TAIL_MQ + MLA_Q_RANK
TAIL_KR = TAIL_CKV + MLA_KV_RANK
TAIL_FF = TAIL_KR + LANES
TAIL_COLS = TAIL_FF + LANES
IN_TILE = MAIN_COLS // 3
PROJ_COLS = MAIN_COLS + IN_TILE
assert TAIL_COLS <= IN_TILE and IN_TILE % LANES == 0

PV_G_DQ = 0
PV_G_DK = PV_G_DQ + LANES
PV_G_FQ = PV_G_DK + LANES
PV_G_FK = PV_G_FQ + LANES
PV_F_BIAS = PV_G_FK + LANES
PV_G_QA = PV_F_BIAS + LANES
PV_G_KVA = PV_G_QA + MLA_Q_RANK
PV_G_QM = PV_G_KVA + MLA_KV_RANK
PV_G_KM = PV_G_QM + MLA_PAD_DIM
PV_LEN = PV_G_KM + MLA_PAD_DIM

NEG_BIG = -1e30
LOG2E = math.log2(math.e)

BF16 = jnp.bfloat16
F32 = jnp.float32


def _params(*semantics):
    return pltpu.CompilerParams(dimension_semantics=semantics,
                                vmem_limit_bytes=VMEM_LIMIT_BYTES)


def _in_proj_kernel(x_ref, g_ref, w_ref, o_ref, wb_ref):
    j = pl.program_id(0)
    first_row_tile = pl.program_id(1) == 0
    n_main = MAIN_COLS // IN_TILE

    @pl.when(jnp.logical_and(first_row_tile, j < n_main))
    def _():
        wb_ref[...] = w_ref[...].astype(BF16)

    @pl.when(jnp.logical_and(first_row_tile, j == n_main))
    def _():
        width = TAIL_KR + LANES
        w = w_ref[:, 0:width]
        r = pltpu.roll(w, width - N_HEADS_FOX, 1)
        lane = lax.broadcasted_iota(jnp.int32, (w.shape[0], LANES), 1)
        wb_ref[:, TAIL_MQ:TAIL_KR] = r[:, TAIL_MQ:TAIL_KR].astype(BF16)
        wb_ref[:, TAIL_KR:TAIL_FF] = jnp.where(lane < MLA_ROPE_DIM, r[:, TAIL_KR:TAIL_FF], 0.0).astype(BF16)
        wb_ref[:, TAIL_FF:TAIL_COLS] = jnp.where(lane < N_HEADS_FOX, w[:, 0:LANES], 0.0).astype(BF16)
        wb_ref[:, TAIL_COLS:] = jnp.zeros((w.shape[0], IN_TILE - TAIL_COLS), BF16)

    rows = x_ref.shape[0]
    chunk = rows // IN_ROW_CHUNKS if rows % (8 * IN_ROW_CHUNKS) == 0 else rows
    for r0 in range(0, rows, chunk):
        x = x_ref[r0:r0 + chunk, :]
        inv = lax.rsqrt(jnp.mean(x * x, axis=-1, keepdims=True) + EPS)
        h = (x * inv * g_ref[...]).astype(BF16)
        o_ref[r0:r0 + chunk, :] = jnp.dot(h, wb_ref[...], preferred_element_type=F32)


def _in_proj(x, g, w_in, layer, *, tm):
    s, d = x.shape
    assert w_in.shape[-1] == MAIN_COLS + TAIL_SRC_COLS
    return pl.pallas_call(
        _in_proj_kernel,
        grid=(PROJ_COLS // IN_TILE, s // tm),
        in_specs=[pl.BlockSpec((tm, d), lambda j, i: (i, 0)),
                  pl.BlockSpec((1, d), lambda j, i: (0, 0)),
                  pl.BlockSpec((None, d, IN_TILE), lambda j, i: (layer, 0, j))],
        out_specs=pl.BlockSpec((tm, IN_TILE), lambda j, i: (i, j)),
        out_shape=jax.ShapeDtypeStruct((s, PROJ_COLS), F32),
        scratch_shapes=[pltpu.VMEM((d, IN_TILE), BF16)],
        compiler_params=_params("arbitrary", "arbitrary"),
        name="in_proj",
    )(x, g, w_in)


def _rms(x, n_valid):
    ss = jnp.sum(x * x, axis=-1, keepdims=True)
    return x * lax.rsqrt(ss * (1.0 / n_valid) + EPS)


def _rms_two_halves(x):
    lo = lax.broadcasted_iota(jnp.int32, x.shape, 1) < DIFF_QK_DIM
    x2 = x * x
    s_lo = jnp.sum(jnp.where(lo, x2, 0.0), axis=-1, keepdims=True)
    s_hi = jnp.sum(jnp.where(lo, 0.0, x2), axis=-1, keepdims=True)
    return x * lax.rsqrt(jnp.where(lo, s_lo, s_hi) * (1.0 / DIFF_QK_DIM) + EPS)


def _shift_pair_bf16(half):
    src = lax.broadcasted_iota(jnp.int32, (LANES, 2 * LANES), 0)
    dst = lax.broadcasted_iota(jnp.int32, (LANES, 2 * LANES), 1)
    want = jnp.where(dst < LANES, dst + half, dst - LANES - half)
    return jnp.where(src == jnp.bitwise_and(want, LANES - 1), 1.0, 0.0).astype(BF16)


def _rope(x, c, s_up, s_dn, shift_pair):
    shifted = jnp.dot(x.astype(BF16), shift_pair, preferred_element_type=F32)
    return x * c + shifted[:, :LANES] * s_up + shifted[:, LANES:] * s_dn


def _identity_bf16(n):
    row = lax.broadcasted_iota(jnp.int32, (n, n), 0)
    col = lax.broadcasted_iota(jnp.int32, (n, n), 1)
    return jnp.where(row == col, 1.0, 0.0).astype(BF16)


def _t_bf16(x, eye):
    return lax.dot_general(eye, x.astype(BF16), (((1,), (1,)), ((), ())),
                           preferred_element_type=F32).astype(BF16)


def _prep_kernel(proj_ref, tail_ref, cd_ref, sd_up_ref, sd_dn_ref, cm_ref, sm_up_ref, sm_dn_ref,
                 pv_ref, wq_ref, wkv_ref,
                 qd_ref, kd_ref, vd_ref, qf_ref, kf_ref, vf_ref,
                 qm_ref, km_ref, vm_ref, carry_ref, *, tm):
    @pl.when(pl.program_id(0) == 0)
    def _():
        carry_ref[...] = jnp.zeros_like(carry_ref)

    def pv(off, n):
        return pv_ref[:, off:off + n]

    def lanes(ref, off, n=LANES):
        return ref[:, off:off + n]

    cd, sd_up, sd_dn = cd_ref[...], sd_up_ref[...], sd_dn_ref[...]
    cm, sm_up, sm_dn = cm_ref[...], sm_up_ref[...], sm_dn_ref[...]
    half_d = _shift_pair_bf16(PARTIAL_ROT_DIM // 2)
    half_m = _shift_pair_bf16(MLA_ROPE_DIM // 2)
    eye = _identity_bf16(LANES)

    g_dq, g_dk = pv(PV_G_DQ, LANES), pv(PV_G_DK, LANES)
    for b in range(N_HEADS_DIFF):
        q = _rope(_rms_two_halves(lanes(proj_ref, OFF_DQ + b * LANES)) * g_dq, cd, sd_up, sd_dn, half_d)
        qd_ref[b] = _t_bf16(q * (DIFF_QK_DIM ** -0.5 * LOG2E), eye)
        k = _rope(_rms_two_halves(lanes(proj_ref, OFF_DK + b * LANES)) * g_dk, cd, sd_up, sd_dn, half_d)
        kd_ref[:, b * LANES:(b + 1) * LANES] = k.astype(BF16)
        vd_ref[b] = _t_bf16(lanes(proj_ref, OFF_DV + b * LANES), eye)

    z = lanes(tail_ref, TAIL_FF) + pv(PV_F_BIAS, LANES)
    c = jnp.minimum(z, 0.0) - jnp.log1p(jnp.exp(-jnp.abs(z)))
    row = lax.broadcasted_iota(jnp.int32, c.shape, 0)
    shift = 1
    while shift < tm:
        c = c + jnp.where(row >= shift, pltpu.roll(c, shift, 0), 0.0)
        shift *= 2
    c = c + carry_ref[...]
    carry_ref[...] = c[tm - 1:tm, :]
    c2 = c * LOG2E

    g_fq, g_fk = pv(PV_G_FQ, LANES), pv(PV_G_FK, LANES)
    lane = lax.broadcasted_iota(jnp.int32, (tm, LANES), 1)
    ones_rows = jnp.where(lax.broadcasted_iota(jnp.int32, (LANES, tm), 0) < DECAY_PIECES, 1.0, 0.0)
    for h in range(N_HEADS_FOX):
        base = h * FOX_PAD_DIM
        q = _rms(lanes(proj_ref, OFF_FQ + h * LANES), FOX_HEAD_DIM) * g_fq
        qf_ref[h, 0:LANES, :] = _t_bf16(q * (FOX_HEAD_DIM ** -0.5 * LOG2E), eye)
        qf_ref[h, LANES:, :] = ones_rows.astype(BF16)
        k = _rms(lanes(proj_ref, OFF_FK + h * LANES), FOX_HEAD_DIM) * g_fk
        kf_ref[:, base:base + LANES] = k.astype(BF16)
        rest = -jnp.broadcast_to(c2[:, h:h + 1], (tm, LANES))
        decay = jnp.zeros((tm, LANES), F32)
        for piece in range(DECAY_PIECES):
            part = rest.astype(BF16).astype(F32)
            decay = jnp.where(lane == piece, part, decay)
            rest = rest - part
        kf_ref[:, base + LANES:base + FOX_PAD_DIM] = decay.astype(BF16)
        vf_ref[h] = _t_bf16(lanes(proj_ref, OFF_FV + h * LANES), eye)

    q_lat = (_rms(lanes(tail_ref, TAIL_MQ, MLA_Q_RANK), MLA_Q_RANK) * pv(PV_G_QA, MLA_Q_RANK)).astype(BF16)
    q_up = jnp.dot(q_lat, wq_ref[...], preferred_element_type=F32)
    c_kv = (_rms(lanes(tail_ref, TAIL_CKV, MLA_KV_RANK), MLA_KV_RANK) * pv(PV_G_KVA, MLA_KV_RANK)).astype(BF16)
    kv_up = jnp.dot(c_kv, wkv_ref[...], preferred_element_type=F32)
    g_qn, g_qr = pv(PV_G_QM, LANES), pv(PV_G_QM + LANES, LANES)
    g_kn, g_kr = pv(PV_G_KM, LANES), pv(PV_G_KM + LANES, LANES)
    k_rope = _rope(_rms(lanes(tail_ref, TAIL_KR), MLA_ROPE_DIM) * g_kr, cm, sm_up, sm_dn, half_m).astype(BF16)
    mla_scale = MLA_QK_DIM ** -0.5 * LOG2E
    v_off = N_HEADS_MLA * MLA_NOPE_DIM
    for h in range(N_HEADS_MLA):
        base = h * MLA_PAD_DIM
        q_nope = _rms(q_up[:, base:base + LANES], MLA_NOPE_DIM) * g_qn
        q_rope = _rope(_rms(q_up[:, base + LANES:base + 2 * LANES], MLA_ROPE_DIM) * g_qr,
                       cm, sm_up, sm_dn, half_m)
        qm_ref[h, 0:LANES, :] = _t_bf16(q_nope * mla_scale, eye)
        qm_ref[h, LANES:2 * LANES, :] = _t_bf16(q_rope * mla_scale, eye)
        k_nope = _rms(kv_up[:, h * LANES:(h + 1) * LANES], MLA_NOPE_DIM) * g_kn
        km_ref[:, base:base + LANES] = k_nope.astype(BF16)
        km_ref[:, base + LANES:base + 2 * LANES] = k_rope
        vm_ref[h] = _t_bf16(kv_up[:, v_off + h * LANES:v_off + (h + 1) * LANES], eye)


def _prep(proj, tabs, pvec, wq, wkv, *, tm):
    s = proj.shape[0]
    tail_blk = pl.BlockSpec((tm, IN_TILE), lambda i: (i, MAIN_COLS // IN_TILE))
    row_blk = lambda n: pl.BlockSpec((tm, n), lambda i: (i, 0))
    full = lambda a: pl.BlockSpec(a.shape, lambda i: (0, 0))

    def head_t(n_heads, dim):
        return (jax.ShapeDtypeStruct((n_heads, dim, s), BF16),
                pl.BlockSpec((n_heads, dim, tm), lambda i: (0, 0, i)))

    def rows(n, dtype=BF16):
        return jax.ShapeDtypeStruct((s, n), dtype), row_blk(n)

    outs = [head_t(N_HEADS_DIFF, LANES), rows(DIFF_Q_COLS), head_t(N_HEADS_DIFF, DIFF_V_DIM),
            head_t(N_HEADS_FOX, FOX_PAD_DIM), rows(N_HEADS_FOX * FOX_PAD_DIM),
            head_t(N_HEADS_FOX, FOX_HEAD_DIM),
            head_t(N_HEADS_MLA, MLA_PAD_DIM), rows(N_HEADS_MLA * MLA_PAD_DIM),
            head_t(N_HEADS_MLA, MLA_V_DIM)]
    return pl.pallas_call(
        functools.partial(_prep_kernel, tm=tm),
        grid=(s // tm,),
        in_specs=[row_blk(MAIN_COLS), tail_blk] + [row_blk(LANES)] * len(tabs)
                 + [full(pvec), full(wq), full(wkv)],
        out_specs=[o[1] for o in outs],
        out_shape=[o[0] for o in outs],
        scratch_shapes=[pltpu.VMEM((1, LANES), F32)],
        compiler_params=_params("arbitrary"),
        name="prep",
    )(proj, proj, *tabs, pvec, wq, wkv)


def _attn_kernel(*refs, mode, tq, tiles, lambda_init, n_cast):
    n_in = 5 if mode == "diff" else 3
    q_ref, k_ref, v_ref = refs[:3]
    if mode == "diff":
        lam_ref, sub_ref = refs[3:5]
    cast_src = refs[n_in:n_in + n_cast]
    o_ref = refs[n_in + n_cast]
    cast_dst = refs[n_in + n_cast + 1:n_in + 2 * n_cast + 1]
    m_ref, l_ref, acc_ref, s_ref = refs[n_in + 2 * n_cast + 1:]
    for src, dst in zip(cast_src, cast_dst):
        dst[...] = src[...].astype(dst.dtype)
    n = pl.program_id(1) * tiles
    dv = o_ref.shape[1]
    cw = min(tq, MXU_COLS)
    n_sub = 2 if mode == "diff" else 1
    streams = [(t, sub) for t in range(tiles) for sub in range(n_sub)]

    m_ref[...] = jnp.full(m_ref.shape, NEG_BIG, F32)
    l_ref[...] = jnp.zeros(l_ref.shape, F32)
    acc_ref[...] = jnp.zeros(acc_ref.shape, F32)

    def q_stream(st):
        t, sub = streams[st]
        q_t = q_ref[:, t * tq:(t + 1) * tq]
        if mode == "diff":
            lo = lax.broadcasted_iota(jnp.int32, q_t.shape, 0) < DIFF_QK_DIM
            zero = jnp.zeros_like(q_t)
            q_t = jnp.where(lo, q_t, zero) if sub == 0 else jnp.where(lo, zero, q_t)
        return q_t

    def scores(j, st, buf):
        start = pl.multiple_of(j * tq, tq)
        s_ref[buf] = jnp.dot(k_ref[pl.ds(start, tq), :], q_stream(st), preferred_element_type=F32)

    def softmax_pv(j, st, buf, masked):
        start = pl.multiple_of(j * tq, tq)
        v_t = v_ref[:, pl.ds(start, tq)]
        for c0 in range(0, tq, cw):
            cols = slice(c0, c0 + cw)
            s_t = s_ref[buf, :, cols]
            if masked:
                key = lax.broadcasted_iota(jnp.int32, s_t.shape, 0)
                qry = lax.broadcasted_iota(jnp.int32, s_t.shape, 1) + c0
                s_t = jnp.where(key <= qry, s_t, NEG_BIG)
            m_prev = m_ref[st, :, cols]
            m_next = jnp.maximum(m_prev, jnp.max(s_t, axis=0, keepdims=True))
            alpha = jnp.exp2(m_prev - m_next)
            m_ref[st, :, cols] = m_next
            p_t = jnp.exp2(s_t - m_next)
            l_ref[st, :, cols] = alpha * l_ref[st, :, cols] + jnp.sum(p_t, axis=0, keepdims=True)
            acc_ref[st, :, cols] = (acc_ref[st, :, cols] * alpha
                                    + jnp.dot(v_t, p_t.astype(BF16), preferred_element_type=F32))

    def run(units, following):
        assert len(units) % 2 == 0 or following is None
        seq = units + ([following] if following is not None else [])
        for pos, (j, st, masked) in enumerate(units):
            if pos + 1 < len(seq):
                scores(seq[pos + 1][0], seq[pos + 1][1], (pos + 1) % 2)
            softmax_pv(j, st, pos % 2, masked)

    def full_block(j):
        return [(j, st, False) for st in range(len(streams))]

    blocks_per_trip = max(1, UNITS_PER_TRIP // len(streams))
    assert blocks_per_trip & (blocks_per_trip - 1) == 0

    def trip(t, carry):
        base = t * blocks_per_trip
        units = [u for b in range(blocks_per_trip) for u in full_block(base + b)]
        run(units, (base + blocks_per_trip, 0, False))
        return carry

    tail = [(n + b, st, b == streams[st][0])
            for b in range(tiles) for st in range(len(streams)) if streams[st][0] >= b]

    scores(0, 0, 0)
    lax.fori_loop(0, lax.shift_right_logical(n, blocks_per_trip.bit_length() - 1), trip, 0)
    left = lax.bitwise_and(n, blocks_per_trip - 1)
    for r in range(0, blocks_per_trip, math.gcd(tiles, blocks_per_trip)):
        @pl.when(left == r)
        def _():
            run([u for b in range(r) for u in full_block(n - r + b)] + tail, None)

    def out_t(st):
        return acc_ref[st] * (1.0 / l_ref[st])

    if mode == "diff":
        lp = lam_ref[...]
        lam = (jnp.exp(jnp.sum(lp[0:1] * lp[1:2], axis=1, keepdims=True))
               - jnp.exp(jnp.sum(lp[2:3] * lp[3:4], axis=1, keepdims=True)) + lambda_init)
    for t in range(tiles):
        rows = slice(t * tq, (t + 1) * tq)
        if mode == "diff":
            o = (out_t(2 * t) - lam * out_t(2 * t + 1)).T
            o = o * lax.rsqrt(jnp.mean(o * o, axis=-1, keepdims=True) + EPS) * sub_ref[...]
            o_ref[rows, :] = (o * (1.0 - lambda_init)).astype(o_ref.dtype)
        else:
            o_ref[rows, :] = out_t(t).T.astype(o_ref.dtype)


def _attn_grid(n_heads, s, tq):
    n_tiles = s // tq
    tiles = math.gcd(n_tiles, TILES_PER_STEP)
    return tiles, (n_heads, n_tiles // tiles)


def _can_ride(weights, n_steps):
    return all(w.shape[1] % (n_steps * BF16_ROWS) == 0 for w in weights)


def _attention(q_t, k, v_t, extras, *, mode, tq, lambda_init=0.0, cast=(), cast_layer=0):
    n_heads, dk, s = q_t.shape
    dv = v_t.shape[1]
    tiles, grid = _attn_grid(n_heads, s, tq)
    n_streams = tiles * (2 if mode == "diff" else 1)
    in_specs = [pl.BlockSpec((None, dk, tiles * tq), lambda h, i: (h, 0, i)),
                pl.BlockSpec((s, dk), lambda h, i: (0, h)),
                pl.BlockSpec((None, dv, s), lambda h, i: (h, 0, 0))]
    if mode == "diff":
        lam, sub = extras
        in_specs += [pl.BlockSpec(lam.shape, lambda h, i: (0, 0)),
                     pl.BlockSpec(sub.shape, lambda h, i: (0, 0))]
    out_specs = [pl.BlockSpec((tiles * tq, dv), lambda h, i: (i, h))]
    out_shape = [jax.ShapeDtypeStruct((s, n_heads * dv), BF16)]
    n_steps = grid[0] * grid[1]
    assert _can_ride(cast, n_steps)
    for w in cast:
        rows = w.shape[1] // n_steps
        in_specs.append(pl.BlockSpec((None, rows, w.shape[2]), lambda h, i: (cast_layer, h * grid[1] + i, 0)))
        out_specs.append(pl.BlockSpec((rows, w.shape[2]), lambda h, i: (h * grid[1] + i, 0)))
        out_shape.append(jax.ShapeDtypeStruct(w.shape[1:], BF16))
    outs = pl.pallas_call(
        functools.partial(_attn_kernel, mode=mode, tq=tq, tiles=tiles, lambda_init=lambda_init,
                          n_cast=len(cast)),
        grid=grid,
        in_specs=in_specs,
        out_specs=out_specs,
        out_shape=out_shape,
        scratch_shapes=[pltpu.VMEM((n_streams, 1, tq), F32),
                        pltpu.VMEM((n_streams, 1, tq), F32),
                        pltpu.VMEM((n_streams, dv, tq), F32),
                        pltpu.VMEM((2, tq, tq), F32)],
        compiler_params=_params("arbitrary", "arbitrary"),
        name="attn_" + mode,
    )(q_t, k, v_t, *extras, *cast)
    return outs if cast else outs[0]


def _out_proj_kernel(x_ref, oa_ref, ob_ref, oc_ref, w_ref, o_ref):
    a0, a1 = DIFF_V_COLS, DIFF_V_COLS + FOX_COLS
    acc = x_ref[...]
    acc = acc + jnp.dot(oa_ref[...], w_ref[0:a0, :], preferred_element_type=F32)
    acc = acc + jnp.dot(ob_ref[...], w_ref[a0:a1, :], preferred_element_type=F32)
    acc = acc + jnp.dot(oc_ref[...], w_ref[a1:, :], preferred_element_type=F32)
    o_ref[...] = acc


def _out_proj(x, oa, ob, oc, w, layer, *, tm):
    s, d = x.shape
    row_blk = lambda n: pl.BlockSpec((tm, n), lambda i: (i, 0))
    return pl.pallas_call(
        _out_proj_kernel,
        grid=(s // tm,),
        in_specs=[row_blk(d), row_blk(oa.shape[1]), row_blk(ob.shape[1]), row_blk(oc.shape[1]),
                  pl.BlockSpec((None,) + w.shape[1:], lambda i: (layer, 0, 0))],
        out_specs=row_blk(d),
        out_shape=jax.ShapeDtypeStruct((s, d), F32),
        compiler_params=_params("arbitrary"),
        name="out_proj",
    )(x, oa, ob, oc, w)


def _ffn_kernel(x_ref, g_ref, wu_ref, wd_ref, o_ref, h_ref):
    @pl.when(pl.program_id(1) == 0)
    def _():
        x = x_ref[...]
        inv = lax.rsqrt(jnp.mean(x * x, axis=-1, keepdims=True) + EPS)
        h_ref[...] = (x * inv * g_ref[...]).astype(BF16)
        o_ref[...] = x

    u = jnp.dot(h_ref[...], wu_ref[...], preferred_element_type=F32)
    a = jnp.square(jnp.maximum(u, 0.0)).astype(BF16)
    o_ref[...] += jnp.dot(a, wd_ref[...], preferred_element_type=F32)


def _ffn(x, g, wu, wd, layer, *, tm, tf):
    s, d = x.shape
    f = wu.shape[2]
    return pl.pallas_call(
        _ffn_kernel,
        grid=(s // tm, f // tf),
        in_specs=[pl.BlockSpec((tm, d), lambda i, j: (i, 0)),
                  pl.BlockSpec((1, d), lambda i, j: (0, 0)),
                  pl.BlockSpec((None, d, tf), lambda i, j: (layer, 0, j)),
                  pl.BlockSpec((None, tf, d), lambda i, j: (layer, j, 0))],
        out_specs=pl.BlockSpec((tm, d), lambda i, j: (i, 0)),
        out_shape=jax.ShapeDtypeStruct((s, d), F32),
        scratch_shapes=[pltpu.VMEM((tm, d), BF16)],
        compiler_params=_params("arbitrary", "arbitrary"),
        name="ffn",
    )(x, g, wu, wd)


def _pack_w_q_up(w):
    w = w.reshape(MLA_Q_RANK, N_HEADS_MLA, MLA_QK_DIM)
    pad = jnp.zeros((MLA_Q_RANK, N_HEADS_MLA, MLA_PAD_DIM - MLA_QK_DIM), w.dtype)
    w = jnp.concatenate([w[:, :, MLA_ROPE_DIM:], w[:, :, :MLA_ROPE_DIM], pad], axis=2)
    return w.reshape(MLA_Q_RANK, N_HEADS_MLA * MLA_PAD_DIM).astype(BF16)


def _pack_w_kv_up(w):
    w = w.reshape(MLA_KV_RANK, N_HEADS_MLA, MLA_NOPE_DIM + MLA_V_DIM)
    k_nope = w[:, :, :MLA_NOPE_DIM].reshape(MLA_KV_RANK, N_HEADS_MLA * MLA_NOPE_DIM)
    v = w[:, :, MLA_NOPE_DIM:].reshape(MLA_KV_RANK, MLA_V_COLS)
    return jnp.concatenate([k_nope, v], axis=1).astype(BF16)


def _pack_vec(diff_q_norm, diff_k_norm, fox_q_norm, fox_k_norm, fox_forget_bias,
              mla_q_a_norm, mla_kv_a_norm, mla_q_norm, mla_k_norm):
    def nope_rope_pad(g):
        return jnp.concatenate([g[MLA_ROPE_DIM:], g[:MLA_ROPE_DIM],
                                jnp.zeros((MLA_PAD_DIM - MLA_QK_DIM,), g.dtype)])
    parts = [jnp.tile(diff_q_norm, 2), jnp.tile(diff_k_norm, 2), fox_q_norm, fox_k_norm,
             jnp.pad(fox_forget_bias, (0, LANES - N_HEADS_FOX)),
             mla_q_a_norm, mla_kv_a_norm, nope_rope_pad(mla_q_norm), nope_rope_pad(mla_k_norm)]
    return jnp.concatenate(parts).astype(F32).reshape(1, PV_LEN)


def _rope_tables(seq):
    def cos_sin(rot_dim):
        half = rot_dim // 2
        inv_freq = ROPE_THETA ** (-jnp.arange(half, dtype=F32) / half)
        ang = jnp.arange(seq, dtype=F32)[:, None] * inv_freq[None, :]
        return jnp.cos(ang), jnp.sin(ang)

    cos_p, sin_p = cos_sin(PARTIAL_ROT_DIM)
    cos_m, sin_m = cos_sin(MLA_ROPE_DIM)
    hp, hm = PARTIAL_ROT_DIM // 2, MLA_ROPE_DIM // 2
    ones = lambda n: jnp.ones((seq, n), F32)
    zeros = lambda n: jnp.zeros((seq, n), F32)
    rest = DIFF_QK_DIM - PARTIAL_ROT_DIM
    cd = jnp.tile(jnp.concatenate([cos_p, cos_p, ones(rest)], axis=1), (1, 2))
    sd_up = jnp.tile(jnp.concatenate([-sin_p, zeros(hp + rest)], axis=1), (1, 2))
    sd_dn = jnp.tile(jnp.concatenate([zeros(hp), sin_p, zeros(rest)], axis=1), (1, 2))
    pad = LANES - MLA_ROPE_DIM
    cm = jnp.concatenate([cos_m, cos_m, zeros(pad)], axis=1)
    sm_up = jnp.concatenate([-sin_m, zeros(hm + pad)], axis=1)
    sm_dn = jnp.concatenate([zeros(hm), sin_m, zeros(pad)], axis=1)
    return cd, sd_up, sd_dn, cm, sm_up, sm_dn


def _tile(seq, want):
    return min(seq, want)


def kernel(x, norm_mix, w_in, diff_q_norm, diff_k_norm, diff_lambda_q1, diff_lambda_k1,
           diff_lambda_q2, diff_lambda_k2, diff_subln, fox_q_norm, fox_k_norm, fox_forget_bias,
           mla_q_a_norm, mla_kv_a_norm, mla_w_q_up, mla_w_kv_up, mla_q_norm, mla_k_norm,
           w_out, norm_ffn, w_ff_up, w_ff_down):
    batch, seq, d = x.shape
    assert batch == 1 and d == D_MODEL and seq % LANES == 0
    tabs = _rope_tables(seq)
    xs = x.reshape(seq, d)
    tq = _tile(seq, 512)
    assert sum(IN_SECTIONS) == w_in.shape[-1]
    _, diff_grid = _attn_grid(N_HEADS_DIFF, seq, tq)
    late_weights = (w_ff_up, w_ff_down, w_out)
    ride = _can_ride(late_weights, diff_grid[0] * diff_grid[1])
    for l in range(DEPTH):
        proj = _in_proj(xs, norm_mix[l].reshape(1, d), w_in, l, tm=_tile(seq, 512))
        pvec = _pack_vec(diff_q_norm[l], diff_k_norm[l], fox_q_norm[l], fox_k_norm[l],
                         fox_forget_bias[l], mla_q_a_norm[l], mla_kv_a_norm[l],
                         mla_q_norm[l], mla_k_norm[l])
        (qd, kd, vd, qf, kf, vf, qm, km, vm) = _prep(
            proj, tabs, pvec, _pack_w_q_up(mla_w_q_up[l]), _pack_w_kv_up(mla_w_kv_up[l]),
            tm=_tile(seq, 256))
        lambda_init = 0.8 - 0.6 * math.exp(-0.3 * l)
        lam = jnp.stack([diff_lambda_q1[l], diff_lambda_k1[l], diff_lambda_q2[l], diff_lambda_k2[l]])
        diff_extras = (lam, diff_subln[l].reshape(1, DIFF_V_DIM))
        if ride:
            o_a, w_up_b, w_down_b, w_out_b = _attention(
                qd, kd, vd, diff_extras, mode="diff", tq=tq, lambda_init=lambda_init,
                cast=late_weights, cast_layer=l)
        else:
            o_a = _attention(qd, kd, vd, diff_extras, mode="diff", tq=tq, lambda_init=lambda_init)
            w_up_b, w_down_b, w_out_b = (w[l].astype(BF16) for w in late_weights)
        o_b = _attention(qf, kf, vf, (), mode="fox", tq=tq)
        o_c = _attention(qm, km, vm, (), mode="mla", tq=tq)
        xs = _out_proj(xs, o_a, o_b, o_c, w_out_b[None], 0, tm=_tile(seq, 512))
        xs = _ffn(xs, norm_ffn[l].reshape(1, d), w_up_b[None], w_down_b[None], 0,
                  tm=_tile(seq, 512), tf=1024)
    return xs.reshape(batch, seq, d)
```

```python
import functools
import math

import jax
import jax.numpy as jnp
from jax import lax
from jax.experimental import pallas as pl
from jax.experimental.pallas import tpu as pltpu

D_MODEL = 2048
DEPTH = 2
N_HEADS_DIFF = 4
DIFF_QK_DIM = 64
DIFF_V_DIM = 128
N_HEADS_FOX = 6
FOX_HEAD_DIM = 128
N_HEADS_MLA = 6
MLA_Q_RANK = 512
MLA_KV_RANK = 256
MLA_NOPE_DIM = 128
MLA_ROPE_DIM = 64
MLA_QK_DIM = MLA_ROPE_DIM + MLA_NOPE_DIM
MLA_V_DIM = 128
D_FF = 4 * D_MODEL
ROPE_THETA = 500000.0
PARTIAL_ROT_DIM = DIFF_QK_DIM // 4
EPS = 1e-6

DIFF_Q_COLS = N_HEADS_DIFF * 2 * DIFF_QK_DIM
DIFF_V_COLS = N_HEADS_DIFF * DIFF_V_DIM
FOX_COLS = N_HEADS_FOX * FOX_HEAD_DIM
MLA_V_COLS = N_HEADS_MLA * MLA_V_DIM
IN_SECTIONS = (DIFF_Q_COLS, DIFF_Q_COLS, DIFF_V_COLS, FOX_COLS, FOX_COLS, FOX_COLS,
               N_HEADS_FOX, MLA_Q_RANK, MLA_KV_RANK + MLA_ROPE_DIM)

LANES = 128
FOX_PAD_DIM = 2 * LANES
DECAY_PIECES = 3
BF16_ROWS = 16
MXU_COLS = 256
UNITS_PER_TRIP = 8
TILES_PER_STEP = 4
MLA_PAD_DIM = 2 * LANES
VMEM_LIMIT_BYTES = 56 * 1024 * 1024

OFF_DQ = 0
OFF_DK = OFF_DQ + DIFF_Q_COLS
OFF_DV = OFF_DK + DIFF_Q_COLS
OFF_FQ = OFF_DV + DIFF_V_COLS
OFF_FK = OFF_FQ + FOX_COLS
OFF_FV = OFF_FK + FOX_COLS
MAIN_COLS = OFF_FV + FOX_COLS
TAIL_SRC_COLS = N_HEADS_FOX + MLA_Q_RANK + MLA_KV_RANK + MLA_ROPE_DIM
TAIL_MQ = 0
TAIL_CKV = TAIL_MQ + MLA_Q_RANK
TAIL_KR = TAIL_CKV + MLA_KV_RANK
TAIL_FF = TAIL_KR + LANES
TAIL_COLS = TAIL_FF + LANES
IN_TILE = MAIN_COLS // 3
PROJ_COLS = MAIN_COLS + IN_TILE
assert TAIL_COLS <= IN_TILE and IN_TILE % LANES == 0

PV_G_DQ = 0
PV_G_DK = PV_G_DQ + LANES
PV_G_FQ = PV_G_DK + LANES
PV_G_FK = PV_G_FQ + LANES
PV_F_BIAS = PV_G_FK + LANES
PV_G_QA = PV_F_BIAS + LANES
PV_G_KVA = PV_G_QA + MLA_Q_RANK
PV_G_QM = PV_G_KVA + MLA_KV_RANK
PV_G_KM = PV_G_QM + MLA_PAD_DIM
PV_LEN = PV_G_KM + MLA_PAD_DIM

NEG_BIG = -1e30
LOG2E = math.log2(math.e)

BF16 = jnp.bfloat16
F32 = jnp.float32


def _params(*semantics):
    return pltpu.CompilerParams(dimension_semantics=semantics,
                                vmem_limit_bytes=VMEM_LIMIT_BYTES)


def _in_proj_kernel(x_ref, g_ref, w_ref, o_ref, wb_ref, h_ref):
    first_row_tile = pl.program_id(0) == 0
    j = pl.program_id(1)
    n_main = MAIN_COLS // IN_TILE

    @pl.when(jnp.logical_and(first_row_tile, j < n_main))
    def _():
        wb_ref[j] = w_ref[...].astype(BF16)

    @pl.when(jnp.logical_and(first_row_tile, j == n_main))
    def _():
        width = TAIL_KR + LANES
        w = w_ref[:, 0:width]
        r = pltpu.roll(w, width - N_HEADS_FOX, 1)
        lane = lax.broadcasted_iota(jnp.int32, (w.shape[0], LANES), 1)
        wb_ref[n_main, :, TAIL_MQ:TAIL_KR] = r[:, TAIL_MQ:TAIL_KR].astype(BF16)
        wb_ref[n_main, :, TAIL_KR:TAIL_FF] = jnp.where(lane < MLA_ROPE_DIM, r[:, TAIL_KR:TAIL_FF], 0.0).astype(BF16)
        wb_ref[n_main, :, TAIL_FF:TAIL_COLS] = jnp.where(lane < N_HEADS_FOX, w[:, 0:LANES], 0.0).astype(BF16)
        wb_ref[n_main, :, TAIL_COLS:] = jnp.zeros((w.shape[0], IN_TILE - TAIL_COLS), BF16)

    @pl.when(j == 0)
    def _():
        x = x_ref[...]
        inv = lax.rsqrt(jnp.mean(x * x, axis=-1, keepdims=True) + EPS)
        h_ref[...] = (x * inv * g_ref[...]).astype(BF16)

    o_ref[...] = jnp.dot(h_ref[...], wb_ref[j], preferred_element_type=F32)


def _in_proj(x, g, w_in, layer, *, tm):
    s, d = x.shape
    assert w_in.shape[-1] == MAIN_COLS + TAIL_SRC_COLS
    n_col_tiles = PROJ_COLS // IN_TILE
    w_index = lambda i, j: (layer, 0, jnp.where(i == 0, j, n_col_tiles - 1))
    return pl.pallas_call(
        _in_proj_kernel,
        grid=(s // tm, n_col_tiles),
        in_specs=[pl.BlockSpec((tm, d), lambda i, j: (i, 0)),
                  pl.BlockSpec((1, d), lambda i, j: (0, 0)),
                  pl.BlockSpec((None, d, IN_TILE), w_index, pipeline_mode=pl.Buffered(1))],
        out_specs=pl.BlockSpec((tm, IN_TILE), lambda i, j: (i, j)),
        out_shape=jax.ShapeDtypeStruct((s, PROJ_COLS), F32),
        scratch_shapes=[pltpu.VMEM((n_col_tiles, d, IN_TILE), BF16),
                        pltpu.VMEM((tm, d), BF16)],
        compiler_params=_params("arbitrary", "arbitrary"),
        name="in_proj",
    )(x, g, w_in)


def _rms(x, n_valid):
    ss = jnp.sum(x * x, axis=-1, keepdims=True)
    return x * lax.rsqrt(ss * (1.0 / n_valid) + EPS)


def _rms_two_halves(x):
    lo = lax.broadcasted_iota(jnp.int32, x.shape, 1) < DIFF_QK_DIM
    x2 = x * x
    s_lo = jnp.sum(jnp.where(lo, x2, 0.0), axis=-1, keepdims=True)
    s_hi = jnp.sum(jnp.where(lo, 0.0, x2), axis=-1, keepdims=True)
    return x * lax.rsqrt(jnp.where(lo, s_lo, s_hi) * (1.0 / DIFF_QK_DIM) + EPS)


def _shift_pair_bf16(half):
    src = lax.broadcasted_iota(jnp.int32, (LANES, 2 * LANES), 0)
    dst = lax.broadcasted_iota(jnp.int32, (LANES, 2 * LANES), 1)
    want = jnp.where(dst < LANES, dst + half, dst - LANES - half)
    return jnp.where(src == jnp.bitwise_and(want, LANES - 1), 1.0, 0.0).astype(BF16)


def _rope(x, c, s_up, s_dn, shift_pair):
    shifted = jnp.dot(x.astype(BF16), shift_pair, preferred_element_type=F32)
    return x * c + shifted[:, :LANES] * s_up + shifted[:, LANES:] * s_dn


def _identity_bf16(n):
    row = lax.broadcasted_iota(jnp.int32, (n, n), 0)
    col = lax.broadcasted_iota(jnp.int32, (n, n), 1)
    return jnp.where(row == col, 1.0, 0.0).astype(BF16)


def _t_bf16(x, eye):
    return lax.dot_general(eye, x.astype(BF16), (((1,), (1,)), ((), ())),
                           preferred_element_type=F32).astype(BF16)


def _prep_kernel(proj_ref, tail_ref, cd_ref, sd_up_ref, sd_dn_ref, cm_ref, sm_up_ref, sm_dn_ref,
                 pv_ref, wq_ref, wkv_ref,
                 qd_ref, kd_ref, vd_ref, qf_ref, kf_ref, vf_ref,
                 qm_ref, km_ref, vm_ref, carry_ref, *, tm):
    @pl.when(pl.program_id(0) == 0)
    def _():
        carry_ref[...] = jnp.zeros_like(carry_ref)

    def pv(off, n):
        return pv_ref[:, off:off + n]

    def lanes(ref, off, n=LANES):
        return ref[:, off:off + n]

    cd, sd_up, sd_dn = cd_ref[...], sd_up_ref[...], sd_dn_ref[...]
    cm, sm_up, sm_dn = cm_ref[...], sm_up_ref[...], sm_dn_ref[...]
    half_d = _shift_pair_bf16(PARTIAL_ROT_DIM // 2)
    half_m = _shift_pair_bf16(MLA_ROPE_DIM // 2)
    eye = _identity_bf16(LANES)

    g_dq, g_dk = pv(PV_G_DQ, LANES), pv(PV_G_DK, LANES)
    for b in range(N_HEADS_DIFF):
        q = _rope(_rms_two_halves(lanes(proj_ref, OFF_DQ + b * LANES)) * g_dq, cd, sd_up, sd_dn, half_d)
        qd_ref[b] = _t_bf16(q * (DIFF_QK_DIM ** -0.5 * LOG2E), eye)
        k = _rope(_rms_two_halves(lanes(proj_ref, OFF_DK + b * LANES)) * g_dk, cd, sd_up, sd_dn, half_d)
        kd_ref[:, b * LANES:(b + 1) * LANES] = k.astype(BF16)
        vd_ref[b] = _t_bf16(lanes(proj_ref, OFF_DV + b * LANES), eye)

    z = lanes(tail_ref, TAIL_FF) + pv(PV_F_BIAS, LANES)
    c = jnp.minimum(z, 0.0) - jnp.log1p(jnp.exp(-jnp.abs(z)))
    row = lax.broadcasted_iota(jnp.int32, c.shape, 0)
    shift = 1
    while shift < tm:
        c = c + jnp.where(row >= shift, pltpu.roll(c, shift, 0), 0.0)
        shift *= 2
    c = c + carry_ref[...]
    carry_ref[...] = c[tm - 1:tm, :]
    c2 = c * LOG2E

    g_fq, g_fk = pv(PV_G_FQ, LANES), pv(PV_G_FK, LANES)
    lane = lax.broadcasted_iota(jnp.int32, (tm, LANES), 1)
    ones_rows = jnp.where(lax.broadcasted_iota(jnp.int32, (LANES, tm), 0) < DECAY_PIECES, 1.0, 0.0)
    for h in range(N_HEADS_FOX):
        base = h * FOX_PAD_DIM
        q = _rms(lanes(proj_ref, OFF_FQ + h * LANES), FOX_HEAD_DIM) * g_fq
        qf_ref[h, 0:LANES, :] = _t_bf16(q * (FOX_HEAD_DIM ** -0.5 * LOG2E), eye)
        qf_ref[h, LANES:, :] = ones_rows.astype(BF16)
        k = _rms(lanes(proj_ref, OFF_FK + h * LANES), FOX_HEAD_DIM) * g_fk
        kf_ref[:, base:base + LANES] = k.astype(BF16)
        rest = -jnp.broadcast_to(c2[:, h:h + 1], (tm, LANES))
        decay = jnp.zeros((tm, LANES), F32)
        for piece in range(DECAY_PIECES):
            part = rest.astype(BF16).astype(F32)
            decay = jnp.where(lane == piece, part, decay)
            rest = rest - part
        kf_ref[:, base + LANES:base + FOX_PAD_DIM] = decay.astype(BF16)
        vf_ref[h] = _t_bf16(lanes(proj_ref, OFF_FV + h * LANES), eye)

    q_lat = (_rms(lanes(tail_ref, TAIL_MQ, MLA_Q_RANK), MLA_Q_RANK) * pv(PV_G_QA, MLA_Q_RANK)).astype(BF16)
    q_up = jnp.dot(q_lat, wq_ref[...], preferred_element_type=F32)
    c_kv = (_rms(lanes(tail_ref, TAIL_CKV, MLA_KV_RANK), MLA_KV_RANK) * pv(PV_G_KVA, MLA_KV_RANK)).astype(BF16)
    kv_up = jnp.dot(c_kv, wkv_ref[...], preferred_element_type=F32)
    g_qn, g_qr = pv(PV_G_QM, LANES), pv(PV_G_QM + LANES, LANES)
    g_kn, g_kr = pv(PV_G_KM, LANES), pv(PV_G_KM + LANES, LANES)
    k_rope = _rope(_rms(lanes(tail_ref, TAIL_KR), MLA_ROPE_DIM) * g_kr, cm, sm_up, sm_dn, half_m).astype(BF16)
    mla_scale = MLA_QK_DIM ** -0.5 * LOG2E
    v_off = N_HEADS_MLA * MLA_NOPE_DIM
    for h in range(N_HEADS_MLA):
        base = h * MLA_PAD_DIM
        q_nope = _rms(q_up[:, base:base + LANES], MLA_NOPE_DIM) * g_qn
        q_rope = _rope(_rms(q_up[:, base + LANES:base + 2 * LANES], MLA_ROPE_DIM) * g_qr,
                       cm, sm_up, sm_dn, half_m)
        qm_ref[h, 0:LANES, :] = _t_bf16(q_nope * mla_scale, eye)
        qm_ref[h, LANES:2 * LANES, :] = _t_bf16(q_rope * mla_scale, eye)
        k_nope = _rms(kv_up[:, h * LANES:(h + 1) * LANES], MLA_NOPE_DIM) * g_kn
        km_ref[:, base:base + LANES] = k_nope.astype(BF16)
        km_ref[:, base + LANES:base + 2 * LANES] = k_rope
        vm_ref[h] = _t_bf16(kv_up[:, v_off + h * LANES:v_off + (h + 1) * LANES], eye)


def _prep(proj, tabs, pvec, wq, wkv, *, tm):
    s = proj.shape[0]
    tail_blk = pl.BlockSpec((tm, IN_TILE), lambda i: (i, MAIN_COLS // IN_TILE))
    row_blk = lambda n: pl.BlockSpec((tm, n), lambda i: (i, 0))
    full = lambda a: pl.BlockSpec(a.shape, lambda i: (0, 0))

    def head_t(n_heads, dim):
        return (jax.ShapeDtypeStruct((n_heads, dim, s), BF16),
                pl.BlockSpec((n_heads, dim, tm), lambda i: (0, 0, i)))

    def rows(n, dtype=BF16):
        return jax.ShapeDtypeStruct((s, n), dtype), row_blk(n)

    outs = [head_t(N_HEADS_DIFF, LANES), rows(DIFF_Q_COLS), head_t(N_HEADS_DIFF, DIFF_V_DIM),
            head_t(N_HEADS_FOX, FOX_PAD_DIM), rows(N_HEADS_FOX * FOX_PAD_DIM),
            head_t(N_HEADS_FOX, FOX_HEAD_DIM),
            head_t(N_HEADS_MLA, MLA_PAD_DIM), rows(N_HEADS_MLA * MLA_PAD_DIM),
            head_t(N_HEADS_MLA, MLA_V_DIM)]
    return pl.pallas_call(
        functools.partial(_prep_kernel, tm=tm),
        grid=(s // tm,),
        in_specs=[row_blk(MAIN_COLS), tail_blk] + [row_blk(LANES)] * len(tabs)
                 + [full(pvec), full(wq), full(wkv)],
        out_specs=[o[1] for o in outs],
        out_shape=[o[0] for o in outs],
        scratch_shapes=[pltpu.VMEM((1, LANES), F32)],
        compiler_params=_params("arbitrary"),
        name="prep",
    )(proj, proj, *tabs, pvec, wq, wkv)


def _attn_kernel(*refs, mode, tq, tiles, lambda_init, n_cast):
    n_in = 5 if mode == "diff" else 3
    q_ref, k_ref, v_ref = refs[:3]
    if mode == "diff":
        lam_ref, sub_ref = refs[3:5]
    cast_src = refs[n_in:n_in + n_cast]
    o_ref = refs[n_in + n_cast]
    cast_dst = refs[n_in + n_cast + 1:n_in + 2 * n_cast + 1]
    m_ref, l_ref, acc_ref, s_ref = refs[n_in + 2 * n_cast + 1:]
    for src, dst in zip(cast_src, cast_dst):
        dst[...] = src[...].astype(dst.dtype)
    n = pl.program_id(1) * tiles
    dv = o_ref.shape[1]
    cw = min(tq, MXU_COLS)
    n_sub = 2 if mode == "diff" else 1
    streams = [(t, sub) for t in range(tiles) for sub in range(n_sub)]

    m_ref[...] = jnp.full(m_ref.shape, NEG_BIG, F32)
    l_ref[...] = jnp.zeros(l_ref.shape, F32)
    acc_ref[...] = jnp.zeros(acc_ref.shape, F32)

    def q_stream(st):
        t, sub = streams[st]
        q_t = q_ref[:, t * tq:(t + 1) * tq]
        if mode == "diff":
            lo = lax.broadcasted_iota(jnp.int32, q_t.shape, 0) < DIFF_QK_DIM
            zero = jnp.zeros_like(q_t)
            q_t = jnp.where(lo, q_t, zero) if sub == 0 else jnp.where(lo, zero, q_t)
        return q_t

    def scores(j, st, buf):
        start = pl.multiple_of(j * tq, tq)
        s_ref[buf] = jnp.dot(k_ref[pl.ds(start, tq), :], q_stream(st), preferred_element_type=F32)

    def softmax_pv(j, st, buf, masked):
        start = pl.multiple_of(j * tq, tq)
        v_t = v_ref[:, pl.ds(start, tq)]
        for c0 in range(0, tq, cw):
            cols = slice(c0, c0 + cw)
            s_t = s_ref[buf, :, cols]
            if masked:
                key = lax.broadcasted_iota(jnp.int32, s_t.shape, 0)
                qry = lax.broadcasted_iota(jnp.int32, s_t.shape, 1) + c0
                s_t = jnp.where(key <= qry, s_t, NEG_BIG)
            m_prev = m_ref[st, :, cols]
            m_next = jnp.maximum(m_prev, jnp.max(s_t, axis=0, keepdims=True))
            alpha = jnp.exp2(m_prev - m_next)
            m_ref[st, :, cols] = m_next
            p_t = jnp.exp2(s_t - m_next)
            l_ref[st, :, cols] = alpha * l_ref[st, :, cols] + jnp.sum(p_t, axis=0, keepdims=True)
            acc_ref[st, :, cols] = (acc_ref[st, :, cols] * alpha
                                    + jnp.dot(v_t, p_t.astype(BF16), preferred_element_type=F32))

    def run(units, following):
        assert len(units) % 2 == 0 or following is None
        seq = units + ([following] if following is not None else [])
        for pos, (j, st, masked) in enumerate(units):
            if pos + 1 < len(seq):
                scores(seq[pos + 1][0], seq[pos + 1][1], (pos + 1) % 2)
            softmax_pv(j, st, pos % 2, masked)

    def full_block(j):
        return [(j, st, False) for st in range(len(streams))]

    blocks_per_trip = max(1, UNITS_PER_TRIP // len(streams))
    assert blocks_per_trip & (blocks_per_trip - 1) == 0

    def trip(t, carry):
        base = t * blocks_per_trip
        units = [u for b in range(blocks_per_trip) for u in full_block(base + b)]
        run(units, (base + blocks_per_trip, 0, False))
        return carry

    tail = [(n + b, st, b == streams[st][0])
            for b in range(tiles) for st in range(len(streams)) if streams[st][0] >= b]

    scores(0, 0, 0)
    lax.fori_loop(0, lax.shift_right_logical(n, blocks_per_trip.bit_length() - 1), trip, 0)
    left = lax.bitwise_and(n, blocks_per_trip - 1)
    for r in range(0, blocks_per_trip, math.gcd(tiles, blocks_per_trip)):
        @pl.when(left == r)
        def _():
            run([u for b in range(r) for u in full_block(n - r + b)] + tail, None)

    def out_t(st):
        return acc_ref[st] * (1.0 / l_ref[st])

    if mode == "diff":
        lp = lam_ref[...]
        lam = (jnp.exp(jnp.sum(lp[0:1] * lp[1:2], axis=1, keepdims=True))
               - jnp.exp(jnp.sum(lp[2:3] * lp[3:4], axis=1, keepdims=True)) + lambda_init)
    for t in range(tiles):
        rows = slice(t * tq, (t + 1) * tq)
        if mode == "diff":
            o = (out_t(2 * t) - lam * out_t(2 * t + 1)).T
            o = o * lax.rsqrt(jnp.mean(o * o, axis=-1, keepdims=True) + EPS) * sub_ref[...]
            o_ref[rows, :] = (o * (1.0 - lambda_init)).astype(o_ref.dtype)
        else:
            o_ref[rows, :] = out_t(t).T.astype(o_ref.dtype)


def _attn_grid(n_heads, s, tq):
    n_tiles = s // tq
    tiles = math.gcd(n_tiles, TILES_PER_STEP)
    return tiles, (n_heads, n_tiles // tiles)


def _can_ride(weights, n_steps):
    return all(w.shape[1] % (n_steps * BF16_ROWS) == 0 for w in weights)


def _attention(q_t, k, v_t, extras, *, mode, tq, lambda_init=0.0, cast=(), cast_layer=0):
    n_heads, dk, s = q_t.shape
    dv = v_t.shape[1]
    tiles, grid = _attn_grid(n_heads, s, tq)
    n_streams = tiles * (2 if mode == "diff" else 1)
    in_specs = [pl.BlockSpec((None, dk, tiles * tq), lambda h, i: (h, 0, i)),
                pl.BlockSpec((s, dk), lambda h, i: (0, h)),
                pl.BlockSpec((None, dv, s), lambda h, i: (h, 0, 0))]
    if mode == "diff":
        lam, sub = extras
        in_specs += [pl.BlockSpec(lam.shape, lambda h, i: (0, 0)),
                     pl.BlockSpec(sub.shape, lambda h, i: (0, 0))]
    out_specs = [pl.BlockSpec((tiles * tq, dv), lambda h, i: (i, h))]
    out_shape = [jax.ShapeDtypeStruct((s, n_heads * dv), BF16)]
    n_steps = grid[0] * grid[1]
    assert _can_ride(cast, n_steps)
    for w in cast:
        rows = w.shape[1] // n_steps
        in_specs.append(pl.BlockSpec((None, rows, w.shape[2]), lambda h, i: (cast_layer, h * grid[1] + i, 0)))
        out_specs.append(pl.BlockSpec((rows, w.shape[2]), lambda h, i: (h * grid[1] + i, 0)))
        out_shape.append(jax.ShapeDtypeStruct(w.shape[1:], BF16))
    outs = pl.pallas_call(
        functools.partial(_attn_kernel, mode=mode, tq=tq, tiles=tiles, lambda_init=lambda_init,
                          n_cast=len(cast)),
        grid=grid,
        in_specs=in_specs,
        out_specs=out_specs,
        out_shape=out_shape,
        scratch_shapes=[pltpu.VMEM((n_streams, 1, tq), F32),
                        pltpu.VMEM((n_streams, 1, tq), F32),
                        pltpu.VMEM((n_streams, dv, tq), F32),
                        pltpu.VMEM((2, tq, tq), F32)],
        compiler_params=_params("arbitrary", "arbitrary"),
        name="attn_" + mode,
    )(q_t, k, v_t, *extras, *cast)
    return outs if cast else outs[0]


def _out_proj_kernel(x_ref, oa_ref, ob_ref, oc_ref, w_ref, o_ref):
    a0, a1 = DIFF_V_COLS, DIFF_V_COLS + FOX_COLS
    acc = x_ref[...]
    acc = acc + jnp.dot(oa_ref[...], w_ref[0:a0, :], preferred_element_type=F32)
    acc = acc + jnp.dot(ob_ref[...], w_ref[a0:a1, :], preferred_element_type=F32)
    acc = acc + jnp.dot(oc_ref[...], w_ref[a1:, :], preferred_element_type=F32)
    o_ref[...] = acc


def _out_proj(x, oa, ob, oc, w, layer, *, tm):
    s, d = x.shape
    row_blk = lambda n: pl.BlockSpec((tm, n), lambda i: (i, 0))
    return pl.pallas_call(
        _out_proj_kernel,
        grid=(s // tm,),
        in_specs=[row_blk(d), row_blk(oa.shape[1]), row_blk(ob.shape[1]), row_blk(oc.shape[1]),
                  pl.BlockSpec((None,) + w.shape[1:], lambda i: (layer, 0, 0))],
        out_specs=row_blk(d),
        out_shape=jax.ShapeDtypeStruct((s, d), F32),
        compiler_params=_params("arbitrary"),
        name="out_proj",
    )(x, oa, ob, oc, w)


def _ffn_kernel(x_ref, g_ref, wu_ref, wd_ref, o_ref, h_ref):
    @pl.when(pl.program_id(1) == 0)
    def _():
        x = x_ref[...]
        inv = lax.rsqrt(jnp.mean(x * x, axis=-1, keepdims=True) + EPS)
        h_ref[...] = (x * inv * g_ref[...]).astype(BF16)
        o_ref[...] = x

    u = jnp.dot(h_ref[...], wu_ref[...], preferred_element_type=F32)
    a = jnp.square(jnp.maximum(u, 0.0)).astype(BF16)
    o_ref[...] += jnp.dot(a, wd_ref[...], preferred_element_type=F32)


def _ffn(x, g, wu, wd, layer, *, tm, tf):
    s, d = x.shape
    f = wu.shape[2]
    return pl.pallas_call(
        _ffn_kernel,
        grid=(s // tm, f // tf),
        in_specs=[pl.BlockSpec((tm, d), lambda i, j: (i, 0)),
                  pl.BlockSpec((1, d), lambda i, j: (0, 0)),
                  pl.BlockSpec((None, d, tf), lambda i, j: (layer, 0, j)),
                  pl.BlockSpec((None, tf, d), lambda i, j: (layer, j, 0))],
        out_specs=pl.BlockSpec((tm, d), lambda i, j: (i, 0)),
        out_shape=jax.ShapeDtypeStruct((s, d), F32),
        scratch_shapes=[pltpu.VMEM((tm, d), BF16)],
        compiler_params=_params("arbitrary", "arbitrary"),
        name="ffn",
    )(x, g, wu, wd)


def _pack_w_q_up(w):
    w = w.reshape(MLA_Q_RANK, N_HEADS_MLA, MLA_QK_DIM)
    pad = jnp.zeros((MLA_Q_RANK, N_HEADS_MLA, MLA_PAD_DIM - MLA_QK_DIM), w.dtype)
    w = jnp.concatenate([w[:, :, MLA_ROPE_DIM:], w[:, :, :MLA_ROPE_DIM], pad], axis=2)
    return w.reshape(MLA_Q_RANK, N_HEADS_MLA * MLA_PAD_DIM).astype(BF16)


def _pack_w_kv_up(w):
    w = w.reshape(MLA_KV_RANK, N_HEADS_MLA, MLA_NOPE_DIM + MLA_V_DIM)
    k_nope = w[:, :, :MLA_NOPE_DIM].reshape(MLA_KV_RANK, N_HEADS_MLA * MLA_NOPE_DIM)
    v = w[:, :, MLA_NOPE_DIM:].reshape(MLA_KV_RANK, MLA_V_COLS)
    return jnp.concatenate([k_nope, v], axis=1).astype(BF16)


def _pack_vec(diff_q_norm, diff_k_norm, fox_q_norm, fox_k_norm, fox_forget_bias,
              mla_q_a_norm, mla_kv_a_norm, mla_q_norm, mla_k_norm):
    def nope_rope_pad(g):
        return jnp.concatenate([g[MLA_ROPE_DIM:], g[:MLA_ROPE_DIM],
                                jnp.zeros((MLA_PAD_DIM - MLA_QK_DIM,), g.dtype)])
    parts = [jnp.tile(diff_q_norm, 2), jnp.tile(diff_k_norm, 2), fox_q_norm, fox_k_norm,
             jnp.pad(fox_forget_bias, (0, LANES - N_HEADS_FOX)),
             mla_q_a_norm, mla_kv_a_norm, nope_rope_pad(mla_q_norm), nope_rope_pad(mla_k_norm)]
    return jnp.concatenate(parts).astype(F32).reshape(1, PV_LEN)


def _rope_tables(seq):
    def cos_sin(rot_dim):
        half = rot_dim // 2
        inv_freq = ROPE_THETA ** (-jnp.arange(half, dtype=F32) / half)
        ang = jnp.arange(seq, dtype=F32)[:, None] * inv_freq[None, :]
        return jnp.cos(ang), jnp.sin(ang)

    cos_p, sin_p = cos_sin(PARTIAL_ROT_DIM)
    cos_m, sin_m = cos_sin(MLA_ROPE_DIM)
    hp, hm = PARTIAL_ROT_DIM // 2, MLA_ROPE_DIM // 2
    ones = lambda n: jnp.ones((seq, n), F32)
    zeros = lambda n: jnp.zeros((seq, n), F32)
    rest = DIFF_QK_DIM - PARTIAL_ROT_DIM
    cd = jnp.tile(jnp.concatenate([cos_p, cos_p, ones(rest)], axis=1), (1, 2))
    sd_up = jnp.tile(jnp.concatenate([-sin_p, zeros(hp + rest)], axis=1), (1, 2))
    sd_dn = jnp.tile(jnp.concatenate([zeros(hp), sin_p, zeros(rest)], axis=1), (1, 2))
    pad = LANES - MLA_ROPE_DIM
    cm = jnp.concatenate([cos_m, cos_m, zeros(pad)], axis=1)
    sm_up = jnp.concatenate([-sin_m, zeros(hm + pad)], axis=1)
    sm_dn = jnp.concatenate([zeros(hm), sin_m, zeros(pad)], axis=1)
    return cd, sd_up, sd_dn, cm, sm_up, sm_dn


def _tile(seq, want):
    return min(seq, want)


def kernel(x, norm_mix, w_in, diff_q_norm, diff_k_norm, diff_lambda_q1, diff_lambda_k1,
           diff_lambda_q2, diff_lambda_k2, diff_subln, fox_q_norm, fox_k_norm, fox_forget_bias,
           mla_q_a_norm, mla_kv_a_norm, mla_w_q_up, mla_w_kv_up, mla_q_norm, mla_k_norm,
           w_out, norm_ffn, w_ff_up, w_ff_down):
    batch, seq, d = x.shape
    assert batch == 1 and d == D_MODEL and seq % LANES == 0
    tabs = _rope_tables(seq)
    xs = x.reshape(seq, d)
    tq = _tile(seq, 512)
    assert sum(IN_SECTIONS) == w_in.shape[-1]
    _, diff_grid = _attn_grid(N_HEADS_DIFF, seq, tq)
    late_weights = (w_ff_up, w_ff_down, w_out)
    ride = _can_ride(late_weights, diff_grid[0] * diff_grid[1])
    for l in range(DEPTH):
        proj = _in_proj(xs, norm_mix[l].reshape(1, d), w_in, l, tm=_tile(seq, 512))
        pvec = _pack_vec(diff_q_norm[l], diff_k_norm[l], fox_q_norm[l], fox_k_norm[l],
                         fox_forget_bias[l], mla_q_a_norm[l], mla_kv_a_norm[l],
                         mla_q_norm[l], mla_k_norm[l])
        (qd, kd, vd, qf, kf, vf, qm, km, vm) = _prep(
            proj, tabs, pvec, _pack_w_q_up(mla_w_q_up[l]), _pack_w_kv_up(mla_w_kv_up[l]),
            tm=_tile(seq, 256))
        lambda_init = 0.8 - 0.6 * math.exp(-0.3 * l)
        lam = jnp.stack([diff_lambda_q1[l], diff_lambda_k1[l], diff_lambda_q2[l], diff_lambda_k2[l]])
        diff_extras = (lam, diff_subln[l].reshape(1, DIFF_V_DIM))
        if ride:
            o_a, w_up_b, w_down_b, w_out_b = _attention(
                qd, kd, vd, diff_extras, mode="diff", tq=tq, lambda_init=lambda_init,
                cast=late_weights, cast_layer=l)
        else:
            o_a = _attention(qd, kd, vd, diff_extras, mode="diff", tq=tq, lambda_init=lambda_init)
            w_up_b, w_down_b, w_out_b = (w[l].astype(BF16) for w in late_weights)
        o_b = _attention(qf, kf, vf, (), mode="fox", tq=tq)
        o_c = _attention(qm, km, vm, (), mode="mla", tq=tq)
        xs = _out_proj(xs, o_a, o_b, o_c, w_out_b[None], 0, tm=_tile(seq, 512))
        xs = _ffn(xs, norm_ffn[l].reshape(1, d), w_up_b[None], w_down_b[None], 0,
                  tm=_tile(seq, 512), tf=1024)
    return xs.reshape(batch, seq, d)
```

```python
import functools
import math

import jax
import jax.numpy as jnp
from jax import lax
from jax.experimental import pallas as pl
from jax.experimental.pallas import tpu as pltpu

D_MODEL = 2048
DEPTH = 2
N_HEADS_DIFF = 4
DIFF_QK_DIM = 64
DIFF_V_DIM = 128
N_HEADS_FOX = 6
FOX_HEAD_DIM = 128
N_HEADS_MLA = 6
MLA_Q_RANK = 512
MLA_KV_RANK = 256
MLA_NOPE_DIM = 128
MLA_ROPE_DIM = 64
MLA_QK_DIM = MLA_ROPE_DIM + MLA_NOPE_DIM
MLA_V_DIM = 128
D_FF = 4 * D_MODEL
ROPE_THETA = 500000.0
PARTIAL_ROT_DIM = DIFF_QK_DIM // 4
EPS = 1e-6

DIFF_Q_COLS = N_HEADS_DIFF * 2 * DIFF_QK_DIM
DIFF_V_COLS = N_HEADS_DIFF * DIFF_V_DIM
FOX_COLS = N_HEADS_FOX * FOX_HEAD_DIM
MLA_V_COLS = N_HEADS_MLA * MLA_V_DIM
IN_SECTIONS = (DIFF_Q_COLS, DIFF_Q_COLS, DIFF_V_COLS, FOX_COLS, FOX_COLS, FOX_COLS,
               N_HEADS_FOX, MLA_Q_RANK, MLA_KV_RANK + MLA_ROPE_DIM)

LANES = 128
FOX_PAD_DIM = 2 * LANES
DECAY_PIECES = 3
BF16_ROWS = 16
MXU_COLS = 256
UNITS_PER_TRIP = 8
IN_ROW_CHUNKS = 2
TILES_PER_STEP = 4
MLA_PAD_DIM = 2 * LANES
VMEM_LIMIT_BYTES = 56 * 1024 * 1024

OFF_DQ = 0
OFF_DK = OFF_DQ + DIFF_Q_COLS
OFF_DV = OFF_DK + DIFF_Q_COLS
OFF_FQ = OFF_DV + DIFF_V_COLS
OFF_FK = OFF_FQ + FOX_COLS
OFF_FV = OFF_FK + FOX_COLS
MAIN_COLS = OFF_FV + FOX_COLS
TAIL_SRC_COLS = N_HEADS_FOX + MLA_Q_RANK + MLA_KV_RANK + MLA_ROPE_DIM
FF_LANE0 = -TAIL_SRC_COLS % 8
TAIL_MQ = 0
TAIL_CKV = TAIL_MQ + MLA_Q_RANK
TAIL_KR = TAIL_CKV + MLA_KV_RANK
TAIL_FF = TAIL_KR + LANES
TAIL_COLS = TAIL_FF + LANES
IN_TILE = MAIN_COLS // 3
PROJ_COLS = MAIN_COLS + IN_TILE
assert TAIL_COLS <= IN_TILE and IN_TILE % LANES == 0

PV_G_DQ = 0
PV_G_DK = PV_G_DQ + LANES
PV_G_FQ = PV_G_DK + LANES
PV_G_FK = PV_G_FQ + LANES
PV_F_BIAS = PV_G_FK + LANES
PV_G_QA = PV_F_BIAS + LANES
PV_G_KVA = PV_G_QA + MLA_Q_RANK
PV_G_QM = PV_G_KVA + MLA_KV_RANK
PV_G_KM = PV_G_QM + MLA_PAD_DIM
PV_LEN = PV_G_KM + MLA_PAD_DIM

NEG_BIG = -1e30
LOG2E = math.log2(math.e)

BF16 = jnp.bfloat16
F32 = jnp.float32


def _params(*semantics):
    return pltpu.CompilerParams(dimension_semantics=semantics,
                                vmem_limit_bytes=VMEM_LIMIT_BYTES)


def _in_proj_kernel(x_ref, g_ref, w_hbm, o_ref, wf_ref, wb_ref, *, layer):
    j = pl.program_id(0)
    first_row_tile = pl.program_id(1) == 0
    n_main = MAIN_COLS // IN_TILE

    @pl.when(jnp.logical_and(first_row_tile, j < n_main))
    def _():
        pltpu.sync_copy(w_hbm.at[pl.ds(pl.multiple_of(j * IN_TILE, IN_TILE), IN_TILE), layer], wf_ref)
        wb_ref[...] = wf_ref[...].astype(BF16)

    @pl.when(jnp.logical_and(first_row_tile, j == n_main))
    def _():
        d = wb_ref.shape[1]
        n_src = TAIL_SRC_COLS + FF_LANE0
        pltpu.sync_copy(w_hbm.at[pl.ds(MAIN_COLS - FF_LANE0, n_src), layer], wf_ref.at[pl.ds(0, n_src)])
        w = wf_ref[0:n_src, :]
        src = FF_LANE0 + N_HEADS_FOX
        wb_ref[TAIL_MQ:TAIL_KR, :] = w[src:src + TAIL_KR].astype(BF16)
        wb_ref[TAIL_KR:TAIL_KR + MLA_ROPE_DIM, :] = w[src + TAIL_KR:src + TAIL_KR + MLA_ROPE_DIM].astype(BF16)
        wb_ref[TAIL_KR + MLA_ROPE_DIM:TAIL_FF, :] = jnp.zeros((LANES - MLA_ROPE_DIM, d), BF16)
        head = w[0:BF16_ROWS]
        row = lax.broadcasted_iota(jnp.int32, head.shape, 0)
        keep = jnp.logical_and(row >= FF_LANE0, row < FF_LANE0 + N_HEADS_FOX)
        wb_ref[TAIL_FF:TAIL_FF + BF16_ROWS, :] = jnp.where(keep, head, 0.0).astype(BF16)
        wb_ref[TAIL_FF + BF16_ROWS:, :] = jnp.zeros((IN_TILE - TAIL_FF - BF16_ROWS, d), BF16)

    rows = x_ref.shape[0]
    chunk = rows // IN_ROW_CHUNKS if rows % (8 * IN_ROW_CHUNKS) == 0 else rows
    for r0 in range(0, rows, chunk):
        x = x_ref[r0:r0 + chunk, :]
        inv = lax.rsqrt(jnp.mean(x * x, axis=-1, keepdims=True) + EPS)
        h = (x * inv * g_ref[...]).astype(BF16)
        o_ref[r0:r0 + chunk, :] = lax.dot_general(h, wb_ref[...], (((1,), (1,)), ((), ())),
                                                  preferred_element_type=F32)


def _in_proj(x, g, w_in_t, layer, *, tm):
    s, d = x.shape
    assert w_in_t.shape[0] == MAIN_COLS + TAIL_SRC_COLS
    return pl.pallas_call(
        functools.partial(_in_proj_kernel, layer=layer),
        grid=(PROJ_COLS // IN_TILE, s // tm),
        in_specs=[pl.BlockSpec((tm, d), lambda j, i: (i, 0)),
                  pl.BlockSpec((1, d), lambda j, i: (0, 0)),
                  pl.BlockSpec(memory_space=pl.ANY)],
        out_specs=pl.BlockSpec((tm, IN_TILE), lambda j, i: (i, j)),
        out_shape=jax.ShapeDtypeStruct((s, PROJ_COLS), F32),
        scratch_shapes=[pltpu.VMEM((IN_TILE, d), F32), pltpu.VMEM((IN_TILE, d), BF16)],
        compiler_params=_params("arbitrary", "arbitrary"),
        name="in_proj",
    )(x, g, w_in_t)


def _rms(x, n_valid):
    ss = jnp.sum(x * x, axis=-1, keepdims=True)
    return x * lax.rsqrt(ss * (1.0 / n_valid) + EPS)


def _rms_two_halves(x):
    lo = lax.broadcasted_iota(jnp.int32, x.shape, 1) < DIFF_QK_DIM
    x2 = x * x
    s_lo = jnp.sum(jnp.where(lo, x2, 0.0), axis=-1, keepdims=True)
    s_hi = jnp.sum(jnp.where(lo, 0.0, x2), axis=-1, keepdims=True)
    return x * lax.rsqrt(jnp.where(lo, s_lo, s_hi) * (1.0 / DIFF_QK_DIM) + EPS)


def _shift_pair_bf16(half):
    src = lax.broadcasted_iota(jnp.int32, (LANES, 2 * LANES), 0)
    dst = lax.broadcasted_iota(jnp.int32, (LANES, 2 * LANES), 1)
    want = jnp.where(dst < LANES, dst + half, dst - LANES - half)
    return jnp.where(src == jnp.bitwise_and(want, LANES - 1), 1.0, 0.0).astype(BF16)


def _rope(x, c, s_up, s_dn, shift_pair):
    shifted = jnp.dot(x.astype(BF16), shift_pair, preferred_element_type=F32)
    return x * c + shifted[:, :LANES] * s_up + shifted[:, LANES:] * s_dn


def _identity_bf16(n):
    row = lax.broadcasted_iota(jnp.int32, (n, n), 0)
    col = lax.broadcasted_iota(jnp.int32, (n, n), 1)
    return jnp.where(row == col, 1.0, 0.0).astype(BF16)


def _t_bf16(x, eye):
    return lax.dot_general(eye, x.astype(BF16), (((1,), (1,)), ((), ())),
                           preferred_element_type=F32).astype(BF16)


def _prep_kernel(proj_ref, tail_ref, cd_ref, sd_up_ref, sd_dn_ref, cm_ref, sm_up_ref, sm_dn_ref,
                 pv_ref, wq_ref, wkv_ref,
                 qd_ref, kd_ref, vd_ref, qf_ref, kf_ref, vf_ref,
                 qm_ref, km_ref, vm_ref, carry_ref, *, tm):
    @pl.when(pl.program_id(0) == 0)
    def _():
        carry_ref[...] = jnp.zeros_like(carry_ref)

    def pv(off, n):
        return pv_ref[:, off:off + n]

    def lanes(ref, off, n=LANES):
        return ref[:, off:off + n]

    cd, sd_up, sd_dn = cd_ref[...], sd_up_ref[...], sd_dn_ref[...]
    cm, sm_up, sm_dn = cm_ref[...], sm_up_ref[...], sm_dn_ref[...]
    half_d = _shift_pair_bf16(PARTIAL_ROT_DIM // 2)
    half_m = _shift_pair_bf16(MLA_ROPE_DIM // 2)
    eye = _identity_bf16(LANES)

    g_dq, g_dk = pv(PV_G_DQ, LANES), pv(PV_G_DK, LANES)
    for b in range(N_HEADS_DIFF):
        q = _rope(_rms_two_halves(lanes(proj_ref, OFF_DQ + b * LANES)) * g_dq, cd, sd_up, sd_dn, half_d)
        qd_ref[b] = _t_bf16(q * (DIFF_QK_DIM ** -0.5 * LOG2E), eye)
        k = _rope(_rms_two_halves(lanes(proj_ref, OFF_DK + b * LANES)) * g_dk, cd, sd_up, sd_dn, half_d)
        kd_ref[:, b * LANES:(b + 1) * LANES] = k.astype(BF16)
        vd_ref[b] = _t_bf16(lanes(proj_ref, OFF_DV + b * LANES), eye)

    z = lanes(tail_ref, TAIL_FF) + pv(PV_F_BIAS, LANES)
    c = jnp.minimum(z, 0.0) - jnp.log1p(jnp.exp(-jnp.abs(z)))
    row = lax.broadcasted_iota(jnp.int32, c.shape, 0)
    shift = 1
    while shift < tm:
        c = c + jnp.where(row >= shift, pltpu.roll(c, shift, 0), 0.0)
        shift *= 2
    c = c + carry_ref[...]
    carry_ref[...] = c[tm - 1:tm, :]
    c2 = c * LOG2E

    g_fq, g_fk = pv(PV_G_FQ, LANES), pv(PV_G_FK, LANES)
    lane = lax.broadcasted_iota(jnp.int32, (tm, LANES), 1)
    ones_rows = jnp.where(lax.broadcasted_iota(jnp.int32, (LANES, tm), 0) < DECAY_PIECES, 1.0, 0.0)
    for h in range(N_HEADS_FOX):
        base = h * FOX_PAD_DIM
        q = _rms(lanes(proj_ref, OFF_FQ + h * LANES), FOX_HEAD_DIM) * g_fq
        qf_ref[h, 0:LANES, :] = _t_bf16(q * (FOX_HEAD_DIM ** -0.5 * LOG2E), eye)
        qf_ref[h, LANES:, :] = ones_rows.astype(BF16)
        k = _rms(lanes(proj_ref, OFF_FK + h * LANES), FOX_HEAD_DIM) * g_fk
        kf_ref[:, base:base + LANES] = k.astype(BF16)
        rest = -jnp.broadcast_to(c2[:, FF_LANE0 + h:FF_LANE0 + h + 1], (tm, LANES))
        decay = jnp.zeros((tm, LANES), F32)
        for piece in range(DECAY_PIECES):
            part = rest.astype(BF16).astype(F32)
            decay = jnp.where(lane == piece, part, decay)
            rest = rest - part
        kf_ref[:, base + LANES:base + FOX_PAD_DIM] = decay.astype(BF16)
        vf_ref[h] = _t_bf16(lanes(proj_ref, OFF_FV + h * LANES), eye)

    q_lat = (_rms(lanes(tail_ref, TAIL_MQ, MLA_Q_RANK), MLA_Q_RANK) * pv(PV_G_QA, MLA_Q_RANK)).astype(BF16)
    q_up = jnp.dot(q_lat, wq_ref[...], preferred_element_type=F32)
    c_kv = (_rms(lanes(tail_ref, TAIL_CKV, MLA_KV_RANK), MLA_KV_RANK) * pv(PV_G_KVA, MLA_KV_RANK)).astype(BF16)
    kv_up = jnp.dot(c_kv, wkv_ref[...], preferred_element_type=F32)
    g_qn, g_qr = pv(PV_G_QM, LANES), pv(PV_G_QM + LANES, LANES)
    g_kn, g_kr = pv(PV_G_KM, LANES), pv(PV_G_KM + LANES, LANES)
    k_rope = _rope(_rms(lanes(tail_ref, TAIL_KR), MLA_ROPE_DIM) * g_kr, cm, sm_up, sm_dn, half_m).astype(BF16)
    mla_scale = MLA_QK_DIM ** -0.5 * LOG2E
    v_off = N_HEADS_MLA * MLA_NOPE_DIM
    for h in range(N_HEADS_MLA):
        base = h * MLA_PAD_DIM
        q_nope = _rms(q_up[:, base:base + LANES], MLA_NOPE_DIM) * g_qn
        q_rope = _rope(_rms(q_up[:, base + LANES:base + 2 * LANES], MLA_ROPE_DIM) * g_qr,
                       cm, sm_up, sm_dn, half_m)
        qm_ref[h, 0:LANES, :] = _t_bf16(q_nope * mla_scale, eye)
        qm_ref[h, LANES:2 * LANES, :] = _t_bf16(q_rope * mla_scale, eye)
        k_nope = _rms(kv_up[:, h * LANES:(h + 1) * LANES], MLA_NOPE_DIM) * g_kn
        km_ref[:, base:base + LANES] = k_nope.astype(BF16)
        km_ref[:, base + LANES:base + 2 * LANES] = k_rope
        vm_ref[h] = _t_bf16(kv_up[:, v_off + h * LANES:v_off + (h + 1) * LANES], eye)


def _prep(proj, tabs, pvec, wq, wkv, *, tm):
    s = proj.shape[0]
    tail_blk = pl.BlockSpec((tm, IN_TILE), lambda i: (i, MAIN_COLS // IN_TILE))
    row_blk = lambda n: pl.BlockSpec((tm, n), lambda i: (i, 0))
    full = lambda a: pl.BlockSpec(a.shape, lambda i: (0, 0))

    def head_t(n_heads, dim):
        return (jax.ShapeDtypeStruct((n_heads, dim, s), BF16),
                pl.BlockSpec((n_heads, dim, tm), lambda i: (0, 0, i)))

    def rows(n, dtype=BF16):
        return jax.ShapeDtypeStruct((s, n), dtype), row_blk(n)

    outs = [head_t(N_HEADS_DIFF, LANES), rows(DIFF_Q_COLS), head_t(N_HEADS_DIFF, DIFF_V_DIM),
            head_t(N_HEADS_FOX, FOX_PAD_DIM), rows(N_HEADS_FOX * FOX_PAD_DIM),
            head_t(N_HEADS_FOX, FOX_HEAD_DIM),
            head_t(N_HEADS_MLA, MLA_PAD_DIM), rows(N_HEADS_MLA * MLA_PAD_DIM),
            head_t(N_HEADS_MLA, MLA_V_DIM)]
    return pl.pallas_call(
        functools.partial(_prep_kernel, tm=tm),
        grid=(s // tm,),
        in_specs=[row_blk(MAIN_COLS), tail_blk] + [row_blk(LANES)] * len(tabs)
                 + [full(pvec), full(wq), full(wkv)],
        out_specs=[o[1] for o in outs],
        out_shape=[o[0] for o in outs],
        scratch_shapes=[pltpu.VMEM((1, LANES), F32)],
        compiler_params=_params("arbitrary"),
        name="prep",
    )(proj, proj, *tabs, pvec, wq, wkv)


def _attn_kernel(*refs, mode, tq, tiles, lambda_init, n_cast):
    n_in = 5 if mode == "diff" else 3
    q_ref, k_ref, v_ref = refs[:3]
    if mode == "diff":
        lam_ref, sub_ref = refs[3:5]
    cast_src = refs[n_in:n_in + n_cast]
    o_ref = refs[n_in + n_cast]
    cast_dst = refs[n_in + n_cast + 1:n_in + 2 * n_cast + 1]
    m_ref, l_ref, acc_ref, s_ref = refs[n_in + 2 * n_cast + 1:]
    for src, dst in zip(cast_src, cast_dst):
        dst[...] = src[...].astype(dst.dtype)
    n = pl.program_id(1) * tiles
    dv = o_ref.shape[1]
    cw = min(tq, MXU_COLS)
    n_sub = 2 if mode == "diff" else 1
    streams = [(t, sub) for t in range(tiles) for sub in range(n_sub)]

    m_ref[...] = jnp.full(m_ref.shape, NEG_BIG, F32)
    l_ref[...] = jnp.zeros(l_ref.shape, F32)
    acc_ref[...] = jnp.zeros(acc_ref.shape, F32)

    def q_stream(st):
        t, sub = streams[st]
        q_t = q_ref[:, t * tq:(t + 1) * tq]
        if mode == "diff":
            lo = lax.broadcasted_iota(jnp.int32, q_t.shape, 0) < DIFF_QK_DIM
            zero = jnp.zeros_like(q_t)
            q_t = jnp.where(lo, q_t, zero) if sub == 0 else jnp.where(lo, zero, q_t)
        return q_t

    def scores(j, st, buf):
        start = pl.multiple_of(j * tq, tq)
        s_ref[buf] = jnp.dot(k_ref[pl.ds(start, tq), :], q_stream(st), preferred_element_type=F32)

    def softmax_pv(j, st, buf, masked):
        start = pl.multiple_of(j * tq, tq)
        v_t = v_ref[:, pl.ds(start, tq)]
        for c0 in range(0, tq, cw):
            cols = slice(c0, c0 + cw)
            s_t = s_ref[buf, :, cols]
            if masked:
                key = lax.broadcasted_iota(jnp.int32, s_t.shape, 0)
                qry = lax.broadcasted_iota(jnp.int32, s_t.shape, 1) + c0
                s_t = jnp.where(key <= qry, s_t, NEG_BIG)
            m_prev = m_ref[st, :, cols]
            m_next = jnp.maximum(m_prev, jnp.max(s_t, axis=0, keepdims=True))
            alpha = jnp.exp2(m_prev - m_next)
            m_ref[st, :, cols] = m_next
            p_t = jnp.exp2(s_t - m_next)
            l_ref[st, :, cols] = alpha * l_ref[st, :, cols] + jnp.sum(p_t, axis=0, keepdims=True)
            acc_ref[st, :, cols] = (acc_ref[st, :, cols] * alpha
                                    + jnp.dot(v_t, p_t.astype(BF16), preferred_element_type=F32))

    def run(units, following):
        assert len(units) % 2 == 0 or following is None
        seq = units + ([following] if following is not None else [])
        for pos, (j, st, masked) in enumerate(units):
            if pos + 1 < len(seq):
                scores(seq[pos + 1][0], seq[pos + 1][1], (pos + 1) % 2)
            softmax_pv(j, st, pos % 2, masked)

    def full_block(j):
        return [(j, st, False) for st in range(len(streams))]

    blocks_per_trip = max(1, UNITS_PER_TRIP // len(streams))
    assert blocks_per_trip & (blocks_per_trip - 1) == 0

    def trip(t, carry):
        base = t * blocks_per_trip
        units = [u for b in range(blocks_per_trip) for u in full_block(base + b)]
        run(units, (base + blocks_per_trip, 0, False))
        return carry

    tail = [(n + b, st, b == streams[st][0])
            for b in range(tiles) for st in range(len(streams)) if streams[st][0] >= b]

    scores(0, 0, 0)
    lax.fori_loop(0, lax.shift_right_logical(n, blocks_per_trip.bit_length() - 1), trip, 0)
    left = lax.bitwise_and(n, blocks_per_trip - 1)
    for r in range(0, blocks_per_trip, math.gcd(tiles, blocks_per_trip)):
        @pl.when(left == r)
        def _():
            run([u for b in range(r) for u in full_block(n - r + b)] + tail, None)

    def out_t(st):
        return acc_ref[st] * (1.0 / l_ref[st])

    if mode == "diff":
        lp = lam_ref[...]
        lam = (jnp.exp(jnp.sum(lp[0:1] * lp[1:2], axis=1, keepdims=True))
               - jnp.exp(jnp.sum(lp[2:3] * lp[3:4], axis=1, keepdims=True)) + lambda_init)
    for t in range(tiles):
        rows = slice(t * tq, (t + 1) * tq)
        if mode == "diff":
            o = (out_t(2 * t) - lam * out_t(2 * t + 1)).T
            o = o * lax.rsqrt(jnp.mean(o * o, axis=-1, keepdims=True) + EPS) * sub_ref[...]
            o_ref[rows, :] = (o * (1.0 - lambda_init)).astype(o_ref.dtype)
        else:
            o_ref[rows, :] = out_t(t).T.astype(o_ref.dtype)


def _attn_grid(n_heads, s, tq):
    n_tiles = s // tq
    tiles = math.gcd(n_tiles, TILES_PER_STEP)
    return tiles, (n_heads, n_tiles // tiles)


def _can_ride(weights, n_steps):
    return all(w.shape[1] % (n_steps * BF16_ROWS) == 0 for w in weights)


def _attention(q_t, k, v_t, extras, *, mode, tq, lambda_init=0.0, cast=(), cast_layer=0):
    n_heads, dk, s = q_t.shape
    dv = v_t.shape[1]
    tiles, grid = _attn_grid(n_heads, s, tq)
    n_streams = tiles * (2 if mode == "diff" else 1)
    in_specs = [pl.BlockSpec((None, dk, tiles * tq), lambda h, i: (h, 0, i)),
                pl.BlockSpec((s, dk), lambda h, i: (0, h)),
                pl.BlockSpec((None, dv, s), lambda h, i: (h, 0, 0))]
    if mode == "diff":
        lam, sub = extras
        in_specs += [pl.BlockSpec(lam.shape, lambda h, i: (0, 0)),
                     pl.BlockSpec(sub.shape, lambda h, i: (0, 0))]
    out_specs = [pl.BlockSpec((tiles * tq, dv), lambda h, i: (i, h))]
    out_shape = [jax.ShapeDtypeStruct((s, n_heads * dv), BF16)]
    n_steps = grid[0] * grid[1]
    assert _can_ride(cast, n_steps)
    for w in cast:
        rows = w.shape[1] // n_steps
        in_specs.append(pl.BlockSpec((None, rows, w.shape[2]), lambda h, i: (cast_layer, h * grid[1] + i, 0)))
        out_specs.append(pl.BlockSpec((rows, w.shape[2]), lambda h, i: (h * grid[1] + i, 0)))
        out_shape.append(jax.ShapeDtypeStruct(w.shape[1:], BF16))
    outs = pl.pallas_call(
        functools.partial(_attn_kernel, mode=mode, tq=tq, tiles=tiles, lambda_init=lambda_init,
                          n_cast=len(cast)),
        grid=grid,
        in_specs=in_specs,
        out_specs=out_specs,
        out_shape=out_shape,
        scratch_shapes=[pltpu.VMEM((n_streams, 1, tq), F32),
                        pltpu.VMEM((n_streams, 1, tq), F32),
                        pltpu.VMEM((n_streams, dv, tq), F32),
                        pltpu.VMEM((2, tq, tq), F32)],
        compiler_params=_params("arbitrary", "arbitrary"),
        name="attn_" + mode,
    )(q_t, k, v_t, *extras, *cast)
    return outs if cast else outs[0]


def _out_proj_kernel(x_ref, oa_ref, ob_ref, oc_ref, w_ref, o_ref):
    a0, a1 = DIFF_V_COLS, DIFF_V_COLS + FOX_COLS
    acc = x_ref[...]
    acc = acc + jnp.dot(oa_ref[...], w_ref[0:a0, :], preferred_element_type=F32)
    acc = acc + jnp.dot(ob_ref[...], w_ref[a0:a1, :], preferred_element_type=F32)
    acc = acc + jnp.dot(oc_ref[...], w_ref[a1:, :], preferred_element_type=F32)
    o_ref[...] = acc


def _out_proj(x, oa, ob, oc, w, layer, *, tm):
    s, d = x.shape
    row_blk = lambda n: pl.BlockSpec((tm, n), lambda i: (i, 0))
    return pl.pallas_call(
        _out_proj_kernel,
        grid=(s // tm,),
        in_specs=[row_blk(d), row_blk(oa.shape[1]), row_blk(ob.shape[1]), row_blk(oc.shape[1]),
                  pl.BlockSpec((None,) + w.shape[1:], lambda i: (layer, 0, 0))],
        out_specs=row_blk(d),
        out_shape=jax.ShapeDtypeStruct((s, d), F32),
        compiler_params=_params("arbitrary"),
        name="out_proj",
    )(x, oa, ob, oc, w)


def _ffn_kernel(x_ref, g_ref, wu_ref, wd_ref, o_ref, h_ref):
    @pl.when(pl.program_id(1) == 0)
    def _():
        x = x_ref[...]
        inv = lax.rsqrt(jnp.mean(x * x, axis=-1, keepdims=True) + EPS)
        h_ref[...] = (x * inv * g_ref[...]).astype(BF16)
        o_ref[...] = x

    u = jnp.dot(h_ref[...], wu_ref[...], preferred_element_type=F32)
    a = jnp.square(jnp.maximum(u, 0.0)).astype(BF16)
    o_ref[...] += jnp.dot(a, wd_ref[...], preferred_element_type=F32)


def _ffn(x, g, wu, wd, layer, *, tm, tf):
    s, d = x.shape
    f = wu.shape[2]
    return pl.pallas_call(
        _ffn_kernel,
        grid=(s // tm, f // tf),
        in_specs=[pl.BlockSpec((tm, d), lambda i, j: (i, 0)),
                  pl.BlockSpec((1, d), lambda i, j: (0, 0)),
                  pl.BlockSpec((None, d, tf), lambda i, j: (layer, 0, j)),
                  pl.BlockSpec((None, tf, d), lambda i, j: (layer, j, 0))],
        out_specs=pl.BlockSpec((tm, d), lambda i, j: (i, 0)),
        out_shape=jax.ShapeDtypeStruct((s, d), F32),
        scratch_shapes=[pltpu.VMEM((tm, d), BF16)],
        compiler_params=_params("arbitrary", "arbitrary"),
        name="ffn",
    )(x, g, wu, wd)


def _pack_w_q_up(w):
    w = w.reshape(MLA_Q_RANK, N_HEADS_MLA, MLA_QK_DIM)
    pad = jnp.zeros((MLA_Q_RANK, N_HEADS_MLA, MLA_PAD_DIM - MLA_QK_DIM), w.dtype)
    w = jnp.concatenate([w[:, :, MLA_ROPE_DIM:], w[:, :, :MLA_ROPE_DIM], pad], axis=2)
    return w.reshape(MLA_Q_RANK, N_HEADS_MLA * MLA_PAD_DIM).astype(BF16)


def _pack_w_kv_up(w):
    w = w.reshape(MLA_KV_RANK, N_HEADS_MLA, MLA_NOPE_DIM + MLA_V_DIM)
    k_nope = w[:, :, :MLA_NOPE_DIM].reshape(MLA_KV_RANK, N_HEADS_MLA * MLA_NOPE_DIM)
    v = w[:, :, MLA_NOPE_DIM:].reshape(MLA_KV_RANK, MLA_V_COLS)
    return jnp.concatenate([k_nope, v], axis=1).astype(BF16)


def _pack_vec(diff_q_norm, diff_k_norm, fox_q_norm, fox_k_norm, fox_forget_bias,
              mla_q_a_norm, mla_kv_a_norm, mla_q_norm, mla_k_norm):
    def nope_rope_pad(g):
        return jnp.concatenate([g[MLA_ROPE_DIM:], g[:MLA_ROPE_DIM],
                                jnp.zeros((MLA_PAD_DIM - MLA_QK_DIM,), g.dtype)])
    parts = [jnp.tile(diff_q_norm, 2), jnp.tile(diff_k_norm, 2), fox_q_norm, fox_k_norm,
             jnp.pad(fox_forget_bias, (FF_LANE0, LANES - N_HEADS_FOX - FF_LANE0)),
             mla_q_a_norm, mla_kv_a_norm, nope_rope_pad(mla_q_norm), nope_rope_pad(mla_k_norm)]
    return jnp.concatenate(parts).astype(F32).reshape(1, PV_LEN)


def _rope_tables(seq):
    def cos_sin(rot_dim):
        half = rot_dim // 2
        inv_freq = ROPE_THETA ** (-jnp.arange(half, dtype=F32) / half)
        ang = jnp.arange(seq, dtype=F32)[:, None] * inv_freq[None, :]
        return jnp.cos(ang), jnp.sin(ang)

    cos_p, sin_p = cos_sin(PARTIAL_ROT_DIM)
    cos_m, sin_m = cos_sin(MLA_ROPE_DIM)
    hp, hm = PARTIAL_ROT_DIM // 2, MLA_ROPE_DIM // 2
    ones = lambda n: jnp.ones((seq, n), F32)
    zeros = lambda n: jnp.zeros((seq, n), F32)
    rest = DIFF_QK_DIM - PARTIAL_ROT_DIM
    cd = jnp.tile(jnp.concatenate([cos_p, cos_p, ones(rest)], axis=1), (1, 2))
    sd_up = jnp.tile(jnp.concatenate([-sin_p, zeros(hp + rest)], axis=1), (1, 2))
    sd_dn = jnp.tile(jnp.concatenate([zeros(hp), sin_p, zeros(rest)], axis=1), (1, 2))
    pad = LANES - MLA_ROPE_DIM
    cm = jnp.concatenate([cos_m, cos_m, zeros(pad)], axis=1)
    sm_up = jnp.concatenate([-sin_m, zeros(hm + pad)], axis=1)
    sm_dn = jnp.concatenate([zeros(hm), sin_m, zeros(pad)], axis=1)
    return cd, sd_up, sd_dn, cm, sm_up, sm_dn


def _tile(seq, want):
    return min(seq, want)


def kernel(x, norm_mix, w_in, diff_q_norm, diff_k_norm, diff_lambda_q1, diff_lambda_k1,
           diff_lambda_q2, diff_lambda_k2, diff_subln, fox_q_norm, fox_k_norm, fox_forget_bias,
           mla_q_a_norm, mla_kv_a_norm, mla_w_q_up, mla_w_kv_up, mla_q_norm, mla_k_norm,
           w_out, norm_ffn, w_ff_up, w_ff_down):
    batch, seq, d = x.shape
    assert batch == 1 and d == D_MODEL and seq % LANES == 0
    tabs = _rope_tables(seq)
    xs = x.reshape(seq, d)
    tq = _tile(seq, 512)
    assert sum(IN_SECTIONS) == w_in.shape[-1]
    w_in_t = jnp.transpose(w_in, (2, 0, 1))
    _, diff_grid = _attn_grid(N_HEADS_DIFF, seq, tq)
    late_weights = (w_ff_up, w_ff_down, w_out)
    ride = _can_ride(late_weights, diff_grid[0] * diff_grid[1])
    for l in range(DEPTH):
        proj = _in_proj(xs, norm_mix[l].reshape(1, d), w_in_t, l, tm=_tile(seq, 512))
        pvec = _pack_vec(diff_q_norm[l], diff_k_norm[l], fox_q_norm[l], fox_k_norm[l],
                         fox_forget_bias[l], mla_q_a_norm[l], mla_kv_a_norm[l],
                         mla_q_norm[l], mla_k_norm[l])
        (qd, kd, vd, qf, kf, vf, qm, km, vm) = _prep(
            proj, tabs, pvec, _pack_w_q_up(mla_w_q_up[l]), _pack_w_kv_up(mla_w_kv_up[l]),
            tm=_tile(seq, 256))
        lambda_init = 0.8 - 0.6 * math.exp(-0.3 * l)
        lam = jnp.stack([diff_lambda_q1[l], diff_lambda_k1[l], diff_lambda_q2[l], diff_lambda_k2[l]])
        diff_extras = (lam, diff_subln[l].reshape(1, DIFF_V_DIM))
        if ride:
            o_a, w_up_b, w_down_b, w_out_b = _attention(
                qd, kd, vd, diff_extras, mode="diff", tq=tq, lambda_init=lambda_init,
                cast=late_weights, cast_layer=l)
        else:
            o_a = _attention(qd, kd, vd, diff_extras, mode="diff", tq=tq, lambda_init=lambda_init)
            w_up_b, w_down_b, w_out_b = (w[l].astype(BF16) for w in late_weights)
        o_b = _attention(qf, kf, vf, (), mode="fox", tq=tq)
        o_c = _attention(qm, km, vm, (), mode="mla", tq=tq)
        xs = _out_proj(xs, o_a, o_b, o_c, w_out_b[None], 0, tm=_tile(seq, 512))
        xs = _ffn(xs, norm_ffn[l].reshape(1, d), w_up_b[None], w_down_b[None], 0,
                  tm=_tile(seq, 512), tf=1024)
    return xs.reshape(batch, seq, d)
```

```python
import functools
import math

import jax
import jax.numpy as jnp
from jax import lax
from jax.experimental import pallas as pl
from jax.experimental.pallas import tpu as pltpu

D_MODEL = 2048
DEPTH = 2
N_HEADS_DIFF = 4
DIFF_QK_DIM = 64
DIFF_V_DIM = 128
N_HEADS_FOX = 6
FOX_HEAD_DIM = 128
N_HEADS_MLA = 6
MLA_Q_RANK = 512
MLA_KV_RANK = 256
MLA_NOPE_DIM = 128
MLA_ROPE_DIM = 64
MLA_QK_DIM = MLA_ROPE_DIM + MLA_NOPE_DIM
MLA_V_DIM = 128
D_FF = 4 * D_MODEL
ROPE_THETA = 500000.0
PARTIAL_ROT_DIM = DIFF_QK_DIM // 4
EPS = 1e-6

DIFF_Q_COLS = N_HEADS_DIFF * 2 * DIFF_QK_DIM
DIFF_V_COLS = N_HEADS_DIFF * DIFF_V_DIM
FOX_COLS = N_HEADS_FOX * FOX_HEAD_DIM
MLA_V_COLS = N_HEADS_MLA * MLA_V_DIM
IN_SECTIONS = (DIFF_Q_COLS, DIFF_Q_COLS, DIFF_V_COLS, FOX_COLS, FOX_COLS, FOX_COLS,
               N_HEADS_FOX, MLA_Q_RANK, MLA_KV_RANK + MLA_ROPE_DIM)

LANES = 128
FOX_PAD_DIM = 2 * LANES
DECAY_PIECES = 3
BF16_ROWS = 16
MXU_COLS = 256
UNITS_PER_TRIP = 8
IN_ROW_CHUNKS = 4
TILES_PER_STEP = 4
MLA_PAD_DIM = 2 * LANES
VMEM_LIMIT_BYTES = 56 * 1024 * 1024

OFF_DQ = 0
OFF_DK = OFF_DQ + DIFF_Q_COLS
OFF_DV = OFF_DK + DIFF_Q_COLS
OFF_FQ = OFF_DV + DIFF_V_COLS
OFF_FK = OFF_FQ + FOX_COLS
OFF_FV = OFF_FK + FOX_COLS
MAIN_COLS = OFF_FV + FOX_COLS
TAIL_SRC_COLS = N_HEADS_FOX + MLA_Q_RANK + MLA_KV_RANK + MLA_ROPE_DIM
FF_LANE0 = -TAIL_SRC_COLS % 8
TAIL_MQ = 0
TAIL_CKV = TAIL_MQ + MLA_Q_RANK
TAIL_KR = TAIL_CKV + MLA_KV_RANK
TAIL_FF = TAIL_KR + LANES
TAIL_COLS = TAIL_FF + LANES
IN_TILE = MAIN_COLS // 3
PROJ_COLS = MAIN_COLS + IN_TILE
assert TAIL_COLS <= IN_TILE and IN_TILE % LANES == 0

PV_G_DQ = 0
PV_G_DK = PV_G_DQ + LANES
PV_G_FQ = PV_G_DK + LANES
PV_G_FK = PV_G_FQ + LANES
PV_F_BIAS = PV_G_FK + LANES
PV_G_QA = PV_F_BIAS + LANES
PV_G_KVA = PV_G_QA + MLA_Q_RANK
PV_G_QM = PV_G_KVA + MLA_KV_RANK
PV_G_KM = PV_G_QM + MLA_PAD_DIM
PV_LEN = PV_G_KM + MLA_PAD_DIM

NEG_BIG = -1e30
LOG2E = math.log2(math.e)

BF16 = jnp.bfloat16
F32 = jnp.float32


def _params(*semantics):
    return pltpu.CompilerParams(dimension_semantics=semantics,
                                vmem_limit_bytes=VMEM_LIMIT_BYTES)


def _in_proj_kernel(x_ref, g_ref, w_hbm, o_ref, wf_ref, wb_ref, *, layer):
    j = pl.program_id(0)
    first_row_tile = pl.program_id(1) == 0
    n_main = MAIN_COLS // IN_TILE

    @pl.when(jnp.logical_and(first_row_tile, j < n_main))
    def _():
        pltpu.sync_copy(w_hbm.at[pl.ds(pl.multiple_of(j * IN_TILE, IN_TILE), IN_TILE), layer], wf_ref)
        wb_ref[...] = wf_ref[...].astype(BF16)

    @pl.when(jnp.logical_and(first_row_tile, j == n_main))
    def _():
        d = wb_ref.shape[1]
        n_src = TAIL_SRC_COLS + FF_LANE0
        pltpu.sync_copy(w_hbm.at[pl.ds(MAIN_COLS - FF_LANE0, n_src), layer], wf_ref.at[pl.ds(0, n_src)])
        w = wf_ref[0:n_src, :]
        src = FF_LANE0 + N_HEADS_FOX
        wb_ref[TAIL_MQ:TAIL_KR, :] = w[src:src + TAIL_KR].astype(BF16)
        wb_ref[TAIL_KR:TAIL_KR + MLA_ROPE_DIM, :] = w[src + TAIL_KR:src + TAIL_KR + MLA_ROPE_DIM].astype(BF16)
        wb_ref[TAIL_KR + MLA_ROPE_DIM:TAIL_FF, :] = jnp.zeros((LANES - MLA_ROPE_DIM, d), BF16)
        head = w[0:BF16_ROWS]
        row = lax.broadcasted_iota(jnp.int32, head.shape, 0)
        keep = jnp.logical_and(row >= FF_LANE0, row < FF_LANE0 + N_HEADS_FOX)
        wb_ref[TAIL_FF:TAIL_FF + BF16_ROWS, :] = jnp.where(keep, head, 0.0).astype(BF16)
        wb_ref[TAIL_FF + BF16_ROWS:, :] = jnp.zeros((IN_TILE - TAIL_FF - BF16_ROWS, d), BF16)

    rows = x_ref.shape[0]
    chunk = rows // IN_ROW_CHUNKS if rows % (8 * IN_ROW_CHUNKS) == 0 else rows
    for r0 in range(0, rows, chunk):
        x = x_ref[r0:r0 + chunk, :]
        inv = lax.rsqrt(jnp.mean(x * x, axis=-1, keepdims=True) + EPS)
        h = (x * inv * g_ref[...]).astype(BF16)
        o_ref[r0:r0 + chunk, :] = lax.dot_general(h, wb_ref[...], (((1,), (1,)), ((), ())),
                                                  preferred_element_type=F32)


def _in_proj(x, g, w_in_t, layer, *, tm):
    s, d = x.shape
    assert w_in_t.shape[0] == MAIN_COLS + TAIL_SRC_COLS
    return pl.pallas_call(
        functools.partial(_in_proj_kernel, layer=layer),
        grid=(PROJ_COLS // IN_TILE, s // tm),
        in_specs=[pl.BlockSpec((tm, d), lambda j, i: (i, 0)),
                  pl.BlockSpec((1, d), lambda j, i: (0, 0)),
                  pl.BlockSpec(memory_space=pl.ANY)],
        out_specs=pl.BlockSpec((tm, IN_TILE), lambda j, i: (i, j)),
        out_shape=jax.ShapeDtypeStruct((s, PROJ_COLS), F32),
        scratch_shapes=[pltpu.VMEM((IN_TILE, d), F32), pltpu.VMEM((IN_TILE, d), BF16)],
        compiler_params=_params("arbitrary", "arbitrary"),
        name="in_proj",
    )(x, g, w_in_t)


def _rms(x, n_valid):
    ss = jnp.sum(x * x, axis=-1, keepdims=True)
    return x * lax.rsqrt(ss * (1.0 / n_valid) + EPS)


def _rms_two_halves(x):
    lo = lax.broadcasted_iota(jnp.int32, x.shape, 1) < DIFF_QK_DIM
    x2 = x * x
    s_lo = jnp.sum(jnp.where(lo, x2, 0.0), axis=-1, keepdims=True)
    s_hi = jnp.sum(jnp.where(lo, 0.0, x2), axis=-1, keepdims=True)
    return x * lax.rsqrt(jnp.where(lo, s_lo, s_hi) * (1.0 / DIFF_QK_DIM) + EPS)


def _shift_pair_bf16(half):
    src = lax.broadcasted_iota(jnp.int32, (LANES, 2 * LANES), 0)
    dst = lax.broadcasted_iota(jnp.int32, (LANES, 2 * LANES), 1)
    want = jnp.where(dst < LANES, dst + half, dst - LANES - half)
    return jnp.where(src == jnp.bitwise_and(want, LANES - 1), 1.0, 0.0).astype(BF16)


def _rope(x, c, s_up, s_dn, shift_pair):
    shifted = jnp.dot(x.astype(BF16), shift_pair, preferred_element_type=F32)
    return x * c + shifted[:, :LANES] * s_up + shifted[:, LANES:] * s_dn


def _identity_bf16(n):
    row = lax.broadcasted_iota(jnp.int32, (n, n), 0)
    col = lax.broadcasted_iota(jnp.int32, (n, n), 1)
    return jnp.where(row == col, 1.0, 0.0).astype(BF16)


def _t_bf16(x, eye):
    return lax.dot_general(eye, x.astype(BF16), (((1,), (1,)), ((), ())),
                           preferred_element_type=F32).astype(BF16)


def _prep_kernel(proj_ref, tail_ref, cd_ref, sd_up_ref, sd_dn_ref, cm_ref, sm_up_ref, sm_dn_ref,
                 pv_ref, wq_ref, wkv_ref,
                 qd_ref, kd_ref, vd_ref, qf_ref, kf_ref, vf_ref,
                 qm_ref, km_ref, vm_ref, carry_ref, *, tm):
    @pl.when(pl.program_id(0) == 0)
    def _():
        carry_ref[...] = jnp.zeros_like(carry_ref)

    def pv(off, n):
        return pv_ref[:, off:off + n]

    def lanes(ref, off, n=LANES):
        return ref[:, off:off + n]

    cd, sd_up, sd_dn = cd_ref[...], sd_up_ref[...], sd_dn_ref[...]
    cm, sm_up, sm_dn = cm_ref[...], sm_up_ref[...], sm_dn_ref[...]
    half_d = _shift_pair_bf16(PARTIAL_ROT_DIM // 2)
    half_m = _shift_pair_bf16(MLA_ROPE_DIM // 2)
    eye = _identity_bf16(LANES)

    g_dq, g_dk = pv(PV_G_DQ, LANES), pv(PV_G_DK, LANES)
    for b in range(N_HEADS_DIFF):
        q = _rope(_rms_two_halves(lanes(proj_ref, OFF_DQ + b * LANES)) * g_dq, cd, sd_up, sd_dn, half_d)
        qd_ref[b] = _t_bf16(q * (DIFF_QK_DIM ** -0.5 * LOG2E), eye)
        k = _rope(_rms_two_halves(lanes(proj_ref, OFF_DK + b * LANES)) * g_dk, cd, sd_up, sd_dn, half_d)
        kd_ref[:, b * LANES:(b + 1) * LANES] = k.astype(BF16)
        vd_ref[b] = _t_bf16(lanes(proj_ref, OFF_DV + b * LANES), eye)

    z = lanes(tail_ref, TAIL_FF) + pv(PV_F_BIAS, LANES)
    c = jnp.minimum(z, 0.0) - jnp.log1p(jnp.exp(-jnp.abs(z)))
    row = lax.broadcasted_iota(jnp.int32, c.shape, 0)
    shift = 1
    while shift < tm:
        c = c + jnp.where(row >= shift, pltpu.roll(c, shift, 0), 0.0)
        shift *= 2
    c = c + carry_ref[...]
    carry_ref[...] = c[tm - 1:tm, :]
    c2 = c * LOG2E

    g_fq, g_fk = pv(PV_G_FQ, LANES), pv(PV_G_FK, LANES)
    lane = lax.broadcasted_iota(jnp.int32, (tm, LANES), 1)
    ones_rows = jnp.where(lax.broadcasted_iota(jnp.int32, (LANES, tm), 0) < DECAY_PIECES, 1.0, 0.0)
    for h in range(N_HEADS_FOX):
        base = h * FOX_PAD_DIM
        q = _rms(lanes(proj_ref, OFF_FQ + h * LANES), FOX_HEAD_DIM) * g_fq
        qf_ref[h, 0:LANES, :] = _t_bf16(q * (FOX_HEAD_DIM ** -0.5 * LOG2E), eye)
        qf_ref[h, LANES:, :] = ones_rows.astype(BF16)
        k = _rms(lanes(proj_ref, OFF_FK + h * LANES), FOX_HEAD_DIM) * g_fk
        kf_ref[:, base:base + LANES] = k.astype(BF16)
        rest = -jnp.broadcast_to(c2[:, FF_LANE0 + h:FF_LANE0 + h + 1], (tm, LANES))
        decay = jnp.zeros((tm, LANES), F32)
        for piece in range(DECAY_PIECES):
            part = rest.astype(BF16).astype(F32)
            decay = jnp.where(lane == piece, part, decay)
            rest = rest - part
        kf_ref[:, base + LANES:base + FOX_PAD_DIM] = decay.astype(BF16)
        vf_ref[h] = _t_bf16(lanes(proj_ref, OFF_FV + h * LANES), eye)

    q_lat = (_rms(lanes(tail_ref, TAIL_MQ, MLA_Q_RANK), MLA_Q_RANK) * pv(PV_G_QA, MLA_Q_RANK)).astype(BF16)
    q_up = jnp.dot(q_lat, wq_ref[...], preferred_element_type=F32)
    c_kv = (_rms(lanes(tail_ref, TAIL_CKV, MLA_KV_RANK), MLA_KV_RANK) * pv(PV_G_KVA, MLA_KV_RANK)).astype(BF16)
    kv_up = jnp.dot(c_kv, wkv_ref[...], preferred_element_type=F32)
    g_qn, g_qr = pv(PV_G_QM, LANES), pv(PV_G_QM + LANES, LANES)
    g_kn, g_kr = pv(PV_G_KM, LANES), pv(PV_G_KM + LANES, LANES)
    k_rope = _rope(_rms(lanes(tail_ref, TAIL_KR), MLA_ROPE_DIM) * g_kr, cm, sm_up, sm_dn, half_m).astype(BF16)
    mla_scale = MLA_QK_DIM ** -0.5 * LOG2E
    v_off = N_HEADS_MLA * MLA_NOPE_DIM
    for h in range(N_HEADS_MLA):
        base = h * MLA_PAD_DIM
        q_nope = _rms(q_up[:, base:base + LANES], MLA_NOPE_DIM) * g_qn
        q_rope = _rope(_rms(q_up[:, base + LANES:base + 2 * LANES], MLA_ROPE_DIM) * g_qr,
                       cm, sm_up, sm_dn, half_m)
        qm_ref[h, 0:LANES, :] = _t_bf16(q_nope * mla_scale, eye)
        qm_ref[h, LANES:2 * LANES, :] = _t_bf16(q_rope * mla_scale, eye)
        k_nope = _rms(kv_up[:, h * LANES:(h + 1) * LANES], MLA_NOPE_DIM) * g_kn
        km_ref[:, base:base + LANES] = k_nope.astype(BF16)
        km_ref[:, base + LANES:base + 2 * LANES] = k_rope
        vm_ref[h] = _t_bf16(kv_up[:, v_off + h * LANES:v_off + (h + 1) * LANES], eye)


def _prep(proj, tabs, pvec, wq, wkv, *, tm):
    s = proj.shape[0]
    tail_blk = pl.BlockSpec((tm, IN_TILE), lambda i: (i, MAIN_COLS // IN_TILE))
    row_blk = lambda n: pl.BlockSpec((tm, n), lambda i: (i, 0))
    full = lambda a: pl.BlockSpec(a.shape, lambda i: (0, 0))

    def head_t(n_heads, dim):
        return (jax.ShapeDtypeStruct((n_heads, dim, s), BF16),
                pl.BlockSpec((n_heads, dim, tm), lambda i: (0, 0, i)))

    def rows(n, dtype=BF16):
        return jax.ShapeDtypeStruct((s, n), dtype), row_blk(n)

    outs = [head_t(N_HEADS_DIFF, LANES), rows(DIFF_Q_COLS), head_t(N_HEADS_DIFF, DIFF_V_DIM),
            head_t(N_HEADS_FOX, FOX_PAD_DIM), rows(N_HEADS_FOX * FOX_PAD_DIM),
            head_t(N_HEADS_FOX, FOX_HEAD_DIM),
            head_t(N_HEADS_MLA, MLA_PAD_DIM), rows(N_HEADS_MLA * MLA_PAD_DIM),
            head_t(N_HEADS_MLA, MLA_V_DIM)]
    return pl.pallas_call(
        functools.partial(_prep_kernel, tm=tm),
        grid=(s // tm,),
        in_specs=[row_blk(MAIN_COLS), tail_blk] + [row_blk(LANES)] * len(tabs)
                 + [full(pvec), full(wq), full(wkv)],
        out_specs=[o[1] for o in outs],
        out_shape=[o[0] for o in outs],
        scratch_shapes=[pltpu.VMEM((1, LANES), F32)],
        compiler_params=_params("arbitrary"),
        name="prep",
    )(proj, proj, *tabs, pvec, wq, wkv)


def _attn_kernel(*refs, mode, tq, tiles, lambda_init, n_cast):
    n_in = 5 if mode == "diff" else 3
    q_ref, k_ref, v_ref = refs[:3]
    if mode == "diff":
        lam_ref, sub_ref = refs[3:5]
    cast_src = refs[n_in:n_in + n_cast]
    o_ref = refs[n_in + n_cast]
    cast_dst = refs[n_in + n_cast + 1:n_in + 2 * n_cast + 1]
    m_ref, l_ref, acc_ref, s_ref = refs[n_in + 2 * n_cast + 1:]
    for src, dst in zip(cast_src, cast_dst):
        dst[...] = src[...].astype(dst.dtype)
    n = pl.program_id(1) * tiles
    dv = o_ref.shape[1]
    cw = min(tq, MXU_COLS)
    n_sub = 2 if mode == "diff" else 1
    streams = [(t, sub) for t in range(tiles) for sub in range(n_sub)]

    m_ref[...] = jnp.full(m_ref.shape, NEG_BIG, F32)
    l_ref[...] = jnp.zeros(l_ref.shape, F32)
    acc_ref[...] = jnp.zeros(acc_ref.shape, F32)

    def q_stream(st):
        t, sub = streams[st]
        q_t = q_ref[:, t * tq:(t + 1) * tq]
        if mode == "diff":
            lo = lax.broadcasted_iota(jnp.int32, q_t.shape, 0) < DIFF_QK_DIM
            zero = jnp.zeros_like(q_t)
            q_t = jnp.where(lo, q_t, zero) if sub == 0 else jnp.where(lo, zero, q_t)
        return q_t

    def scores(j, st, buf):
        start = pl.multiple_of(j * tq, tq)
        s_ref[buf] = jnp.dot(k_ref[pl.ds(start, tq), :], q_stream(st), preferred_element_type=F32)

    def softmax_pv(j, st, buf, masked):
        start = pl.multiple_of(j * tq, tq)
        v_t = v_ref[:, pl.ds(start, tq)]
        for c0 in range(0, tq, cw):
            cols = slice(c0, c0 + cw)
            s_t = s_ref[buf, :, cols]
            if masked:
                key = lax.broadcasted_iota(jnp.int32, s_t.shape, 0)
                qry = lax.broadcasted_iota(jnp.int32, s_t.shape, 1) + c0
                s_t = jnp.where(key <= qry, s_t, NEG_BIG)
            m_prev = m_ref[st, :, cols]
            m_next = jnp.maximum(m_prev, jnp.max(s_t, axis=0, keepdims=True))
            alpha = jnp.exp2(m_prev - m_next)
            m_ref[st, :, cols] = m_next
            p_t = jnp.exp2(s_t - m_next)
            l_ref[st, :, cols] = alpha * l_ref[st, :, cols] + jnp.sum(p_t, axis=0, keepdims=True)
            acc_ref[st, :, cols] = (acc_ref[st, :, cols] * alpha
                                    + jnp.dot(v_t, p_t.astype(BF16), preferred_element_type=F32))

    def run(units, following):
        assert len(units) % 2 == 0 or following is None
        seq = units + ([following] if following is not None else [])
        for pos, (j, st, masked) in enumerate(units):
            if pos + 1 < len(seq):
                scores(seq[pos + 1][0], seq[pos + 1][1], (pos + 1) % 2)
            softmax_pv(j, st, pos % 2, masked)

    def full_block(j):
        return [(j, st, False) for st in range(len(streams))]

    blocks_per_trip = max(1, UNITS_PER_TRIP // len(streams))
    assert blocks_per_trip & (blocks_per_trip - 1) == 0

    def trip(t, carry):
        base = t * blocks_per_trip
        units = [u for b in range(blocks_per_trip) for u in full_block(base + b)]
        run(units, (base + blocks_per_trip, 0, False))
        return carry

    tail = [(n + b, st, b == streams[st][0])
            for b in range(tiles) for st in range(len(streams)) if streams[st][0] >= b]

    scores(0, 0, 0)
    lax.fori_loop(0, lax.shift_right_logical(n, blocks_per_trip.bit_length() - 1), trip, 0)
    left = lax.bitwise_and(n, blocks_per_trip - 1)
    for r in range(0, blocks_per_trip, math.gcd(tiles, blocks_per_trip)):
        @pl.when(left == r)
        def _():
            run([u for b in range(r) for u in full_block(n - r + b)] + tail, None)

    def out_t(st):
        return acc_ref[st] * (1.0 / l_ref[st])

    if mode == "diff":
        lp = lam_ref[...]
        lam = (jnp.exp(jnp.sum(lp[0:1] * lp[1:2], axis=1, keepdims=True))
               - jnp.exp(jnp.sum(lp[2:3] * lp[3:4], axis=1, keepdims=True)) + lambda_init)
    for t in range(tiles):
        rows = slice(t * tq, (t + 1) * tq)
        if mode == "diff":
            o = (out_t(2 * t) - lam * out_t(2 * t + 1)).T
            o = o * lax.rsqrt(jnp.mean(o * o, axis=-1, keepdims=True) + EPS) * sub_ref[...]
            o_ref[rows, :] = (o * (1.0 - lambda_init)).astype(o_ref.dtype)
        else:
            o_ref[rows, :] = out_t(t).T.astype(o_ref.dtype)


def _attn_grid(n_heads, s, tq):
    n_tiles = s // tq
    tiles = math.gcd(n_tiles, TILES_PER_STEP)
    return tiles, (n_heads, n_tiles // tiles)


def _can_ride(weights, n_steps):
    return all(w.shape[1] % (n_steps * BF16_ROWS) == 0 for w in weights)


def _attention(q_t, k, v_t, extras, *, mode, tq, lambda_init=0.0, cast=(), cast_layer=0):
    n_heads, dk, s = q_t.shape
    dv = v_t.shape[1]
    tiles, grid = _attn_grid(n_heads, s, tq)
    n_streams = tiles * (2 if mode == "diff" else 1)
    in_specs = [pl.BlockSpec((None, dk, tiles * tq), lambda h, i: (h, 0, i)),
                pl.BlockSpec((s, dk), lambda h, i: (0, h)),
                pl.BlockSpec((None, dv, s), lambda h, i: (h, 0, 0))]
    if mode == "diff":
        lam, sub = extras
        in_specs += [pl.BlockSpec(lam.shape, lambda h, i: (0, 0)),
                     pl.BlockSpec(sub.shape, lambda h, i: (0, 0))]
    out_specs = [pl.BlockSpec((tiles * tq, dv), lambda h, i: (i, h))]
    out_shape = [jax.ShapeDtypeStruct((s, n_heads * dv), BF16)]
    n_steps = grid[0] * grid[1]
    assert _can_ride(cast, n_steps)
    for w in cast:
        rows = w.shape[1] // n_steps
        in_specs.append(pl.BlockSpec((None, rows, w.shape[2]), lambda h, i: (cast_layer, h * grid[1] + i, 0)))
        out_specs.append(pl.BlockSpec((rows, w.shape[2]), lambda h, i: (h * grid[1] + i, 0)))
        out_shape.append(jax.ShapeDtypeStruct(w.shape[1:], BF16))
    outs = pl.pallas_call(
        functools.partial(_attn_kernel, mode=mode, tq=tq, tiles=tiles, lambda_init=lambda_init,
                          n_cast=len(cast)),
        grid=grid,
        in_specs=in_specs,
        out_specs=out_specs,
        out_shape=out_shape,
        scratch_shapes=[pltpu.VMEM((n_streams, 1, tq), F32),
                        pltpu.VMEM((n_streams, 1, tq), F32),
                        pltpu.VMEM((n_streams, dv, tq), F32),
                        pltpu.VMEM((2, tq, tq), F32)],
        compiler_params=_params("arbitrary", "arbitrary"),
        name="attn_" + mode,
    )(q_t, k, v_t, *extras, *cast)
    return outs if cast else outs[0]


def _out_proj_kernel(x_ref, oa_ref, ob_ref, oc_ref, w_ref, o_ref):
    a0, a1 = DIFF_V_COLS, DIFF_V_COLS + FOX_COLS
    acc = x_ref[...]
    acc = acc + jnp.dot(oa_ref[...], w_ref[0:a0, :], preferred_element_type=F32)
    acc = acc + jnp.dot(ob_ref[...], w_ref[a0:a1, :], preferred_element_type=F32)
    acc = acc + jnp.dot(oc_ref[...], w_ref[a1:, :], preferred_element_type=F32)
    o_ref[...] = acc


def _out_proj(x, oa, ob, oc, w, layer, *, tm):
    s, d = x.shape
    row_blk = lambda n: pl.BlockSpec((tm, n), lambda i: (i, 0))
    return pl.pallas_call(
        _out_proj_kernel,
        grid=(s // tm,),
        in_specs=[row_blk(d), row_blk(oa.shape[1]), row_blk(ob.shape[1]), row_blk(oc.shape[1]),
                  pl.BlockSpec((None,) + w.shape[1:], lambda i: (layer, 0, 0))],
        out_specs=row_blk(d),
        out_shape=jax.ShapeDtypeStruct((s, d), F32),
        compiler_params=_params("arbitrary"),
        name="out_proj",
    )(x, oa, ob, oc, w)


def _ffn_kernel(x_ref, g_ref, wu_ref, wd_ref, o_ref, h_ref):
    @pl.when(pl.program_id(1) == 0)
    def _():
        x = x_ref[...]
        inv = lax.rsqrt(jnp.mean(x * x, axis=-1, keepdims=True) + EPS)
        h_ref[...] = (x * inv * g_ref[...]).astype(BF16)
        o_ref[...] = x

    u = jnp.dot(h_ref[...], wu_ref[...], preferred_element_type=F32)
    a = jnp.square(jnp.maximum(u, 0.0)).astype(BF16)
    o_ref[...] += jnp.dot(a, wd_ref[...], preferred_element_type=F32)


def _ffn(x, g, wu, wd, layer, *, tm, tf):
    s, d = x.shape
    f = wu.shape[2]
    return pl.pallas_call(
        _ffn_kernel,
        grid=(s // tm, f // tf),
        in_specs=[pl.BlockSpec((tm, d), lambda i, j: (i, 0)),
                  pl.BlockSpec((1, d), lambda i, j: (0, 0)),
                  pl.BlockSpec((None, d, tf), lambda i, j: (layer, 0, j)),
                  pl.BlockSpec((None, tf, d), lambda i, j: (layer, j, 0))],
        out_specs=pl.BlockSpec((tm, d), lambda i, j: (i, 0)),
        out_shape=jax.ShapeDtypeStruct((s, d), F32),
        scratch_shapes=[pltpu.VMEM((tm, d), BF16)],
        compiler_params=_params("arbitrary", "arbitrary"),
        name="ffn",
    )(x, g, wu, wd)


def _pack_w_q_up(w):
    w = w.reshape(MLA_Q_RANK, N_HEADS_MLA, MLA_QK_DIM)
    pad = jnp.zeros((MLA_Q_RANK, N_HEADS_MLA, MLA_PAD_DIM - MLA_QK_DIM), w.dtype)
    w = jnp.concatenate([w[:, :, MLA_ROPE_DIM:], w[:, :, :MLA_ROPE_DIM], pad], axis=2)
    return w.reshape(MLA_Q_RANK, N_HEADS_MLA * MLA_PAD_DIM).astype(BF16)


def _pack_w_kv_up(w):
    w = w.reshape(MLA_KV_RANK, N_HEADS_MLA, MLA_NOPE_DIM + MLA_V_DIM)
    k_nope = w[:, :, :MLA_NOPE_DIM].reshape(MLA_KV_RANK, N_HEADS_MLA * MLA_NOPE_DIM)
    v = w[:, :, MLA_NOPE_DIM:].reshape(MLA_KV_RANK, MLA_V_COLS)
    return jnp.concatenate([k_nope, v], axis=1).astype(BF16)


def _pack_vec(diff_q_norm, diff_k_norm, fox_q_norm, fox_k_norm, fox_forget_bias,
              mla_q_a_norm, mla_kv_a_norm, mla_q_norm, mla_k_norm):
    def nope_rope_pad(g):
        return jnp.concatenate([g[MLA_ROPE_DIM:], g[:MLA_ROPE_DIM],
                                jnp.zeros((MLA_PAD_DIM - MLA_QK_DIM,), g.dtype)])
    parts = [jnp.tile(diff_q_norm, 2), jnp.tile(diff_k_norm, 2), fox_q_norm, fox_k_norm,
             jnp.pad(fox_forget_bias, (FF_LANE0, LANES - N_HEADS_FOX - FF_LANE0)),
             mla_q_a_norm, mla_kv_a_norm, nope_rope_pad(mla_q_norm), nope_rope_pad(mla_k_norm)]
    return jnp.concatenate(parts).astype(F32).reshape(1, PV_LEN)


def _rope_tables(seq):
    def cos_sin(rot_dim):
        half = rot_dim // 2
        inv_freq = ROPE_THETA ** (-jnp.arange(half, dtype=F32) / half)
        ang = jnp.arange(seq, dtype=F32)[:, None] * inv_freq[None, :]
        return jnp.cos(ang), jnp.sin(ang)

    cos_p, sin_p = cos_sin(PARTIAL_ROT_DIM)
    cos_m, sin_m = cos_sin(MLA_ROPE_DIM)
    hp, hm = PARTIAL_ROT_DIM // 2, MLA_ROPE_DIM // 2
    ones = lambda n: jnp.ones((seq, n), F32)
    zeros = lambda n: jnp.zeros((seq, n), F32)
    rest = DIFF_QK_DIM - PARTIAL_ROT_DIM
    cd = jnp.tile(jnp.concatenate([cos_p, cos_p, ones(rest)], axis=1), (1, 2))
    sd_up = jnp.tile(jnp.concatenate([-sin_p, zeros(hp + rest)], axis=1), (1, 2))
    sd_dn = jnp.tile(jnp.concatenate([zeros(hp), sin_p, zeros(rest)], axis=1), (1, 2))
    pad = LANES - MLA_ROPE_DIM
    cm = jnp.concatenate([cos_m, cos_m, zeros(pad)], axis=1)
    sm_up = jnp.concatenate([-sin_m, zeros(hm + pad)], axis=1)
    sm_dn = jnp.concatenate([zeros(hm), sin_m, zeros(pad)], axis=1)
    return cd, sd_up, sd_dn, cm, sm_up, sm_dn


def _tile(seq, want):
    return min(seq, want)


def kernel(x, norm_mix, w_in, diff_q_norm, diff_k_norm, diff_lambda_q1, diff_lambda_k1,
           diff_lambda_q2, diff_lambda_k2, diff_subln, fox_q_norm, fox_k_norm, fox_forget_bias,
           mla_q_a_norm, mla_kv_a_norm, mla_w_q_up, mla_w_kv_up, mla_q_norm, mla_k_norm,
           w_out, norm_ffn, w_ff_up, w_ff_down):
    batch, seq, d = x.shape
    assert batch == 1 and d == D_MODEL and seq % LANES == 0
    tabs = _rope_tables(seq)
    xs = x.reshape(seq, d)
    tq = _tile(seq, 512)
    assert sum(IN_SECTIONS) == w_in.shape[-1]
    w_in_t = jnp.transpose(w_in, (2, 0, 1))
    _, diff_grid = _attn_grid(N_HEADS_DIFF, seq, tq)
    late_weights = (w_ff_up, w_ff_down, w_out)
    ride = _can_ride(late_weights, diff_grid[0] * diff_grid[1])
    for l in range(DEPTH):
        proj = _in_proj(xs, norm_mix[l].reshape(1, d), w_in_t, l, tm=_tile(seq, 1024))
        pvec = _pack_vec(diff_q_norm[l], diff_k_norm[l], fox_q_norm[l], fox_k_norm[l],
                         fox_forget_bias[l], mla_q_a_norm[l], mla_kv_a_norm[l],
                         mla_q_norm[l], mla_k_norm[l])
        (qd, kd, vd, qf, kf, vf, qm, km, vm) = _prep(
            proj, tabs, pvec, _pack_w_q_up(mla_w_q_up[l]), _pack_w_kv_up(mla_w_kv_up[l]),
            tm=_tile(seq, 256))
        lambda_init = 0.8 - 0.6 * math.exp(-0.3 * l)
        lam = jnp.stack([diff_lambda_q1[l], diff_lambda_k1[l], diff_lambda_q2[l], diff_lambda_k2[l]])
        diff_extras = (lam, diff_subln[l].reshape(1, DIFF_V_DIM))
        if ride:
            o_a, w_up_b, w_down_b, w_out_b = _attention(
                qd, kd, vd, diff_extras, mode="diff", tq=tq, lambda_init=lambda_init,
                cast=late_weights, cast_layer=l)
        else:
            o_a = _attention(qd, kd, vd, diff_extras, mode="diff", tq=tq, lambda_init=lambda_init)
            w_up_b, w_down_b, w_out_b = (w[l].astype(BF16) for w in late_weights)
        o_b = _attention(qf, kf, vf, (), mode="fox", tq=tq)
        o_c = _attention(qm, km, vm, (), mode="mla", tq=tq)
        xs = _out_proj(xs, o_a, o_b, o_c, w_out_b[None], 0, tm=_tile(seq, 512))
        xs = _ffn(xs, norm_ffn[l].reshape(1, d), w_up_b[None], w_down_b[None], 0,
                  tm=_tile(seq, 512), tf=1024)
    return xs.reshape(batch, seq, d)
```

```python
import functools
import math

import jax
import jax.numpy as jnp
from jax import lax
from jax.experimental import pallas as pl
from jax.experimental.pallas import tpu as pltpu

D_MODEL = 2048
DEPTH = 2
N_HEADS_DIFF = 4
DIFF_QK_DIM = 64
DIFF_V_DIM = 128
N_HEADS_FOX = 6
FOX_HEAD_DIM = 128
N_HEADS_MLA = 6
MLA_Q_RANK = 512
MLA_KV_RANK = 256
MLA_NOPE_DIM = 128
MLA_ROPE_DIM = 64
MLA_QK_DIM = MLA_ROPE_DIM + MLA_NOPE_DIM
MLA_V_DIM = 128
D_FF = 4 * D_MODEL
ROPE_THETA = 500000.0
PARTIAL_ROT_DIM = DIFF_QK_DIM // 4
EPS = 1e-6

DIFF_Q_COLS = N_HEADS_DIFF * 2 * DIFF_QK_DIM
DIFF_V_COLS = N_HEADS_DIFF * DIFF_V_DIM
FOX_COLS = N_HEADS_FOX * FOX_HEAD_DIM
MLA_V_COLS = N_HEADS_MLA * MLA_V_DIM
IN_SECTIONS = (DIFF_Q_COLS, DIFF_Q_COLS, DIFF_V_COLS, FOX_COLS, FOX_COLS, FOX_COLS,
               N_HEADS_FOX, MLA_Q_RANK, MLA_KV_RANK + MLA_ROPE_DIM)

LANES = 128
FOX_PAD_DIM = 2 * LANES
DECAY_PIECES = 3
BF16_ROWS = 16
MXU_COLS = 256
UNITS_PER_TRIP = 16
IN_ROW_CHUNKS = 4
TILES_PER_STEP = 4
MLA_PAD_DIM = 2 * LANES
VMEM_LIMIT_BYTES = 56 * 1024 * 1024

OFF_DQ = 0
OFF_DK = OFF_DQ + DIFF_Q_COLS
OFF_DV = OFF_DK + DIFF_Q_COLS
OFF_FQ = OFF_DV + DIFF_V_COLS
OFF_FK = OFF_FQ + FOX_COLS
OFF_FV = OFF_FK + FOX_COLS
MAIN_COLS = OFF_FV + FOX_COLS
TAIL_SRC_COLS = N_HEADS_FOX + MLA_Q_RANK + MLA_KV_RANK + MLA_ROPE_DIM
FF_LANE0 = -TAIL_SRC_COLS % 8
TAIL_MQ = 0
TAIL_CKV = TAIL_MQ + MLA_Q_RANK
TAIL_KR = TAIL_CKV + MLA_KV_RANK
TAIL_FF = TAIL_KR + LANES
TAIL_COLS = TAIL_FF + LANES
IN_TILE = MAIN_COLS // 3
PROJ_COLS = MAIN_COLS + IN_TILE
assert TAIL_COLS <= IN_TILE and IN_TILE % LANES == 0

PV_G_DQ = 0
PV_G_DK = PV_G_DQ + LANES
PV_G_FQ = PV_G_DK + LANES
PV_G_FK = PV_G_FQ + LANES
PV_F_BIAS = PV_G_FK + LANES
PV_G_QA = PV_F_BIAS + LANES
PV_G_KVA = PV_G_QA + MLA_Q_RANK
PV_G_QM = PV_G_KVA + MLA_KV_RANK
PV_G_KM = PV_G_QM + MLA_PAD_DIM
PV_LEN = PV_G_KM + MLA_PAD_DIM

NEG_BIG = -1e30
LOG2E = math.log2(math.e)

BF16 = jnp.bfloat16
F32 = jnp.float32


def _params(*semantics):
    return pltpu.CompilerParams(dimension_semantics=semantics,
                                vmem_limit_bytes=VMEM_LIMIT_BYTES)


def _in_proj_kernel(x_ref, g_ref, w_hbm, o_ref, wf_ref, wb_ref, *, layer):
    j = pl.program_id(0)
    first_row_tile = pl.program_id(1) == 0
    n_main = MAIN_COLS // IN_TILE

    @pl.when(jnp.logical_and(first_row_tile, j < n_main))
    def _():
        pltpu.sync_copy(w_hbm.at[pl.ds(pl.multiple_of(j * IN_TILE, IN_TILE), IN_TILE), layer], wf_ref)
        wb_ref[...] = wf_ref[...].astype(BF16)

    @pl.when(jnp.logical_and(first_row_tile, j == n_main))
    def _():
        d = wb_ref.shape[1]
        n_src = TAIL_SRC_COLS + FF_LANE0
        pltpu.sync_copy(w_hbm.at[pl.ds(MAIN_COLS - FF_LANE0, n_src), layer], wf_ref.at[pl.ds(0, n_src)])
        w = wf_ref[0:n_src, :]
        src = FF_LANE0 + N_HEADS_FOX
        wb_ref[TAIL_MQ:TAIL_KR, :] = w[src:src + TAIL_KR].astype(BF16)
        wb_ref[TAIL_KR:TAIL_KR + MLA_ROPE_DIM, :] = w[src + TAIL_KR:src + TAIL_KR + MLA_ROPE_DIM].astype(BF16)
        wb_ref[TAIL_KR + MLA_ROPE_DIM:TAIL_FF, :] = jnp.zeros((LANES - MLA_ROPE_DIM, d), BF16)
        head = w[0:BF16_ROWS]
        row = lax.broadcasted_iota(jnp.int32, head.shape, 0)
        keep = jnp.logical_and(row >= FF_LANE0, row < FF_LANE0 + N_HEADS_FOX)
        wb_ref[TAIL_FF:TAIL_FF + BF16_ROWS, :] = jnp.where(keep, head, 0.0).astype(BF16)
        wb_ref[TAIL_FF + BF16_ROWS:, :] = jnp.zeros((IN_TILE - TAIL_FF - BF16_ROWS, d), BF16)

    rows = x_ref.shape[0]
    chunk = rows // IN_ROW_CHUNKS if rows % (8 * IN_ROW_CHUNKS) == 0 else rows
    for r0 in range(0, rows, chunk):
        x = x_ref[r0:r0 + chunk, :]
        inv = lax.rsqrt(jnp.mean(x * x, axis=-1, keepdims=True) + EPS)
        h = (x * inv * g_ref[...]).astype(BF16)
        o_ref[r0:r0 + chunk, :] = lax.dot_general(h, wb_ref[...], (((1,), (1,)), ((), ())),
                                                  preferred_element_type=F32)


def _in_proj(x, g, w_in_t, layer, *, tm):
    s, d = x.shape
    assert w_in_t.shape[0] == MAIN_COLS + TAIL_SRC_COLS
    return pl.pallas_call(
        functools.partial(_in_proj_kernel, layer=layer),
        grid=(PROJ_COLS // IN_TILE, s // tm),
        in_specs=[pl.BlockSpec((tm, d), lambda j, i: (i, 0)),
                  pl.BlockSpec((1, d), lambda j, i: (0, 0)),
                  pl.BlockSpec(memory_space=pl.ANY)],
        out_specs=pl.BlockSpec((tm, IN_TILE), lambda j, i: (i, j)),
        out_shape=jax.ShapeDtypeStruct((s, PROJ_COLS), F32),
        scratch_shapes=[pltpu.VMEM((IN_TILE, d), F32), pltpu.VMEM((IN_TILE, d), BF16)],
        compiler_params=_params("arbitrary", "arbitrary"),
        name="in_proj",
    )(x, g, w_in_t)


def _rms(x, n_valid):
    ss = jnp.sum(x * x, axis=-1, keepdims=True)
    return x * lax.rsqrt(ss * (1.0 / n_valid) + EPS)


def _rms_two_halves(x):
    lo = lax.broadcasted_iota(jnp.int32, x.shape, 1) < DIFF_QK_DIM
    x2 = x * x
    s_lo = jnp.sum(jnp.where(lo, x2, 0.0), axis=-1, keepdims=True)
    s_hi = jnp.sum(jnp.where(lo, 0.0, x2), axis=-1, keepdims=True)
    return x * lax.rsqrt(jnp.where(lo, s_lo, s_hi) * (1.0 / DIFF_QK_DIM) + EPS)


def _shift_pair_bf16(half):
    src = lax.broadcasted_iota(jnp.int32, (LANES, 2 * LANES), 0)
    dst = lax.broadcasted_iota(jnp.int32, (LANES, 2 * LANES), 1)
    want = jnp.where(dst < LANES, dst + half, dst - LANES - half)
    return jnp.where(src == jnp.bitwise_and(want, LANES - 1), 1.0, 0.0).astype(BF16)


def _rope(x, c, s_up, s_dn, shift_pair):
    shifted = jnp.dot(x.astype(BF16), shift_pair, preferred_element_type=F32)
    return x * c + shifted[:, :LANES] * s_up + shifted[:, LANES:] * s_dn


def _identity_bf16(n):
    row = lax.broadcasted_iota(jnp.int32, (n, n), 0)
    col = lax.broadcasted_iota(jnp.int32, (n, n), 1)
    return jnp.where(row == col, 1.0, 0.0).astype(BF16)


def _t_bf16(x, eye):
    return lax.dot_general(eye, x.astype(BF16), (((1,), (1,)), ((), ())),
                           preferred_element_type=F32).astype(BF16)


def _prep_kernel(proj_ref, tail_ref, cd_ref, sd_up_ref, sd_dn_ref, cm_ref, sm_up_ref, sm_dn_ref,
                 pv_ref, wq_ref, wkv_ref,
                 qd_ref, kd_ref, vd_ref, qf_ref, kf_ref, vf_ref,
                 qm_ref, km_ref, vm_ref, carry_ref, *, tm):
    @pl.when(pl.program_id(0) == 0)
    def _():
        carry_ref[...] = jnp.zeros_like(carry_ref)

    def pv(off, n):
        return pv_ref[:, off:off + n]

    def lanes(ref, off, n=LANES):
        return ref[:, off:off + n]

    cd, sd_up, sd_dn = cd_ref[...], sd_up_ref[...], sd_dn_ref[...]
    cm, sm_up, sm_dn = cm_ref[...], sm_up_ref[...], sm_dn_ref[...]
    half_d = _shift_pair_bf16(PARTIAL_ROT_DIM // 2)
    half_m = _shift_pair_bf16(MLA_ROPE_DIM // 2)
    eye = _identity_bf16(LANES)

    g_dq, g_dk = pv(PV_G_DQ, LANES), pv(PV_G_DK, LANES)
    for b in range(N_HEADS_DIFF):
        q = _rope(_rms_two_halves(lanes(proj_ref, OFF_DQ + b * LANES)) * g_dq, cd, sd_up, sd_dn, half_d)
        qd_ref[b] = _t_bf16(q * (DIFF_QK_DIM ** -0.5 * LOG2E), eye)
        k = _rope(_rms_two_halves(lanes(proj_ref, OFF_DK + b * LANES)) * g_dk, cd, sd_up, sd_dn, half_d)
        kd_ref[:, b * LANES:(b + 1) * LANES] = k.astype(BF16)
        vd_ref[b] = _t_bf16(lanes(proj_ref, OFF_DV + b * LANES), eye)

    z = lanes(tail_ref, TAIL_FF) + pv(PV_F_BIAS, LANES)
    c = jnp.minimum(z, 0.0) - jnp.log1p(jnp.exp(-jnp.abs(z)))
    row = lax.broadcasted_iota(jnp.int32, c.shape, 0)
    shift = 1
    while shift < tm:
        c = c + jnp.where(row >= shift, pltpu.roll(c, shift, 0), 0.0)
        shift *= 2
    c = c + carry_ref[...]
    carry_ref[...] = c[tm - 1:tm, :]
    c2 = c * LOG2E

    g_fq, g_fk = pv(PV_G_FQ, LANES), pv(PV_G_FK, LANES)
    lane = lax.broadcasted_iota(jnp.int32, (tm, LANES), 1)
    ones_rows = jnp.where(lax.broadcasted_iota(jnp.int32, (LANES, tm), 0) < DECAY_PIECES, 1.0, 0.0)
    for h in range(N_HEADS_FOX):
        base = h * FOX_PAD_DIM
        q = _rms(lanes(proj_ref, OFF_FQ + h * LANES), FOX_HEAD_DIM) * g_fq
        qf_ref[h, 0:LANES, :] = _t_bf16(q * (FOX_HEAD_DIM ** -0.5 * LOG2E), eye)
        qf_ref[h, LANES:, :] = ones_rows.astype(BF16)
        k = _rms(lanes(proj_ref, OFF_FK + h * LANES), FOX_HEAD_DIM) * g_fk
        kf_ref[:, base:base + LANES] = k.astype(BF16)
        rest = -jnp.broadcast_to(c2[:, FF_LANE0 + h:FF_LANE0 + h + 1], (tm, LANES))
        decay = jnp.zeros((tm, LANES), F32)
        for piece in range(DECAY_PIECES):
            part = rest.astype(BF16).astype(F32)
            decay = jnp.where(lane == piece, part, decay)
            rest = rest - part
        kf_ref[:, base + LANES:base + FOX_PAD_DIM] = decay.astype(BF16)
        vf_ref[h] = _t_bf16(lanes(proj_ref, OFF_FV + h * LANES), eye)

    q_lat = (_rms(lanes(tail_ref, TAIL_MQ, MLA_Q_RANK), MLA_Q_RANK) * pv(PV_G_QA, MLA_Q_RANK)).astype(BF16)
    q_up = jnp.dot(q_lat, wq_ref[...], preferred_element_type=F32)
    c_kv = (_rms(lanes(tail_ref, TAIL_CKV, MLA_KV_RANK), MLA_KV_RANK) * pv(PV_G_KVA, MLA_KV_RANK)).astype(BF16)
    kv_up = jnp.dot(c_kv, wkv_ref[...], preferred_element_type=F32)
    g_qn, g_qr = pv(PV_G_QM, LANES), pv(PV_G_QM + LANES, LANES)
    g_kn, g_kr = pv(PV_G_KM, LANES), pv(PV_G_KM + LANES, LANES)
    k_rope = _rope(_rms(lanes(tail_ref, TAIL_KR), MLA_ROPE_DIM) * g_kr, cm, sm_up, sm_dn, half_m).astype(BF16)
    mla_scale = MLA_QK_DIM ** -0.5 * LOG2E
    v_off = N_HEADS_MLA * MLA_NOPE_DIM
    for h in range(N_HEADS_MLA):
        base = h * MLA_PAD_DIM
        q_nope = _rms(q_up[:, base:base + LANES], MLA_NOPE_DIM) * g_qn
        q_rope = _rope(_rms(q_up[:, base + LANES:base + 2 * LANES], MLA_ROPE_DIM) * g_qr,
                       cm, sm_up, sm_dn, half_m)
        qm_ref[h, 0:LANES, :] = _t_bf16(q_nope * mla_scale, eye)
        qm_ref[h, LANES:2 * LANES, :] = _t_bf16(q_rope * mla_scale, eye)
        k_nope = _rms(kv_up[:, h * LANES:(h + 1) * LANES], MLA_NOPE_DIM) * g_kn
        km_ref[:, base:base + LANES] = k_nope.astype(BF16)
        km_ref[:, base + LANES:base + 2 * LANES] = k_rope
        vm_ref[h] = _t_bf16(kv_up[:, v_off + h * LANES:v_off + (h + 1) * LANES], eye)


def _prep(proj, tabs, pvec, wq, wkv, *, tm):
    s = proj.shape[0]
    tail_blk = pl.BlockSpec((tm, IN_TILE), lambda i: (i, MAIN_COLS // IN_TILE))
    row_blk = lambda n: pl.BlockSpec((tm, n), lambda i: (i, 0))
    full = lambda a: pl.BlockSpec(a.shape, lambda i: (0, 0))

    def head_t(n_heads, dim):
        return (jax.ShapeDtypeStruct((n_heads, dim, s), BF16),
                pl.BlockSpec((n_heads, dim, tm), lambda i: (0, 0, i)))

    def rows(n, dtype=BF16):
        return jax.ShapeDtypeStruct((s, n), dtype), row_blk(n)

    outs = [head_t(N_HEADS_DIFF, LANES), rows(DIFF_Q_COLS), head_t(N_HEADS_DIFF, DIFF_V_DIM),
            head_t(N_HEADS_FOX, FOX_PAD_DIM), rows(N_HEADS_FOX * FOX_PAD_DIM),
            head_t(N_HEADS_FOX, FOX_HEAD_DIM),
            head_t(N_HEADS_MLA, MLA_PAD_DIM), rows(N_HEADS_MLA * MLA_PAD_DIM),
            head_t(N_HEADS_MLA, MLA_V_DIM)]
    return pl.pallas_call(
        functools.partial(_prep_kernel, tm=tm),
        grid=(s // tm,),
        in_specs=[row_blk(MAIN_COLS), tail_blk] + [row_blk(LANES)] * len(tabs)
                 + [full(pvec), full(wq), full(wkv)],
        out_specs=[o[1] for o in outs],
        out_shape=[o[0] for o in outs],
        scratch_shapes=[pltpu.VMEM((1, LANES), F32)],
        compiler_params=_params("arbitrary"),
        name="prep",
    )(proj, proj, *tabs, pvec, wq, wkv)


def _attn_kernel(*refs, mode, tq, tiles, lambda_init, n_cast):
    n_in = 5 if mode == "diff" else 3
    q_ref, k_ref, v_ref = refs[:3]
    if mode == "diff":
        lam_ref, sub_ref = refs[3:5]
    cast_src = refs[n_in:n_in + n_cast]
    o_ref = refs[n_in + n_cast]
    cast_dst = refs[n_in + n_cast + 1:n_in + 2 * n_cast + 1]
    m_ref, l_ref, acc_ref, s_ref = refs[n_in + 2 * n_cast + 1:]
    for src, dst in zip(cast_src, cast_dst):
        dst[...] = src[...].astype(dst.dtype)
    n = pl.program_id(1) * tiles
    dv = o_ref.shape[1]
    cw = min(tq, MXU_COLS)
    n_sub = 2 if mode == "diff" else 1
    streams = [(t, sub) for t in range(tiles) for sub in range(n_sub)]

    m_ref[...] = jnp.full(m_ref.shape, NEG_BIG, F32)
    l_ref[...] = jnp.zeros(l_ref.shape, F32)
    acc_ref[...] = jnp.zeros(acc_ref.shape, F32)

    def q_stream(st):
        t, sub = streams[st]
        q_t = q_ref[:, t * tq:(t + 1) * tq]
        if mode == "diff":
            lo = lax.broadcasted_iota(jnp.int32, q_t.shape, 0) < DIFF_QK_DIM
            zero = jnp.zeros_like(q_t)
            q_t = jnp.where(lo, q_t, zero) if sub == 0 else jnp.where(lo, zero, q_t)
        return q_t

    def scores(j, st, buf):
        start = pl.multiple_of(j * tq, tq)
        s_ref[buf] = jnp.dot(k_ref[pl.ds(start, tq), :], q_stream(st), preferred_element_type=F32)

    def softmax_pv(j, st, buf, masked):
        start = pl.multiple_of(j * tq, tq)
        v_t = v_ref[:, pl.ds(start, tq)]
        for c0 in range(0, tq, cw):
            cols = slice(c0, c0 + cw)
            s_t = s_ref[buf, :, cols]
            if masked:
                key = lax.broadcasted_iota(jnp.int32, s_t.shape, 0)
                qry = lax.broadcasted_iota(jnp.int32, s_t.shape, 1) + c0
                s_t = jnp.where(key <= qry, s_t, NEG_BIG)
            m_prev = m_ref[st, :, cols]
            m_next = jnp.maximum(m_prev, jnp.max(s_t, axis=0, keepdims=True))
            alpha = jnp.exp2(m_prev - m_next)
            m_ref[st, :, cols] = m_next
            p_t = jnp.exp2(s_t - m_next)
            l_ref[st, :, cols] = alpha * l_ref[st, :, cols] + jnp.sum(p_t, axis=0, keepdims=True)
            acc_ref[st, :, cols] = (acc_ref[st, :, cols] * alpha
                                    + jnp.dot(v_t, p_t.astype(BF16), preferred_element_type=F32))

    def run(units, following):
        assert len(units) % 2 == 0 or following is None
        seq = units + ([following] if following is not None else [])
        for pos, (j, st, masked) in enumerate(units):
            if pos + 1 < len(seq):
                scores(seq[pos + 1][0], seq[pos + 1][1], (pos + 1) % 2)
            softmax_pv(j, st, pos % 2, masked)

    def full_block(j):
        return [(j, st, False) for st in range(len(streams))]

    blocks_per_trip = max(1, UNITS_PER_TRIP // len(streams))
    assert blocks_per_trip & (blocks_per_trip - 1) == 0

    def trip(t, carry):
        base = t * blocks_per_trip
        units = [u for b in range(blocks_per_trip) for u in full_block(base + b)]
        run(units, (base + blocks_per_trip, 0, False))
        return carry

    tail = [(n + b, st, b == streams[st][0])
            for b in range(tiles) for st in range(len(streams)) if streams[st][0] >= b]

    scores(0, 0, 0)
    lax.fori_loop(0, lax.shift_right_logical(n, blocks_per_trip.bit_length() - 1), trip, 0)
    left = lax.bitwise_and(n, blocks_per_trip - 1)
    for r in range(0, blocks_per_trip, math.gcd(tiles, blocks_per_trip)):
        @pl.when(left == r)
        def _():
            run([u for b in range(r) for u in full_block(n - r + b)] + tail, None)

    def out_t(st):
        return acc_ref[st] * (1.0 / l_ref[st])

    if mode == "diff":
        lp = lam_ref[...]
        lam = (jnp.exp(jnp.sum(lp[0:1] * lp[1:2], axis=1, keepdims=True))
               - jnp.exp(jnp.sum(lp[2:3] * lp[3:4], axis=1, keepdims=True)) + lambda_init)
    for t in range(tiles):
        rows = slice(t * tq, (t + 1) * tq)
        if mode == "diff":
            o = (out_t(2 * t) - lam * out_t(2 * t + 1)).T
            o = o * lax.rsqrt(jnp.mean(o * o, axis=-1, keepdims=True) + EPS) * sub_ref[...]
            o_ref[rows, :] = (o * (1.0 - lambda_init)).astype(o_ref.dtype)
        else:
            o_ref[rows, :] = out_t(t).T.astype(o_ref.dtype)


def _attn_grid(n_heads, s, tq):
    n_tiles = s // tq
    tiles = math.gcd(n_tiles, TILES_PER_STEP)
    return tiles, (n_heads, n_tiles // tiles)


def _can_ride(weights, n_steps):
    return all(w.shape[1] % (n_steps * BF16_ROWS) == 0 for w in weights)


def _attention(q_t, k, v_t, extras, *, mode, tq, lambda_init=0.0, cast=(), cast_layer=0):
    n_heads, dk, s = q_t.shape
    dv = v_t.shape[1]
    tiles, grid = _attn_grid(n_heads, s, tq)
    n_streams = tiles * (2 if mode == "diff" else 1)
    in_specs = [pl.BlockSpec((None, dk, tiles * tq), lambda h, i: (h, 0, i)),
                pl.BlockSpec((s, dk), lambda h, i: (0, h)),
                pl.BlockSpec((None, dv, s), lambda h, i: (h, 0, 0))]
    if mode == "diff":
        lam, sub = extras
        in_specs += [pl.BlockSpec(lam.shape, lambda h, i: (0, 0)),
                     pl.BlockSpec(sub.shape, lambda h, i: (0, 0))]
    out_specs = [pl.BlockSpec((tiles * tq, dv), lambda h, i: (i, h))]
    out_shape = [jax.ShapeDtypeStruct((s, n_heads * dv), BF16)]
    n_steps = grid[0] * grid[1]
    assert _can_ride(cast, n_steps)
    for w in cast:
        rows = w.shape[1] // n_steps
        in_specs.append(pl.BlockSpec((None, rows, w.shape[2]), lambda h, i: (cast_layer, h * grid[1] + i, 0)))
        out_specs.append(pl.BlockSpec((rows, w.shape[2]), lambda h, i: (h * grid[1] + i, 0)))
        out_shape.append(jax.ShapeDtypeStruct(w.shape[1:], BF16))
    outs = pl.pallas_call(
        functools.partial(_attn_kernel, mode=mode, tq=tq, tiles=tiles, lambda_init=lambda_init,
                          n_cast=len(cast)),
        grid=grid,
        in_specs=in_specs,
        out_specs=out_specs,
        out_shape=out_shape,
        scratch_shapes=[pltpu.VMEM((n_streams, 1, tq), F32),
                        pltpu.VMEM((n_streams, 1, tq), F32),
                        pltpu.VMEM((n_streams, dv, tq), F32),
                        pltpu.VMEM((2, tq, tq), F32)],
        compiler_params=_params("arbitrary", "arbitrary"),
        name="attn_" + mode,
    )(q_t, k, v_t, *extras, *cast)
    return outs if cast else outs[0]


def _out_proj_kernel(x_ref, oa_ref, ob_ref, oc_ref, w_ref, o_ref):
    a0, a1 = DIFF_V_COLS, DIFF_V_COLS + FOX_COLS
    acc = x_ref[...]
    acc = acc + jnp.dot(oa_ref[...], w_ref[0:a0, :], preferred_element_type=F32)
    acc = acc + jnp.dot(ob_ref[...], w_ref[a0:a1, :], preferred_element_type=F32)
    acc = acc + jnp.dot(oc_ref[...], w_ref[a1:, :], preferred_element_type=F32)
    o_ref[...] = acc


def _out_proj(x, oa, ob, oc, w, layer, *, tm):
    s, d = x.shape
    row_blk = lambda n: pl.BlockSpec((tm, n), lambda i: (i, 0))
    return pl.pallas_call(
        _out_proj_kernel,
        grid=(s // tm,),
        in_specs=[row_blk(d), row_blk(oa.shape[1]), row_blk(ob.shape[1]), row_blk(oc.shape[1]),
                  pl.BlockSpec((None,) + w.shape[1:], lambda i: (layer, 0, 0))],
        out_specs=row_blk(d),
        out_shape=jax.ShapeDtypeStruct((s, d), F32),
        compiler_params=_params("arbitrary"),
        name="out_proj",
    )(x, oa, ob, oc, w)


def _ffn_kernel(x_ref, g_ref, wu_ref, wd_ref, o_ref, h_ref):
    @pl.when(pl.program_id(1) == 0)
    def _():
        x = x_ref[...]
        inv = lax.rsqrt(jnp.mean(x * x, axis=-1, keepdims=True) + EPS)
        h_ref[...] = (x * inv * g_ref[...]).astype(BF16)
        o_ref[...] = x

    u = jnp.dot(h_ref[...], wu_ref[...], preferred_element_type=F32)
    a = jnp.square(jnp.maximum(u, 0.0)).astype(BF16)
    o_ref[...] += jnp.dot(a, wd_ref[...], preferred_element_type=F32)


def _ffn(x, g, wu, wd, layer, *, tm, tf):
    s, d = x.shape
    f = wu.shape[2]
    return pl.pallas_call(
        _ffn_kernel,
        grid=(s // tm, f // tf),
        in_specs=[pl.BlockSpec((tm, d), lambda i, j: (i, 0)),
                  pl.BlockSpec((1, d), lambda i, j: (0, 0)),
                  pl.BlockSpec((None, d, tf), lambda i, j: (layer, 0, j)),
                  pl.BlockSpec((None, tf, d), lambda i, j: (layer, j, 0))],
        out_specs=pl.BlockSpec((tm, d), lambda i, j: (i, 0)),
        out_shape=jax.ShapeDtypeStruct((s, d), F32),
        scratch_shapes=[pltpu.VMEM((tm, d), BF16)],
        compiler_params=_params("arbitrary", "arbitrary"),
        name="ffn",
    )(x, g, wu, wd)


def _pack_w_q_up(w):
    w = w.reshape(MLA_Q_RANK, N_HEADS_MLA, MLA_QK_DIM)
    pad = jnp.zeros((MLA_Q_RANK, N_HEADS_MLA, MLA_PAD_DIM - MLA_QK_DIM), w.dtype)
    w = jnp.concatenate([w[:, :, MLA_ROPE_DIM:], w[:, :, :MLA_ROPE_DIM], pad], axis=2)
    return w.reshape(MLA_Q_RANK, N_HEADS_MLA * MLA_PAD_DIM).astype(BF16)


def _pack_w_kv_up(w):
    w = w.reshape(MLA_KV_RANK, N_HEADS_MLA, MLA_NOPE_DIM + MLA_V_DIM)
    k_nope = w[:, :, :MLA_NOPE_DIM].reshape(MLA_KV_RANK, N_HEADS_MLA * MLA_NOPE_DIM)
    v = w[:, :, MLA_NOPE_DIM:].reshape(MLA_KV_RANK, MLA_V_COLS)
    return jnp.concatenate([k_nope, v], axis=1).astype(BF16)


def _pack_vec(diff_q_norm, diff_k_norm, fox_q_norm, fox_k_norm, fox_forget_bias,
              mla_q_a_norm, mla_kv_a_norm, mla_q_norm, mla_k_norm):
    def nope_rope_pad(g):
        return jnp.concatenate([g[MLA_ROPE_DIM:], g[:MLA_ROPE_DIM],
                                jnp.zeros((MLA_PAD_DIM - MLA_QK_DIM,), g.dtype)])
    parts = [jnp.tile(diff_q_norm, 2), jnp.tile(diff_k_norm, 2), fox_q_norm, fox_k_norm,
             jnp.pad(fox_forget_bias, (FF_LANE0, LANES - N_HEADS_FOX - FF_LANE0)),
             mla_q_a_norm, mla_kv_a_norm, nope_rope_pad(mla_q_norm), nope_rope_pad(mla_k_norm)]
    return jnp.concatenate(parts).astype(F32).reshape(1, PV_LEN)


def _rope_tables(seq):
    def cos_sin(rot_dim):
        half = rot_dim // 2
        inv_freq = ROPE_THETA ** (-jnp.arange(half, dtype=F32) / half)
        ang = jnp.arange(seq, dtype=F32)[:, None] * inv_freq[None, :]
        return jnp.cos(ang), jnp.sin(ang)

    cos_p, sin_p = cos_sin(PARTIAL_ROT_DIM)
    cos_m, sin_m = cos_sin(MLA_ROPE_DIM)
    hp, hm = PARTIAL_ROT_DIM // 2, MLA_ROPE_DIM // 2
    ones = lambda n: jnp.ones((seq, n), F32)
    zeros = lambda n: jnp.zeros((seq, n), F32)
    rest = DIFF_QK_DIM - PARTIAL_ROT_DIM
    cd = jnp.tile(jnp.concatenate([cos_p, cos_p, ones(rest)], axis=1), (1, 2))
    sd_up = jnp.tile(jnp.concatenate([-sin_p, zeros(hp + rest)], axis=1), (1, 2))
    sd_dn = jnp.tile(jnp.concatenate([zeros(hp), sin_p, zeros(rest)], axis=1), (1, 2))
    pad = LANES - MLA_ROPE_DIM
    cm = jnp.concatenate([cos_m, cos_m, zeros(pad)], axis=1)
    sm_up = jnp.concatenate([-sin_m, zeros(hm + pad)], axis=1)
    sm_dn = jnp.concatenate([zeros(hm), sin_m, zeros(pad)], axis=1)
    return cd, sd_up, sd_dn, cm, sm_up, sm_dn


def _tile(seq, want):
    return min(seq, want)


def kernel(x, norm_mix, w_in, diff_q_norm, diff_k_norm, diff_lambda_q1, diff_lambda_k1,
           diff_lambda_q2, diff_lambda_k2, diff_subln, fox_q_norm, fox_k_norm, fox_forget_bias,
           mla_q_a_norm, mla_kv_a_norm, mla_w_q_up, mla_w_kv_up, mla_q_norm, mla_k_norm,
           w_out, norm_ffn, w_ff_up, w_ff_down):
    batch, seq, d = x.shape
    assert batch == 1 and d == D_MODEL and seq % LANES == 0
    tabs = _rope_tables(seq)
    xs = x.reshape(seq, d)
    tq = _tile(seq, 512)
    assert sum(IN_SECTIONS) == w_in.shape[-1]
    w_in_t = jnp.transpose(w_in, (2, 0, 1))
    _, diff_grid = _attn_grid(N_HEADS_DIFF, seq, tq)
    late_weights = (w_ff_up, w_ff_down, w_out)
    ride = _can_ride(late_weights, diff_grid[0] * diff_grid[1])
    for l in range(DEPTH):
        proj = _in_proj(xs, norm_mix[l].reshape(1, d), w_in_t, l, tm=_tile(seq, 1024))
        pvec = _pack_vec(diff_q_norm[l], diff_k_norm[l], fox_q_norm[l], fox_k_norm[l],
                         fox_forget_bias[l], mla_q_a_norm[l], mla_kv_a_norm[l],
                         mla_q_norm[l], mla_k_norm[l])
        (qd, kd, vd, qf, kf, vf, qm, km, vm) = _prep(
            proj, tabs, pvec, _pack_w_q_up(mla_w_q_up[l]), _pack_w_kv_up(mla_w_kv_up[l]),
            tm=_tile(seq, 256))
        lambda_init = 0.8 - 0.6 * math.exp(-0.3 * l)
        lam = jnp.stack([diff_lambda_q1[l], diff_lambda_k1[l], diff_lambda_q2[l], diff_lambda_k2[l]])
        diff_extras = (lam, diff_subln[l].reshape(1, DIFF_V_DIM))
        if ride:
            o_a, w_up_b, w_down_b, w_out_b = _attention(
                qd, kd, vd, diff_extras, mode="diff", tq=tq, lambda_init=lambda_init,
                cast=late_weights, cast_layer=l)
        else:
            o_a = _attention(qd, kd, vd, diff_extras, mode="diff", tq=tq, lambda_init=lambda_init)
            w_up_b, w_down_b, w_out_b = (w[l].astype(BF16) for w in late_weights)
        o_b = _attention(qf, kf, vf, (), mode="fox", tq=tq)
        o_c = _attention(qm, km, vm, (), mode="mla", tq=tq)
        xs = _out_proj(xs, o_a, o_b, o_c, w_out_b[None], 0, tm=_tile(seq, 512))
        xs = _ffn(xs, norm_ffn[l].reshape(1, d), w_up_b[None], w_down_b[None], 0,
                  tm=_tile(seq, 512), tf=1024)
    return xs.reshape(batch, seq, d)
```

```python
import functools
import math

import jax
import jax.numpy as jnp
from jax import lax
from jax.experimental import pallas as pl
from jax.experimental.pallas import tpu as pltpu

D_MODEL = 2048
DEPTH = 2
N_HEADS_DIFF = 4
DIFF_QK_DIM = 64
DIFF_V_DIM = 128
N_HEADS_FOX = 6
FOX_HEAD_DIM = 128
N_HEADS_MLA = 6
MLA_Q_RANK = 512
MLA_KV_RANK = 256
MLA_NOPE_DIM = 128
MLA_ROPE_DIM = 64
MLA_QK_DIM = MLA_ROPE_DIM + MLA_NOPE_DIM
MLA_V_DIM = 128
D_FF = 4 * D_MODEL
ROPE_THETA = 500000.0
PARTIAL_ROT_DIM = DIFF_QK_DIM // 4
EPS = 1e-6

DIFF_Q_COLS = N_HEADS_DIFF * 2 * DIFF_QK_DIM
DIFF_V_COLS = N_HEADS_DIFF * DIFF_V_DIM
FOX_COLS = N_HEADS_FOX * FOX_HEAD_DIM
MLA_V_COLS = N_HEADS_MLA * MLA_V_DIM
IN_SECTIONS = (DIFF_Q_COLS, DIFF_Q_COLS, DIFF_V_COLS, FOX_COLS, FOX_COLS, FOX_COLS,
               N_HEADS_FOX, MLA_Q_RANK, MLA_KV_RANK + MLA_ROPE_DIM)

LANES = 128
FOX_PAD_DIM = 2 * LANES
DECAY_PIECES = 3
BF16_ROWS = 16
MXU_COLS = 256
UNITS_PER_TRIP = 16
IN_ROW_CHUNKS = 4
FFN_ROW_CHUNKS = 2
TILES_PER_STEP = 4
MLA_PAD_DIM = 2 * LANES
VMEM_LIMIT_BYTES = 56 * 1024 * 1024

OFF_DQ = 0
OFF_DK = OFF_DQ + DIFF_Q_COLS
OFF_DV = OFF_DK + DIFF_Q_COLS
OFF_FQ = OFF_DV + DIFF_V_COLS
OFF_FK = OFF_FQ + FOX_COLS
OFF_FV = OFF_FK + FOX_COLS
MAIN_COLS = OFF_FV + FOX_COLS
TAIL_SRC_COLS = N_HEADS_FOX + MLA_Q_RANK + MLA_KV_RANK + MLA_ROPE_DIM
FF_LANE0 = -TAIL_SRC_COLS % 8
TAIL_MQ = 0
TAIL_CKV = TAIL_MQ + MLA_Q_RANK
TAIL_KR = TAIL_CKV + MLA_KV_RANK
TAIL_FF = TAIL_KR + LANES
TAIL_COLS = TAIL_FF + LANES
IN_TILE = MAIN_COLS // 3
PROJ_COLS = MAIN_COLS + IN_TILE
assert TAIL_COLS <= IN_TILE and IN_TILE % LANES == 0

PV_G_DQ = 0
PV_G_DK = PV_G_DQ + LANES
PV_G_FQ = PV_G_DK + LANES
PV_G_FK = PV_G_FQ + LANES
PV_F_BIAS = PV_G_FK + LANES
PV_G_QA = PV_F_BIAS + LANES
PV_G_KVA = PV_G_QA + MLA_Q_RANK
PV_G_QM = PV_G_KVA + MLA_KV_RANK
PV_G_KM = PV_G_QM + MLA_PAD_DIM
PV_LEN = PV_G_KM + MLA_PAD_DIM

NEG_BIG = -1e30
LOG2E = math.log2(math.e)

BF16 = jnp.bfloat16
F32 = jnp.float32


def _params(*semantics):
    return pltpu.CompilerParams(dimension_semantics=semantics,
                                vmem_limit_bytes=VMEM_LIMIT_BYTES)


def _in_proj_kernel(x_ref, g_ref, w_hbm, o_ref, wf_ref, wb_ref, *, layer):
    j = pl.program_id(0)
    first_row_tile = pl.program_id(1) == 0
    n_main = MAIN_COLS // IN_TILE

    @pl.when(jnp.logical_and(first_row_tile, j < n_main))
    def _():
        pltpu.sync_copy(w_hbm.at[pl.ds(pl.multiple_of(j * IN_TILE, IN_TILE), IN_TILE), layer], wf_ref)
        wb_ref[...] = wf_ref[...].astype(BF16)

    @pl.when(jnp.logical_and(first_row_tile, j == n_main))
    def _():
        d = wb_ref.shape[1]
        n_src = TAIL_SRC_COLS + FF_LANE0
        pltpu.sync_copy(w_hbm.at[pl.ds(MAIN_COLS - FF_LANE0, n_src), layer], wf_ref.at[pl.ds(0, n_src)])
        w = wf_ref[0:n_src, :]
        src = FF_LANE0 + N_HEADS_FOX
        wb_ref[TAIL_MQ:TAIL_KR, :] = w[src:src + TAIL_KR].astype(BF16)
        wb_ref[TAIL_KR:TAIL_KR + MLA_ROPE_DIM, :] = w[src + TAIL_KR:src + TAIL_KR + MLA_ROPE_DIM].astype(BF16)
        wb_ref[TAIL_KR + MLA_ROPE_DIM:TAIL_FF, :] = jnp.zeros((LANES - MLA_ROPE_DIM, d), BF16)
        head = w[0:BF16_ROWS]
        row = lax.broadcasted_iota(jnp.int32, head.shape, 0)
        keep = jnp.logical_and(row >= FF_LANE0, row < FF_LANE0 + N_HEADS_FOX)
        wb_ref[TAIL_FF:TAIL_FF + BF16_ROWS, :] = jnp.where(keep, head, 0.0).astype(BF16)
        wb_ref[TAIL_FF + BF16_ROWS:, :] = jnp.zeros((IN_TILE - TAIL_FF - BF16_ROWS, d), BF16)

    rows = x_ref.shape[0]
    chunk = rows // IN_ROW_CHUNKS if rows % (8 * IN_ROW_CHUNKS) == 0 else rows
    for r0 in range(0, rows, chunk):
        x = x_ref[r0:r0 + chunk, :]
        inv = lax.rsqrt(jnp.mean(x * x, axis=-1, keepdims=True) + EPS)
        h = (x * inv * g_ref[...]).astype(BF16)
        o_ref[r0:r0 + chunk, :] = lax.dot_general(h, wb_ref[...], (((1,), (1,)), ((), ())),
                                                  preferred_element_type=F32)


def _in_proj(x, g, w_in_t, layer, *, tm):
    s, d = x.shape
    assert w_in_t.shape[0] == MAIN_COLS + TAIL_SRC_COLS
    return pl.pallas_call(
        functools.partial(_in_proj_kernel, layer=layer),
        grid=(PROJ_COLS // IN_TILE, s // tm),
        in_specs=[pl.BlockSpec((tm, d), lambda j, i: (i, 0)),
                  pl.BlockSpec((1, d), lambda j, i: (0, 0)),
                  pl.BlockSpec(memory_space=pl.ANY)],
        out_specs=pl.BlockSpec((tm, IN_TILE), lambda j, i: (i, j)),
        out_shape=jax.ShapeDtypeStruct((s, PROJ_COLS), F32),
        scratch_shapes=[pltpu.VMEM((IN_TILE, d), F32), pltpu.VMEM((IN_TILE, d), BF16)],
        compiler_params=_params("arbitrary", "arbitrary"),
        name="in_proj",
    )(x, g, w_in_t)


def _rms(x, n_valid):
    ss = jnp.sum(x * x, axis=-1, keepdims=True)
    return x * lax.rsqrt(ss * (1.0 / n_valid) + EPS)


def _rms_two_halves(x):
    lo = lax.broadcasted_iota(jnp.int32, x.shape, 1) < DIFF_QK_DIM
    x2 = x * x
    s_lo = jnp.sum(jnp.where(lo, x2, 0.0), axis=-1, keepdims=True)
    s_hi = jnp.sum(jnp.where(lo, 0.0, x2), axis=-1, keepdims=True)
    return x * lax.rsqrt(jnp.where(lo, s_lo, s_hi) * (1.0 / DIFF_QK_DIM) + EPS)


def _shift_pair_bf16(half):
    src = lax.broadcasted_iota(jnp.int32, (LANES, 2 * LANES), 0)
    dst = lax.broadcasted_iota(jnp.int32, (LANES, 2 * LANES), 1)
    want = jnp.where(dst < LANES, dst + half, dst - LANES - half)
    return jnp.where(src == jnp.bitwise_and(want, LANES - 1), 1.0, 0.0).astype(BF16)


def _rope(x, c, s_up, s_dn, shift_pair):
    shifted = jnp.dot(x.astype(BF16), shift_pair, preferred_element_type=F32)
    return x * c + shifted[:, :LANES] * s_up + shifted[:, LANES:] * s_dn


def _identity_bf16(n):
    row = lax.broadcasted_iota(jnp.int32, (n, n), 0)
    col = lax.broadcasted_iota(jnp.int32, (n, n), 1)
    return jnp.where(row == col, 1.0, 0.0).astype(BF16)


def _t_bf16(x, eye):
    return lax.dot_general(eye, x.astype(BF16), (((1,), (1,)), ((), ())),
                           preferred_element_type=F32).astype(BF16)


def _prep_kernel(proj_ref, tail_ref, cd_ref, sd_up_ref, sd_dn_ref, cm_ref, sm_up_ref, sm_dn_ref,
                 pv_ref, wq_ref, wkv_ref,
                 qd_ref, kd_ref, vd_ref, qf_ref, kf_ref, vf_ref,
                 qm_ref, km_ref, vm_ref, carry_ref, *, tm):
    @pl.when(pl.program_id(0) == 0)
    def _():
        carry_ref[...] = jnp.zeros_like(carry_ref)

    def pv(off, n):
        return pv_ref[:, off:off + n]

    def lanes(ref, off, n=LANES):
        return ref[:, off:off + n]

    cd, sd_up, sd_dn = cd_ref[...], sd_up_ref[...], sd_dn_ref[...]
    cm, sm_up, sm_dn = cm_ref[...], sm_up_ref[...], sm_dn_ref[...]
    half_d = _shift_pair_bf16(PARTIAL_ROT_DIM // 2)
    half_m = _shift_pair_bf16(MLA_ROPE_DIM // 2)
    eye = _identity_bf16(LANES)

    g_dq, g_dk = pv(PV_G_DQ, LANES), pv(PV_G_DK, LANES)
    for b in range(N_HEADS_DIFF):
        q = _rope(_rms_two_halves(lanes(proj_ref, OFF_DQ + b * LANES)) * g_dq, cd, sd_up, sd_dn, half_d)
        qd_ref[b] = _t_bf16(q * (DIFF_QK_DIM ** -0.5 * LOG2E), eye)
        k = _rope(_rms_two_halves(lanes(proj_ref, OFF_DK + b * LANES)) * g_dk, cd, sd_up, sd_dn, half_d)
        kd_ref[:, b * LANES:(b + 1) * LANES] = k.astype(BF16)
        vd_ref[b] = _t_bf16(lanes(proj_ref, OFF_DV + b * LANES), eye)

    z = lanes(tail_ref, TAIL_FF) + pv(PV_F_BIAS, LANES)
    c = jnp.minimum(z, 0.0) - jnp.log1p(jnp.exp(-jnp.abs(z)))
    row = lax.broadcasted_iota(jnp.int32, c.shape, 0)
    shift = 1
    while shift < tm:
        c = c + jnp.where(row >= shift, pltpu.roll(c, shift, 0), 0.0)
        shift *= 2
    c = c + carry_ref[...]
    carry_ref[...] = c[tm - 1:tm, :]
    c2 = c * LOG2E

    g_fq, g_fk = pv(PV_G_FQ, LANES), pv(PV_G_FK, LANES)
    lane = lax.broadcasted_iota(jnp.int32, (tm, LANES), 1)
    ones_rows = jnp.where(lax.broadcasted_iota(jnp.int32, (LANES, tm), 0) < DECAY_PIECES, 1.0, 0.0)
    for h in range(N_HEADS_FOX):
        base = h * FOX_PAD_DIM
        q = _rms(lanes(proj_ref, OFF_FQ + h * LANES), FOX_HEAD_DIM) * g_fq
        qf_ref[h, 0:LANES, :] = _t_bf16(q * (FOX_HEAD_DIM ** -0.5 * LOG2E), eye)
        qf_ref[h, LANES:, :] = ones_rows.astype(BF16)
        k = _rms(lanes(proj_ref, OFF_FK + h * LANES), FOX_HEAD_DIM) * g_fk
        kf_ref[:, base:base + LANES] = k.astype(BF16)
        rest = -jnp.broadcast_to(c2[:, FF_LANE0 + h:FF_LANE0 + h + 1], (tm, LANES))
        decay = jnp.zeros((tm, LANES), F32)
        for piece in range(DECAY_PIECES):
            part = rest.astype(BF16).astype(F32)
            decay = jnp.where(lane == piece, part, decay)
            rest = rest - part
        kf_ref[:, base + LANES:base + FOX_PAD_DIM] = decay.astype(BF16)
        vf_ref[h] = _t_bf16(lanes(proj_ref, OFF_FV + h * LANES), eye)

    q_lat = (_rms(lanes(tail_ref, TAIL_MQ, MLA_Q_RANK), MLA_Q_RANK) * pv(PV_G_QA, MLA_Q_RANK)).astype(BF16)
    q_up = jnp.dot(q_lat, wq_ref[...], preferred_element_type=F32)
    c_kv = (_rms(lanes(tail_ref, TAIL_CKV, MLA_KV_RANK), MLA_KV_RANK) * pv(PV_G_KVA, MLA_KV_RANK)).astype(BF16)
    kv_up = jnp.dot(c_kv, wkv_ref[...], preferred_element_type=F32)
    g_qn, g_qr = pv(PV_G_QM, LANES), pv(PV_G_QM + LANES, LANES)
    g_kn, g_kr = pv(PV_G_KM, LANES), pv(PV_G_KM + LANES, LANES)
    k_rope = _rope(_rms(lanes(tail_ref, TAIL_KR), MLA_ROPE_DIM) * g_kr, cm, sm_up, sm_dn, half_m).astype(BF16)
    mla_scale = MLA_QK_DIM ** -0.5 * LOG2E
    v_off = N_HEADS_MLA * MLA_NOPE_DIM
    for h in range(N_HEADS_MLA):
        base = h * MLA_PAD_DIM
        q_nope = _rms(q_up[:, base:base + LANES], MLA_NOPE_DIM) * g_qn
        q_rope = _rope(_rms(q_up[:, base + LANES:base + 2 * LANES], MLA_ROPE_DIM) * g_qr,
                       cm, sm_up, sm_dn, half_m)
        qm_ref[h, 0:LANES, :] = _t_bf16(q_nope * mla_scale, eye)
        qm_ref[h, LANES:2 * LANES, :] = _t_bf16(q_rope * mla_scale, eye)
        k_nope = _rms(kv_up[:, h * LANES:(h + 1) * LANES], MLA_NOPE_DIM) * g_kn
        km_ref[:, base:base + LANES] = k_nope.astype(BF16)
        km_ref[:, base + LANES:base + 2 * LANES] = k_rope
        vm_ref[h] = _t_bf16(kv_up[:, v_off + h * LANES:v_off + (h + 1) * LANES], eye)


def _prep(proj, tabs, pvec, wq, wkv, *, tm):
    s = proj.shape[0]
    tail_blk = pl.BlockSpec((tm, IN_TILE), lambda i: (i, MAIN_COLS // IN_TILE))
    row_blk = lambda n: pl.BlockSpec((tm, n), lambda i: (i, 0))
    full = lambda a: pl.BlockSpec(a.shape, lambda i: (0, 0))

    def head_t(n_heads, dim):
        return (jax.ShapeDtypeStruct((n_heads, dim, s), BF16),
                pl.BlockSpec((n_heads, dim, tm), lambda i: (0, 0, i)))

    def rows(n, dtype=BF16):
        return jax.ShapeDtypeStruct((s, n), dtype), row_blk(n)

    outs = [head_t(N_HEADS_DIFF, LANES), rows(DIFF_Q_COLS), head_t(N_HEADS_DIFF, DIFF_V_DIM),
            head_t(N_HEADS_FOX, FOX_PAD_DIM), rows(N_HEADS_FOX * FOX_PAD_DIM),
            head_t(N_HEADS_FOX, FOX_HEAD_DIM),
            head_t(N_HEADS_MLA, MLA_PAD_DIM), rows(N_HEADS_MLA * MLA_PAD_DIM),
            head_t(N_HEADS_MLA, MLA_V_DIM)]
    return pl.pallas_call(
        functools.partial(_prep_kernel, tm=tm),
        grid=(s // tm,),
        in_specs=[row_blk(MAIN_COLS), tail_blk] + [row_blk(LANES)] * len(tabs)
                 + [full(pvec), full(wq), full(wkv)],
        out_specs=[o[1] for o in outs],
        out_shape=[o[0] for o in outs],
        scratch_shapes=[pltpu.VMEM((1, LANES), F32)],
        compiler_params=_params("arbitrary"),
        name="prep",
    )(proj, proj, *tabs, pvec, wq, wkv)


def _attn_kernel(*refs, mode, tq, tiles, lambda_init, n_cast):
    n_in = 5 if mode == "diff" else 3
    q_ref, k_ref, v_ref = refs[:3]
    if mode == "diff":
        lam_ref, sub_ref = refs[3:5]
    cast_src = refs[n_in:n_in + n_cast]
    o_ref = refs[n_in + n_cast]
    cast_dst = refs[n_in + n_cast + 1:n_in + 2 * n_cast + 1]
    m_ref, l_ref, acc_ref, s_ref = refs[n_in + 2 * n_cast + 1:]
    for src, dst in zip(cast_src, cast_dst):
        dst[...] = src[...].astype(dst.dtype)
    n = pl.program_id(1) * tiles
    dv = o_ref.shape[1]
    cw = min(tq, MXU_COLS)
    n_sub = 2 if mode == "diff" else 1
    streams = [(t, sub) for t in range(tiles) for sub in range(n_sub)]

    m_ref[...] = jnp.full(m_ref.shape, NEG_BIG, F32)
    l_ref[...] = jnp.zeros(l_ref.shape, F32)
    acc_ref[...] = jnp.zeros(acc_ref.shape, F32)

    def q_stream(st):
        t, sub = streams[st]
        q_t = q_ref[:, t * tq:(t + 1) * tq]
        if mode == "diff":
            lo = lax.broadcasted_iota(jnp.int32, q_t.shape, 0) < DIFF_QK_DIM
            zero = jnp.zeros_like(q_t)
            q_t = jnp.where(lo, q_t, zero) if sub == 0 else jnp.where(lo, zero, q_t)
        return q_t

    def scores(j, st, buf):
        start = pl.multiple_of(j * tq, tq)
        s_ref[buf] = jnp.dot(k_ref[pl.ds(start, tq), :], q_stream(st), preferred_element_type=F32)

    def softmax_pv(j, st, buf, masked):
        start = pl.multiple_of(j * tq, tq)
        v_t = v_ref[:, pl.ds(start, tq)]
        for c0 in range(0, tq, cw):
            cols = slice(c0, c0 + cw)
            s_t = s_ref[buf, :, cols]
            if masked:
                key = lax.broadcasted_iota(jnp.int32, s_t.shape, 0)
                qry = lax.broadcasted_iota(jnp.int32, s_t.shape, 1) + c0
                s_t = jnp.where(key <= qry, s_t, NEG_BIG)
            m_prev = m_ref[st, :, cols]
            m_next = jnp.maximum(m_prev, jnp.max(s_t, axis=0, keepdims=True))
            alpha = jnp.exp2(m_prev - m_next)
            m_ref[st, :, cols] = m_next
            p_t = jnp.exp2(s_t - m_next)
            l_ref[st, :, cols] = alpha * l_ref[st, :, cols] + jnp.sum(p_t, axis=0, keepdims=True)
            acc_ref[st, :, cols] = (acc_ref[st, :, cols] * alpha
                                    + jnp.dot(v_t, p_t.astype(BF16), preferred_element_type=F32))

    def run(units, following):
        assert len(units) % 2 == 0 or following is None
        seq = units + ([following] if following is not None else [])
        for pos, (j, st, masked) in enumerate(units):
            if pos + 1 < len(seq):
                scores(seq[pos + 1][0], seq[pos + 1][1], (pos + 1) % 2)
            softmax_pv(j, st, pos % 2, masked)

    def full_block(j):
        return [(j, st, False) for st in range(len(streams))]

    blocks_per_trip = max(1, UNITS_PER_TRIP // len(streams))
    assert blocks_per_trip & (blocks_per_trip - 1) == 0

    def trip(t, carry):
        base = t * blocks_per_trip
        units = [u for b in range(blocks_per_trip) for u in full_block(base + b)]
        run(units, (base + blocks_per_trip, 0, False))
        return carry

    tail = [(n + b, st, b == streams[st][0])
            for b in range(tiles) for st in range(len(streams)) if streams[st][0] >= b]

    scores(0, 0, 0)
    lax.fori_loop(0, lax.shift_right_logical(n, blocks_per_trip.bit_length() - 1), trip, 0)
    left = lax.bitwise_and(n, blocks_per_trip - 1)
    for r in range(0, blocks_per_trip, math.gcd(tiles, blocks_per_trip)):
        @pl.when(left == r)
        def _():
            run([u for b in range(r) for u in full_block(n - r + b)] + tail, None)

    def out_t(st):
        return acc_ref[st] * (1.0 / l_ref[st])

    if mode == "diff":
        lp = lam_ref[...]
        lam = (jnp.exp(jnp.sum(lp[0:1] * lp[1:2], axis=1, keepdims=True))
               - jnp.exp(jnp.sum(lp[2:3] * lp[3:4], axis=1, keepdims=True)) + lambda_init)
    for t in range(tiles):
        rows = slice(t * tq, (t + 1) * tq)
        if mode == "diff":
            o = (out_t(2 * t) - lam * out_t(2 * t + 1)).T
            o = o * lax.rsqrt(jnp.mean(o * o, axis=-1, keepdims=True) + EPS) * sub_ref[...]
            o_ref[rows, :] = (o * (1.0 - lambda_init)).astype(o_ref.dtype)
        else:
            o_ref[rows, :] = out_t(t).T.astype(o_ref.dtype)


def _attn_grid(n_heads, s, tq):
    n_tiles = s // tq
    tiles = math.gcd(n_tiles, TILES_PER_STEP)
    return tiles, (n_heads, n_tiles // tiles)


def _can_ride(weights, n_steps):
    return all(w.shape[1] % (n_steps * BF16_ROWS) == 0 for w in weights)


def _attention(q_t, k, v_t, extras, *, mode, tq, lambda_init=0.0, cast=(), cast_layer=0):
    n_heads, dk, s = q_t.shape
    dv = v_t.shape[1]
    tiles, grid = _attn_grid(n_heads, s, tq)
    n_streams = tiles * (2 if mode == "diff" else 1)
    in_specs = [pl.BlockSpec((None, dk, tiles * tq), lambda h, i: (h, 0, i)),
                pl.BlockSpec((s, dk), lambda h, i: (0, h)),
                pl.BlockSpec((None, dv, s), lambda h, i: (h, 0, 0))]
    if mode == "diff":
        lam, sub = extras
        in_specs += [pl.BlockSpec(lam.shape, lambda h, i: (0, 0)),
                     pl.BlockSpec(sub.shape, lambda h, i: (0, 0))]
    out_specs = [pl.BlockSpec((tiles * tq, dv), lambda h, i: (i, h))]
    out_shape = [jax.ShapeDtypeStruct((s, n_heads * dv), BF16)]
    n_steps = grid[0] * grid[1]
    assert _can_ride(cast, n_steps)
    for w in cast:
        rows = w.shape[1] // n_steps
        in_specs.append(pl.BlockSpec((None, rows, w.shape[2]), lambda h, i: (cast_layer, h * grid[1] + i, 0)))
        out_specs.append(pl.BlockSpec((rows, w.shape[2]), lambda h, i: (h * grid[1] + i, 0)))
        out_shape.append(jax.ShapeDtypeStruct(w.shape[1:], BF16))
    outs = pl.pallas_call(
        functools.partial(_attn_kernel, mode=mode, tq=tq, tiles=tiles, lambda_init=lambda_init,
                          n_cast=len(cast)),
        grid=grid,
        in_specs=in_specs,
        out_specs=out_specs,
        out_shape=out_shape,
        scratch_shapes=[pltpu.VMEM((n_streams, 1, tq), F32),
                        pltpu.VMEM((n_streams, 1, tq), F32),
                        pltpu.VMEM((n_streams, dv, tq), F32),
                        pltpu.VMEM((2, tq, tq), F32)],
        compiler_params=_params("arbitrary", "arbitrary"),
        name="attn_" + mode,
    )(q_t, k, v_t, *extras, *cast)
    return outs if cast else outs[0]


def _out_proj_kernel(x_ref, oa_ref, ob_ref, oc_ref, w_ref, o_ref):
    a0, a1 = DIFF_V_COLS, DIFF_V_COLS + FOX_COLS
    acc = x_ref[...]
    acc = acc + jnp.dot(oa_ref[...], w_ref[0:a0, :], preferred_element_type=F32)
    acc = acc + jnp.dot(ob_ref[...], w_ref[a0:a1, :], preferred_element_type=F32)
    acc = acc + jnp.dot(oc_ref[...], w_ref[a1:, :], preferred_element_type=F32)
    o_ref[...] = acc


def _out_proj(x, oa, ob, oc, w, layer, *, tm):
    s, d = x.shape
    row_blk = lambda n: pl.BlockSpec((tm, n), lambda i: (i, 0))
    return pl.pallas_call(
        _out_proj_kernel,
        grid=(s // tm,),
        in_specs=[row_blk(d), row_blk(oa.shape[1]), row_blk(ob.shape[1]), row_blk(oc.shape[1]),
                  pl.BlockSpec((None,) + w.shape[1:], lambda i: (layer, 0, 0))],
        out_specs=row_blk(d),
        out_shape=jax.ShapeDtypeStruct((s, d), F32),
        compiler_params=_params("arbitrary"),
        name="out_proj",
    )(x, oa, ob, oc, w)


def _ffn_kernel(x_ref, g_ref, wu_ref, wd_ref, o_ref, h_ref):
    def mlp(h):
        u = jnp.dot(h, wu_ref[...], preferred_element_type=F32)
        a = jnp.square(jnp.maximum(u, 0.0)).astype(BF16)
        return jnp.dot(a, wd_ref[...], preferred_element_type=F32)

    @pl.when(pl.program_id(1) == 0)
    def _():
        rows = x_ref.shape[0]
        chunk = rows // FFN_ROW_CHUNKS if rows % (BF16_ROWS * FFN_ROW_CHUNKS) == 0 else rows
        for r0 in range(0, rows, chunk):
            x = x_ref[r0:r0 + chunk, :]
            inv = lax.rsqrt(jnp.mean(x * x, axis=-1, keepdims=True) + EPS)
            h = (x * inv * g_ref[...]).astype(BF16)
            h_ref[r0:r0 + chunk, :] = h
            o_ref[r0:r0 + chunk, :] = x + mlp(h)

    @pl.when(pl.program_id(1) > 0)
    def _():
        o_ref[...] += mlp(h_ref[...])


def _ffn(x, g, wu, wd, layer, *, tm, tf):
    s, d = x.shape
    f = wu.shape[2]
    return pl.pallas_call(
        _ffn_kernel,
        grid=(s // tm, f // tf),
        in_specs=[pl.BlockSpec((tm, d), lambda i, j: (i, 0)),
                  pl.BlockSpec((1, d), lambda i, j: (0, 0)),
                  pl.BlockSpec((None, d, tf), lambda i, j: (layer, 0, j)),
                  pl.BlockSpec((None, tf, d), lambda i, j: (layer, j, 0))],
        out_specs=pl.BlockSpec((tm, d), lambda i, j: (i, 0)),
        out_shape=jax.ShapeDtypeStruct((s, d), F32),
        scratch_shapes=[pltpu.VMEM((tm, d), BF16)],
        compiler_params=_params("arbitrary", "arbitrary"),
        name="ffn",
    )(x, g, wu, wd)


def _pack_w_q_up(w):
    w = w.reshape(MLA_Q_RANK, N_HEADS_MLA, MLA_QK_DIM)
    pad = jnp.zeros((MLA_Q_RANK, N_HEADS_MLA, MLA_PAD_DIM - MLA_QK_DIM), w.dtype)
    w = jnp.concatenate([w[:, :, MLA_ROPE_DIM:], w[:, :, :MLA_ROPE_DIM], pad], axis=2)
    return w.reshape(MLA_Q_RANK, N_HEADS_MLA * MLA_PAD_DIM).astype(BF16)


def _pack_w_kv_up(w):
    w = w.reshape(MLA_KV_RANK, N_HEADS_MLA, MLA_NOPE_DIM + MLA_V_DIM)
    k_nope = w[:, :, :MLA_NOPE_DIM].reshape(MLA_KV_RANK, N_HEADS_MLA * MLA_NOPE_DIM)
    v = w[:, :, MLA_NOPE_DIM:].reshape(MLA_KV_RANK, MLA_V_COLS)
    return jnp.concatenate([k_nope, v], axis=1).astype(BF16)


def _pack_vec(diff_q_norm, diff_k_norm, fox_q_norm, fox_k_norm, fox_forget_bias,
              mla_q_a_norm, mla_kv_a_norm, mla_q_norm, mla_k_norm):
    def nope_rope_pad(g):
        return jnp.concatenate([g[MLA_ROPE_DIM:], g[:MLA_ROPE_DIM],
                                jnp.zeros((MLA_PAD_DIM - MLA_QK_DIM,), g.dtype)])
    parts = [jnp.tile(diff_q_norm, 2), jnp.tile(diff_k_norm, 2), fox_q_norm, fox_k_norm,
             jnp.pad(fox_forget_bias, (FF_LANE0, LANES - N_HEADS_FOX - FF_LANE0)),
             mla_q_a_norm, mla_kv_a_norm, nope_rope_pad(mla_q_norm), nope_rope_pad(mla_k_norm)]
    return jnp.concatenate(parts).astype(F32).reshape(1, PV_LEN)


def _rope_tables(seq):
    def cos_sin(rot_dim):
        half = rot_dim // 2
        inv_freq = ROPE_THETA ** (-jnp.arange(half, dtype=F32) / half)
        ang = jnp.arange(seq, dtype=F32)[:, None] * inv_freq[None, :]
        return jnp.cos(ang), jnp.sin(ang)

    cos_p, sin_p = cos_sin(PARTIAL_ROT_DIM)
    cos_m, sin_m = cos_sin(MLA_ROPE_DIM)
    hp, hm = PARTIAL_ROT_DIM // 2, MLA_ROPE_DIM // 2
    ones = lambda n: jnp.ones((seq, n), F32)
    zeros = lambda n: jnp.zeros((seq, n), F32)
    rest = DIFF_QK_DIM - PARTIAL_ROT_DIM
    cd = jnp.tile(jnp.concatenate([cos_p, cos_p, ones(rest)], axis=1), (1, 2))
    sd_up = jnp.tile(jnp.concatenate([-sin_p, zeros(hp + rest)], axis=1), (1, 2))
    sd_dn = jnp.tile(jnp.concatenate([zeros(hp), sin_p, zeros(rest)], axis=1), (1, 2))
    pad = LANES - MLA_ROPE_DIM
    cm = jnp.concatenate([cos_m, cos_m, zeros(pad)], axis=1)
    sm_up = jnp.concatenate([-sin_m, zeros(hm + pad)], axis=1)
    sm_dn = jnp.concatenate([zeros(hm), sin_m, zeros(pad)], axis=1)
    return cd, sd_up, sd_dn, cm, sm_up, sm_dn


def _tile(seq, want):
    return min(seq, want)


def kernel(x, norm_mix, w_in, diff_q_norm, diff_k_norm, diff_lambda_q1, diff_lambda_k1,
           diff_lambda_q2, diff_lambda_k2, diff_subln, fox_q_norm, fox_k_norm, fox_forget_bias,
           mla_q_a_norm, mla_kv_a_norm, mla_w_q_up, mla_w_kv_up, mla_q_norm, mla_k_norm,
           w_out, norm_ffn, w_ff_up, w_ff_down):
    batch, seq, d = x.shape
    assert batch == 1 and d == D_MODEL and seq % LANES == 0
    tabs = _rope_tables(seq)
    xs = x.reshape(seq, d)
    tq = _tile(seq, 512)
    assert sum(IN_SECTIONS) == w_in.shape[-1]
    w_in_t = jnp.transpose(w_in, (2, 0, 1))
    _, diff_grid = _attn_grid(N_HEADS_DIFF, seq, tq)
    late_weights = (w_ff_up, w_ff_down, w_out)
    ride = _can_ride(late_weights, diff_grid[0] * diff_grid[1])
    for l in range(DEPTH):
        proj = _in_proj(xs, norm_mix[l].reshape(1, d), w_in_t, l, tm=_tile(seq, 1024))
        pvec = _pack_vec(diff_q_norm[l], diff_k_norm[l], fox_q_norm[l], fox_k_norm[l],
                         fox_forget_bias[l], mla_q_a_norm[l], mla_kv_a_norm[l],
                         mla_q_norm[l], mla_k_norm[l])
        (qd, kd, vd, qf, kf, vf, qm, km, vm) = _prep(
            proj, tabs, pvec, _pack_w_q_up(mla_w_q_up[l]), _pack_w_kv_up(mla_w_kv_up[l]),
            tm=_tile(seq, 256))
        lambda_init = 0.8 - 0.6 * math.exp(-0.3 * l)
        lam = jnp.stack([diff_lambda_q1[l], diff_lambda_k1[l], diff_lambda_q2[l], diff_lambda_k2[l]])
        diff_extras = (lam, diff_subln[l].reshape(1, DIFF_V_DIM))
        if ride:
            o_a, w_up_b, w_down_b, w_out_b = _attention(
                qd, kd, vd, diff_extras, mode="diff", tq=tq, lambda_init=lambda_init,
                cast=late_weights, cast_layer=l)
        else:
            o_a = _attention(qd, kd, vd, diff_extras, mode="diff", tq=tq, lambda_init=lambda_init)
            w_up_b, w_down_b, w_out_b = (w[l].astype(BF16) for w in late_weights)
        o_b = _attention(qf, kf, vf, (), mode="fox", tq=tq)
        o_c = _attention(qm, km, vm, (), mode="mla", tq=tq)
        xs = _out_proj(xs, o_a, o_b, o_c, w_out_b[None], 0, tm=_tile(seq, 512))
        xs = _ffn(xs, norm_ffn[l].reshape(1, d), w_up_b[None], w_down_b[None], 0,
                  tm=_tile(seq, 512), tf=1024)
    return xs.reshape(batch, seq, d)
```

```python
import functools
import math

import jax
import jax.numpy as jnp
from jax import lax
from jax.experimental import pallas as pl
from jax.experimental.pallas import tpu as pltpu

D_MODEL = 2048
DEPTH = 2
N_HEADS_DIFF = 4
DIFF_QK_DIM = 64
DIFF_V_DIM = 128
N_HEADS_FOX = 6
FOX_HEAD_DIM = 128
N_HEADS_MLA = 6
MLA_Q_RANK = 512
MLA_KV_RANK = 256
MLA_NOPE_DIM = 128
MLA_ROPE_DIM = 64
MLA_QK_DIM = MLA_ROPE_DIM + MLA_NOPE_DIM
MLA_V_DIM = 128
D_FF = 4 * D_MODEL
ROPE_THETA = 500000.0
PARTIAL_ROT_DIM = DIFF_QK_DIM // 4
EPS = 1e-6

DIFF_Q_COLS = N_HEADS_DIFF * 2 * DIFF_QK_DIM
DIFF_V_COLS = N_HEADS_DIFF * DIFF_V_DIM
FOX_COLS = N_HEADS_FOX * FOX_HEAD_DIM
MLA_V_COLS = N_HEADS_MLA * MLA_V_DIM
IN_SECTIONS = (DIFF_Q_COLS, DIFF_Q_COLS, DIFF_V_COLS, FOX_COLS, FOX_COLS, FOX_COLS,
               N_HEADS_FOX, MLA_Q_RANK, MLA_KV_RANK + MLA_ROPE_DIM)

LANES = 128
FOX_PAD_DIM = 2 * LANES
DECAY_PIECES = 3
BF16_ROWS = 16
MXU_COLS = 256
UNITS_PER_TRIP = 16
IN_ROW_CHUNKS = 2
FFN_ROW_CHUNKS = 2
TILES_PER_STEP = 4
MLA_PAD_DIM = 2 * LANES
VMEM_LIMIT_BYTES = 56 * 1024 * 1024

OFF_DQ = 0
OFF_DK = OFF_DQ + DIFF_Q_COLS
OFF_DV = OFF_DK + DIFF_Q_COLS
OFF_FQ = OFF_DV + DIFF_V_COLS
OFF_FK = OFF_FQ + FOX_COLS
OFF_FV = OFF_FK + FOX_COLS
MAIN_COLS = OFF_FV + FOX_COLS
TAIL_SRC_COLS = N_HEADS_FOX + MLA_Q_RANK + MLA_KV_RANK + MLA_ROPE_DIM
FF_LANE0 = -TAIL_SRC_COLS % 8
TAIL_MQ = 0
TAIL_CKV = TAIL_MQ + MLA_Q_RANK
TAIL_KR = TAIL_CKV + MLA_KV_RANK
TAIL_FF = TAIL_KR + LANES
TAIL_COLS = TAIL_FF + LANES
IN_TILE = MAIN_COLS // 3
PROJ_COLS = MAIN_COLS + IN_TILE
assert TAIL_COLS <= IN_TILE and IN_TILE % LANES == 0

PV_G_DQ = 0
PV_G_DK = PV_G_DQ + LANES
PV_G_FQ = PV_G_DK + LANES
PV_G_FK = PV_G_FQ + LANES
PV_F_BIAS = PV_G_FK + LANES
PV_G_QA = PV_F_BIAS + LANES
PV_G_KVA = PV_G_QA + MLA_Q_RANK
PV_G_QM = PV_G_KVA + MLA_KV_RANK
PV_G_KM = PV_G_QM + MLA_PAD_DIM
PV_LEN = PV_G_KM + MLA_PAD_DIM

NEG_BIG = -1e30
LOG2E = math.log2(math.e)

BF16 = jnp.bfloat16
F32 = jnp.float32


def _params(*semantics):
    return pltpu.CompilerParams(dimension_semantics=semantics,
                                vmem_limit_bytes=VMEM_LIMIT_BYTES)


def _in_proj_kernel(x_ref, g_ref, w_hbm, o_ref, wf_ref, wb_ref, sem, *, layer):
    j = pl.program_id(0)
    first_row_tile = pl.program_id(1) == 0
    n_main = MAIN_COLS // IN_TILE
    n_src = TAIL_SRC_COLS + FF_LANE0

    def fetch(jj):
        slot = jj % 2
        if jj < n_main:
            return pltpu.make_async_copy(w_hbm.at[pl.ds(jj * IN_TILE, IN_TILE), layer],
                                         wf_ref.at[slot], sem.at[slot])
        return pltpu.make_async_copy(w_hbm.at[pl.ds(MAIN_COLS - FF_LANE0, n_src), layer],
                                     wf_ref.at[slot, pl.ds(0, n_src)], sem.at[slot])

    def advance(jj):
        if jj == 0:
            fetch(0).start()
        fetch(jj).wait()
        if jj < n_main:
            fetch(jj + 1).start()

    for jj in range(n_main):
        @pl.when(jnp.logical_and(first_row_tile, j == jj))
        def _(jj=jj):
            advance(jj)
            wb_ref[...] = wf_ref[jj % 2].astype(BF16)

    @pl.when(jnp.logical_and(first_row_tile, j == n_main))
    def _():
        d = wb_ref.shape[1]
        advance(n_main)
        w = wf_ref[n_main % 2, 0:n_src, :]
        src = FF_LANE0 + N_HEADS_FOX
        wb_ref[TAIL_MQ:TAIL_KR, :] = w[src:src + TAIL_KR].astype(BF16)
        wb_ref[TAIL_KR:TAIL_KR + MLA_ROPE_DIM, :] = w[src + TAIL_KR:src + TAIL_KR + MLA_ROPE_DIM].astype(BF16)
        wb_ref[TAIL_KR + MLA_ROPE_DIM:TAIL_FF, :] = jnp.zeros((LANES - MLA_ROPE_DIM, d), BF16)
        head = w[0:BF16_ROWS]
        row = lax.broadcasted_iota(jnp.int32, head.shape, 0)
        keep = jnp.logical_and(row >= FF_LANE0, row < FF_LANE0 + N_HEADS_FOX)
        wb_ref[TAIL_FF:TAIL_FF + BF16_ROWS, :] = jnp.where(keep, head, 0.0).astype(BF16)
        wb_ref[TAIL_FF + BF16_ROWS:, :] = jnp.zeros((IN_TILE - TAIL_FF - BF16_ROWS, d), BF16)

    rows = x_ref.shape[0]
    chunk = rows // IN_ROW_CHUNKS if rows % (8 * IN_ROW_CHUNKS) == 0 else rows
    for r0 in range(0, rows, chunk):
        x = x_ref[r0:r0 + chunk, :]
        inv = lax.rsqrt(jnp.mean(x * x, axis=-1, keepdims=True) + EPS)
        h = (x * inv * g_ref[...]).astype(BF16)
        o_ref[r0:r0 + chunk, :] = lax.dot_general(h, wb_ref[...], (((1,), (1,)), ((), ())),
                                                  preferred_element_type=F32)


def _in_proj(x, g, w_in_t, layer, *, tm):
    s, d = x.shape
    assert w_in_t.shape[0] == MAIN_COLS + TAIL_SRC_COLS
    return pl.pallas_call(
        functools.partial(_in_proj_kernel, layer=layer),
        grid=(PROJ_COLS // IN_TILE, s // tm),
        in_specs=[pl.BlockSpec((tm, d), lambda j, i: (i, 0)),
                  pl.BlockSpec((1, d), lambda j, i: (0, 0)),
                  pl.BlockSpec(memory_space=pl.ANY)],
        out_specs=pl.BlockSpec((tm, IN_TILE), lambda j, i: (i, j)),
        out_shape=jax.ShapeDtypeStruct((s, PROJ_COLS), F32),
        scratch_shapes=[pltpu.VMEM((2, IN_TILE, d), F32), pltpu.VMEM((IN_TILE, d), BF16),
                        pltpu.SemaphoreType.DMA((2,))],
        compiler_params=_params("arbitrary", "arbitrary"),
        name="in_proj",
    )(x, g, w_in_t)


def _rms(x, n_valid):
    ss = jnp.sum(x * x, axis=-1, keepdims=True)
    return x * lax.rsqrt(ss * (1.0 / n_valid) + EPS)


def _rms_two_halves(x):
    lo = lax.broadcasted_iota(jnp.int32, x.shape, 1) < DIFF_QK_DIM
    x2 = x * x
    s_lo = jnp.sum(jnp.where(lo, x2, 0.0), axis=-1, keepdims=True)
    s_hi = jnp.sum(jnp.where(lo, 0.0, x2), axis=-1, keepdims=True)
    return x * lax.rsqrt(jnp.where(lo, s_lo, s_hi) * (1.0 / DIFF_QK_DIM) + EPS)


def _shift_pair_bf16(half):
    src = lax.broadcasted_iota(jnp.int32, (LANES, 2 * LANES), 0)
    dst = lax.broadcasted_iota(jnp.int32, (LANES, 2 * LANES), 1)
    want = jnp.where(dst < LANES, dst + half, dst - LANES - half)
    return jnp.where(src == jnp.bitwise_and(want, LANES - 1), 1.0, 0.0).astype(BF16)


def _rope(x, c, s_up, s_dn, shift_pair):
    shifted = jnp.dot(x.astype(BF16), shift_pair, preferred_element_type=F32)
    return x * c + shifted[:, :LANES] * s_up + shifted[:, LANES:] * s_dn


def _identity_bf16(n):
    row = lax.broadcasted_iota(jnp.int32, (n, n), 0)
    col = lax.broadcasted_iota(jnp.int32, (n, n), 1)
    return jnp.where(row == col, 1.0, 0.0).astype(BF16)


def _t_bf16(x, eye):
    return lax.dot_general(eye, x.astype(BF16), (((1,), (1,)), ((), ())),
                           preferred_element_type=F32).astype(BF16)


def _prep_kernel(proj_ref, tail_ref, cd_ref, sd_up_ref, sd_dn_ref, cm_ref, sm_up_ref, sm_dn_ref,
                 pv_ref, wq_ref, wkv_ref,
                 qd_ref, kd_ref, vd_ref, qf_ref, kf_ref, vf_ref,
                 qm_ref, km_ref, vm_ref, carry_ref, *, tm):
    @pl.when(pl.program_id(0) == 0)
    def _():
        carry_ref[...] = jnp.zeros_like(carry_ref)

    def pv(off, n):
        return pv_ref[:, off:off + n]

    def lanes(ref, off, n=LANES):
        return ref[:, off:off + n]

    cd, sd_up, sd_dn = cd_ref[...], sd_up_ref[...], sd_dn_ref[...]
    cm, sm_up, sm_dn = cm_ref[...], sm_up_ref[...], sm_dn_ref[...]
    half_d = _shift_pair_bf16(PARTIAL_ROT_DIM // 2)
    half_m = _shift_pair_bf16(MLA_ROPE_DIM // 2)
    eye = _identity_bf16(LANES)

    g_dq, g_dk = pv(PV_G_DQ, LANES), pv(PV_G_DK, LANES)
    for b in range(N_HEADS_DIFF):
        q = _rope(_rms_two_halves(lanes(proj_ref, OFF_DQ + b * LANES)) * g_dq, cd, sd_up, sd_dn, half_d)
        qd_ref[b] = _t_bf16(q * (DIFF_QK_DIM ** -0.5 * LOG2E), eye)
        k = _rope(_rms_two_halves(lanes(proj_ref, OFF_DK + b * LANES)) * g_dk, cd, sd_up, sd_dn, half_d)
        kd_ref[:, b * LANES:(b + 1) * LANES] = k.astype(BF16)
        vd_ref[b] = _t_bf16(lanes(proj_ref, OFF_DV + b * LANES), eye)

    z = lanes(tail_ref, TAIL_FF) + pv(PV_F_BIAS, LANES)
    c = jnp.minimum(z, 0.0) - jnp.log1p(jnp.exp(-jnp.abs(z)))
    row = lax.broadcasted_iota(jnp.int32, c.shape, 0)
    shift = 1
    while shift < tm:
        c = c + jnp.where(row >= shift, pltpu.roll(c, shift, 0), 0.0)
        shift *= 2
    c = c + carry_ref[...]
    carry_ref[...] = c[tm - 1:tm, :]
    c2 = c * LOG2E

    g_fq, g_fk = pv(PV_G_FQ, LANES), pv(PV_G_FK, LANES)
    lane = lax.broadcasted_iota(jnp.int32, (tm, LANES), 1)
    ones_rows = jnp.where(lax.broadcasted_iota(jnp.int32, (LANES, tm), 0) < DECAY_PIECES, 1.0, 0.0)
    for h in range(N_HEADS_FOX):
        base = h * FOX_PAD_DIM
        q = _rms(lanes(proj_ref, OFF_FQ + h * LANES), FOX_HEAD_DIM) * g_fq
        qf_ref[h, 0:LANES, :] = _t_bf16(q * (FOX_HEAD_DIM ** -0.5 * LOG2E), eye)
        qf_ref[h, LANES:, :] = ones_rows.astype(BF16)
        k = _rms(lanes(proj_ref, OFF_FK + h * LANES), FOX_HEAD_DIM) * g_fk
        kf_ref[:, base:base + LANES] = k.astype(BF16)
        rest = -jnp.broadcast_to(c2[:, FF_LANE0 + h:FF_LANE0 + h + 1], (tm, LANES))
        decay = jnp.zeros((tm, LANES), F32)
        for piece in range(DECAY_PIECES):
            part = rest.astype(BF16).astype(F32)
            decay = jnp.where(lane == piece, part, decay)
            rest = rest - part
        kf_ref[:, base + LANES:base + FOX_PAD_DIM] = decay.astype(BF16)
        vf_ref[h] = _t_bf16(lanes(proj_ref, OFF_FV + h * LANES), eye)

    q_lat = (_rms(lanes(tail_ref, TAIL_MQ, MLA_Q_RANK), MLA_Q_RANK) * pv(PV_G_QA, MLA_Q_RANK)).astype(BF16)
    q_up = jnp.dot(q_lat, wq_ref[...], preferred_element_type=F32)
    c_kv = (_rms(lanes(tail_ref, TAIL_CKV, MLA_KV_RANK), MLA_KV_RANK) * pv(PV_G_KVA, MLA_KV_RANK)).astype(BF16)
    kv_up = jnp.dot(c_kv, wkv_ref[...], preferred_element_type=F32)
    g_qn, g_qr = pv(PV_G_QM, LANES), pv(PV_G_QM + LANES, LANES)
    g_kn, g_kr = pv(PV_G_KM, LANES), pv(PV_G_KM + LANES, LANES)
    k_rope = _rope(_rms(lanes(tail_ref, TAIL_KR), MLA_ROPE_DIM) * g_kr, cm, sm_up, sm_dn, half_m).astype(BF16)
    mla_scale = MLA_QK_DIM ** -0.5 * LOG2E
    v_off = N_HEADS_MLA * MLA_NOPE_DIM
    for h in range(N_HEADS_MLA):
        base = h * MLA_PAD_DIM
        q_nope = _rms(q_up[:, base:base + LANES], MLA_NOPE_DIM) * g_qn
        q_rope = _rope(_rms(q_up[:, base + LANES:base + 2 * LANES], MLA_ROPE_DIM) * g_qr,
                       cm, sm_up, sm_dn, half_m)
        qm_ref[h, 0:LANES, :] = _t_bf16(q_nope * mla_scale, eye)
        qm_ref[h, LANES:2 * LANES, :] = _t_bf16(q_rope * mla_scale, eye)
        k_nope = _rms(kv_up[:, h * LANES:(h + 1) * LANES], MLA_NOPE_DIM) * g_kn
        km_ref[:, base:base + LANES] = k_nope.astype(BF16)
        km_ref[:, base + LANES:base + 2 * LANES] = k_rope
        vm_ref[h] = _t_bf16(kv_up[:, v_off + h * LANES:v_off + (h + 1) * LANES], eye)


def _prep(proj, tabs, pvec, wq, wkv, *, tm):
    s = proj.shape[0]
    tail_blk = pl.BlockSpec((tm, IN_TILE), lambda i: (i, MAIN_COLS // IN_TILE))
    row_blk = lambda n: pl.BlockSpec((tm, n), lambda i: (i, 0))
    full = lambda a: pl.BlockSpec(a.shape, lambda i: (0, 0))

    def head_t(n_heads, dim):
        return (jax.ShapeDtypeStruct((n_heads, dim, s), BF16),
                pl.BlockSpec((n_heads, dim, tm), lambda i: (0, 0, i)))

    def rows(n, dtype=BF16):
        return jax.ShapeDtypeStruct((s, n), dtype), row_blk(n)

    outs = [head_t(N_HEADS_DIFF, LANES), rows(DIFF_Q_COLS), head_t(N_HEADS_DIFF, DIFF_V_DIM),
            head_t(N_HEADS_FOX, FOX_PAD_DIM), rows(N_HEADS_FOX * FOX_PAD_DIM),
            head_t(N_HEADS_FOX, FOX_HEAD_DIM),
            head_t(N_HEADS_MLA, MLA_PAD_DIM), rows(N_HEADS_MLA * MLA_PAD_DIM),
            head_t(N_HEADS_MLA, MLA_V_DIM)]
    return pl.pallas_call(
        functools.partial(_prep_kernel, tm=tm),
        grid=(s // tm,),
        in_specs=[row_blk(MAIN_COLS), tail_blk] + [row_blk(LANES)] * len(tabs)
                 + [full(pvec), full(wq), full(wkv)],
        out_specs=[o[1] for o in outs],
        out_shape=[o[0] for o in outs],
        scratch_shapes=[pltpu.VMEM((1, LANES), F32)],
        compiler_params=_params("arbitrary"),
        name="prep",
    )(proj, proj, *tabs, pvec, wq, wkv)


def _attn_kernel(*refs, mode, tq, tiles, lambda_init, n_cast):
    n_in = 5 if mode == "diff" else 3
    q_ref, k_ref, v_ref = refs[:3]
    if mode == "diff":
        lam_ref, sub_ref = refs[3:5]
    cast_src = refs[n_in:n_in + n_cast]
    o_ref = refs[n_in + n_cast]
    cast_dst = refs[n_in + n_cast + 1:n_in + 2 * n_cast + 1]
    m_ref, l_ref, acc_ref, s_ref = refs[n_in + 2 * n_cast + 1:]
    for src, dst in zip(cast_src, cast_dst):
        dst[...] = src[...].astype(dst.dtype)
    n = pl.program_id(1) * tiles
    dv = o_ref.shape[1]
    cw = min(tq, MXU_COLS)
    n_sub = 2 if mode == "diff" else 1
    streams = [(t, sub) for t in range(tiles) for sub in range(n_sub)]

    m_ref[...] = jnp.full(m_ref.shape, NEG_BIG, F32)
    l_ref[...] = jnp.zeros(l_ref.shape, F32)
    acc_ref[...] = jnp.zeros(acc_ref.shape, F32)

    def q_stream(st):
        t, sub = streams[st]
        q_t = q_ref[:, t * tq:(t + 1) * tq]
        if mode == "diff":
            lo = lax.broadcasted_iota(jnp.int32, q_t.shape, 0) < DIFF_QK_DIM
            zero = jnp.zeros_like(q_t)
            q_t = jnp.where(lo, q_t, zero) if sub == 0 else jnp.where(lo, zero, q_t)
        return q_t

    def scores(j, st, buf):
        start = pl.multiple_of(j * tq, tq)
        s_ref[buf] = jnp.dot(k_ref[pl.ds(start, tq), :], q_stream(st), preferred_element_type=F32)

    def softmax_pv(j, st, buf, masked):
        start = pl.multiple_of(j * tq, tq)
        v_t = v_ref[:, pl.ds(start, tq)]
        for c0 in range(0, tq, cw):
            cols = slice(c0, c0 + cw)
            s_t = s_ref[buf, :, cols]
            if masked:
                key = lax.broadcasted_iota(jnp.int32, s_t.shape, 0)
                qry = lax.broadcasted_iota(jnp.int32, s_t.shape, 1) + c0
                s_t = jnp.where(key <= qry, s_t, NEG_BIG)
            m_prev = m_ref[st, :, cols]
            m_next = jnp.maximum(m_prev, jnp.max(s_t, axis=0, keepdims=True))
            alpha = jnp.exp2(m_prev - m_next)
            m_ref[st, :, cols] = m_next
            p_t = jnp.exp2(s_t - m_next)
            l_ref[st, :, cols] = alpha * l_ref[st, :, cols] + jnp.sum(p_t, axis=0, keepdims=True)
            acc_ref[st, :, cols] = (acc_ref[st, :, cols] * alpha
                                    + jnp.dot(v_t, p_t.astype(BF16), preferred_element_type=F32))

    def run(units, following):
        assert len(units) % 2 == 0 or following is None
        seq = units + ([following] if following is not None else [])
        for pos, (j, st, masked) in enumerate(units):
            if pos + 1 < len(seq):
                scores(seq[pos + 1][0], seq[pos + 1][1], (pos + 1) % 2)
            softmax_pv(j, st, pos % 2, masked)

    def full_block(j):
        return [(j, st, False) for st in range(len(streams))]

    blocks_per_trip = max(1, UNITS_PER_TRIP // len(streams))
    assert blocks_per_trip & (blocks_per_trip - 1) == 0

    def trip(t, carry):
        base = t * blocks_per_trip
        units = [u for b in range(blocks_per_trip) for u in full_block(base + b)]
        run(units, (base + blocks_per_trip, 0, False))
        return carry

    tail = [(n + b, st, b == streams[st][0])
            for b in range(tiles) for st in range(len(streams)) if streams[st][0] >= b]

    scores(0, 0, 0)
    lax.fori_loop(0, lax.shift_right_logical(n, blocks_per_trip.bit_length() - 1), trip, 0)
    left = lax.bitwise_and(n, blocks_per_trip - 1)
    for r in range(0, blocks_per_trip, math.gcd(tiles, blocks_per_trip)):
        @pl.when(left == r)
        def _():
            run([u for b in range(r) for u in full_block(n - r + b)] + tail, None)

    def out_t(st):
        return acc_ref[st] * (1.0 / l_ref[st])

    if mode == "diff":
        lp = lam_ref[...]
        lam = (jnp.exp(jnp.sum(lp[0:1] * lp[1:2], axis=1, keepdims=True))
               - jnp.exp(jnp.sum(lp[2:3] * lp[3:4], axis=1, keepdims=True)) + lambda_init)
    for t in range(tiles):
        rows = slice(t * tq, (t + 1) * tq)
        if mode == "diff":
            o = (out_t(2 * t) - lam * out_t(2 * t + 1)).T
            o = o * lax.rsqrt(jnp.mean(o * o, axis=-1, keepdims=True) + EPS) * sub_ref[...]
            o_ref[rows, :] = (o * (1.0 - lambda_init)).astype(o_ref.dtype)
        else:
            o_ref[rows, :] = out_t(t).T.astype(o_ref.dtype)


def _attn_grid(n_heads, s, tq):
    n_tiles = s // tq
    tiles = math.gcd(n_tiles, TILES_PER_STEP)
    return tiles, (n_heads, n_tiles // tiles)


def _can_ride(weights, n_steps):
    return all(w.shape[1] % (n_steps * BF16_ROWS) == 0 for w in weights)


def _attention(q_t, k, v_t, extras, *, mode, tq, lambda_init=0.0, cast=(), cast_layer=0):
    n_heads, dk, s = q_t.shape
    dv = v_t.shape[1]
    tiles, grid = _attn_grid(n_heads, s, tq)
    n_streams = tiles * (2 if mode == "diff" else 1)
    in_specs = [pl.BlockSpec((None, dk, tiles * tq), lambda h, i: (h, 0, i)),
                pl.BlockSpec((s, dk), lambda h, i: (0, h)),
                pl.BlockSpec((None, dv, s), lambda h, i: (h, 0, 0))]
    if mode == "diff":
        lam, sub = extras
        in_specs += [pl.BlockSpec(lam.shape, lambda h, i: (0, 0)),
                     pl.BlockSpec(sub.shape, lambda h, i: (0, 0))]
    out_specs = [pl.BlockSpec((tiles * tq, dv), lambda h, i: (i, h))]
    out_shape = [jax.ShapeDtypeStruct((s, n_heads * dv), BF16)]
    n_steps = grid[0] * grid[1]
    assert _can_ride(cast, n_steps)
    for w in cast:
        rows = w.shape[1] // n_steps
        in_specs.append(pl.BlockSpec((None, rows, w.shape[2]), lambda h, i: (cast_layer, h * grid[1] + i, 0)))
        out_specs.append(pl.BlockSpec((rows, w.shape[2]), lambda h, i: (h * grid[1] + i, 0)))
        out_shape.append(jax.ShapeDtypeStruct(w.shape[1:], BF16))
    outs = pl.pallas_call(
        functools.partial(_attn_kernel, mode=mode, tq=tq, tiles=tiles, lambda_init=lambda_init,
                          n_cast=len(cast)),
        grid=grid,
        in_specs=in_specs,
        out_specs=out_specs,
        out_shape=out_shape,
        scratch_shapes=[pltpu.VMEM((n_streams, 1, tq), F32),
                        pltpu.VMEM((n_streams, 1, tq), F32),
                        pltpu.VMEM((n_streams, dv, tq), F32),
                        pltpu.VMEM((2, tq, tq), F32)],
        compiler_params=_params("arbitrary", "arbitrary"),
        name="attn_" + mode,
    )(q_t, k, v_t, *extras, *cast)
    return outs if cast else outs[0]


def _out_proj_kernel(x_ref, oa_ref, ob_ref, oc_ref, w_ref, o_ref):
    a0, a1 = DIFF_V_COLS, DIFF_V_COLS + FOX_COLS
    acc = x_ref[...]
    acc = acc + jnp.dot(oa_ref[...], w_ref[0:a0, :], preferred_element_type=F32)
    acc = acc + jnp.dot(ob_ref[...], w_ref[a0:a1, :], preferred_element_type=F32)
    acc = acc + jnp.dot(oc_ref[...], w_ref[a1:, :], preferred_element_type=F32)
    o_ref[...] = acc


def _out_proj(x, oa, ob, oc, w, layer, *, tm):
    s, d = x.shape
    row_blk = lambda n: pl.BlockSpec((tm, n), lambda i: (i, 0))
    return pl.pallas_call(
        _out_proj_kernel,
        grid=(s // tm,),
        in_specs=[row_blk(d), row_blk(oa.shape[1]), row_blk(ob.shape[1]), row_blk(oc.shape[1]),
                  pl.BlockSpec((None,) + w.shape[1:], lambda i: (layer, 0, 0))],
        out_specs=row_blk(d),
        out_shape=jax.ShapeDtypeStruct((s, d), F32),
        compiler_params=_params("arbitrary"),
        name="out_proj",
    )(x, oa, ob, oc, w)


def _ffn_kernel(x_ref, g_ref, wu_ref, wd_ref, o_ref, h_ref):
    def mlp(h):
        u = jnp.dot(h, wu_ref[...], preferred_element_type=F32)
        a = jnp.square(jnp.maximum(u, 0.0)).astype(BF16)
        return jnp.dot(a, wd_ref[...], preferred_element_type=F32)

    @pl.when(pl.program_id(1) == 0)
    def _():
        rows = x_ref.shape[0]
        chunk = rows // FFN_ROW_CHUNKS if rows % (BF16_ROWS * FFN_ROW_CHUNKS) == 0 else rows
        for r0 in range(0, rows, chunk):
            x = x_ref[r0:r0 + chunk, :]
            inv = lax.rsqrt(jnp.mean(x * x, axis=-1, keepdims=True) + EPS)
            h = (x * inv * g_ref[...]).astype(BF16)
            h_ref[r0:r0 + chunk, :] = h
            o_ref[r0:r0 + chunk, :] = x + mlp(h)

    @pl.when(pl.program_id(1) > 0)
    def _():
        o_ref[...] += mlp(h_ref[...])


def _ffn(x, g, wu, wd, layer, *, tm, tf):
    s, d = x.shape
    f = wu.shape[2]
    return pl.pallas_call(
        _ffn_kernel,
        grid=(s // tm, f // tf),
        in_specs=[pl.BlockSpec((tm, d), lambda i, j: (i, 0)),
                  pl.BlockSpec((1, d), lambda i, j: (0, 0)),
                  pl.BlockSpec((None, d, tf), lambda i, j: (layer, 0, j)),
                  pl.BlockSpec((None, tf, d), lambda i, j: (layer, j, 0))],
        out_specs=pl.BlockSpec((tm, d), lambda i, j: (i, 0)),
        out_shape=jax.ShapeDtypeStruct((s, d), F32),
        scratch_shapes=[pltpu.VMEM((tm, d), BF16)],
        compiler_params=_params("arbitrary", "arbitrary"),
        name="ffn",
    )(x, g, wu, wd)


def _pack_w_q_up(w):
    w = w.reshape(MLA_Q_RANK, N_HEADS_MLA, MLA_QK_DIM)
    pad = jnp.zeros((MLA_Q_RANK, N_HEADS_MLA, MLA_PAD_DIM - MLA_QK_DIM), w.dtype)
    w = jnp.concatenate([w[:, :, MLA_ROPE_DIM:], w[:, :, :MLA_ROPE_DIM], pad], axis=2)
    return w.reshape(MLA_Q_RANK, N_HEADS_MLA * MLA_PAD_DIM).astype(BF16)


def _pack_w_kv_up(w):
    w = w.reshape(MLA_KV_RANK, N_HEADS_MLA, MLA_NOPE_DIM + MLA_V_DIM)
    k_nope = w[:, :, :MLA_NOPE_DIM].reshape(MLA_KV_RANK, N_HEADS_MLA * MLA_NOPE_DIM)
    v = w[:, :, MLA_NOPE_DIM:].reshape(MLA_KV_RANK, MLA_V_COLS)
    return jnp.concatenate([k_nope, v], axis=1).astype(BF16)


def _pack_vec(diff_q_norm, diff_k_norm, fox_q_norm, fox_k_norm, fox_forget_bias,
              mla_q_a_norm, mla_kv_a_norm, mla_q_norm, mla_k_norm):
    def nope_rope_pad(g):
        return jnp.concatenate([g[MLA_ROPE_DIM:], g[:MLA_ROPE_DIM],
                                jnp.zeros((MLA_PAD_DIM - MLA_QK_DIM,), g.dtype)])
    parts = [jnp.tile(diff_q_norm, 2), jnp.tile(diff_k_norm, 2), fox_q_norm, fox_k_norm,
             jnp.pad(fox_forget_bias, (FF_LANE0, LANES - N_HEADS_FOX - FF_LANE0)),
             mla_q_a_norm, mla_kv_a_norm, nope_rope_pad(mla_q_norm), nope_rope_pad(mla_k_norm)]
    return jnp.concatenate(parts).astype(F32).reshape(1, PV_LEN)


def _rope_tables(seq):
    def cos_sin(rot_dim):
        half = rot_dim // 2
        inv_freq = ROPE_THETA ** (-jnp.arange(half, dtype=F32) / half)
        ang = jnp.arange(seq, dtype=F32)[:, None] * inv_freq[None, :]
        return jnp.cos(ang), jnp.sin(ang)

    cos_p, sin_p = cos_sin(PARTIAL_ROT_DIM)
    cos_m, sin_m = cos_sin(MLA_ROPE_DIM)
    hp, hm = PARTIAL_ROT_DIM // 2, MLA_ROPE_DIM // 2
    ones = lambda n: jnp.ones((seq, n), F32)
    zeros = lambda n: jnp.zeros((seq, n), F32)
    rest = DIFF_QK_DIM - PARTIAL_ROT_DIM
    cd = jnp.tile(jnp.concatenate([cos_p, cos_p, ones(rest)], axis=1), (1, 2))
    sd_up = jnp.tile(jnp.concatenate([-sin_p, zeros(hp + rest)], axis=1), (1, 2))
    sd_dn = jnp.tile(jnp.concatenate([zeros(hp), sin_p, zeros(rest)], axis=1), (1, 2))
    pad = LANES - MLA_ROPE_DIM
    cm = jnp.concatenate([cos_m, cos_m, zeros(pad)], axis=1)
    sm_up = jnp.concatenate([-sin_m, zeros(hm + pad)], axis=1)
    sm_dn = jnp.concatenate([zeros(hm), sin_m, zeros(pad)], axis=1)
    return cd, sd_up, sd_dn, cm, sm_up, sm_dn


def _tile(seq, want):
    return min(seq, want)


def kernel(x, norm_mix, w_in, diff_q_norm, diff_k_norm, diff_lambda_q1, diff_lambda_k1,
           diff_lambda_q2, diff_lambda_k2, diff_subln, fox_q_norm, fox_k_norm, fox_forget_bias,
           mla_q_a_norm, mla_kv_a_norm, mla_w_q_up, mla_w_kv_up, mla_q_norm, mla_k_norm,
           w_out, norm_ffn, w_ff_up, w_ff_down):
    batch, seq, d = x.shape
    assert batch == 1 and d == D_MODEL and seq % LANES == 0
    tabs = _rope_tables(seq)
    xs = x.reshape(seq, d)
    tq = _tile(seq, 512)
    assert sum(IN_SECTIONS) == w_in.shape[-1]
    w_in_t = jnp.transpose(w_in, (2, 0, 1))
    _, diff_grid = _attn_grid(N_HEADS_DIFF, seq, tq)
    late_weights = (w_ff_up, w_ff_down, w_out)
    ride = _can_ride(late_weights, diff_grid[0] * diff_grid[1])
    for l in range(DEPTH):
        proj = _in_proj(xs, norm_mix[l].reshape(1, d), w_in_t, l, tm=_tile(seq, 512))
        pvec = _pack_vec(diff_q_norm[l], diff_k_norm[l], fox_q_norm[l], fox_k_norm[l],
                         fox_forget_bias[l], mla_q_a_norm[l], mla_kv_a_norm[l],
                         mla_q_norm[l], mla_k_norm[l])
        (qd, kd, vd, qf, kf, vf, qm, km, vm) = _prep(
            proj, tabs, pvec, _pack_w_q_up(mla_w_q_up[l]), _pack_w_kv_up(mla_w_kv_up[l]),
            tm=_tile(seq, 256))
        lambda_init = 0.8 - 0.6 * math.exp(-0.3 * l)
        lam = jnp.stack([diff_lambda_q1[l], diff_lambda_k1[l], diff_lambda_q2[l], diff_lambda_k2[l]])
        diff_extras = (lam, diff_subln[l].reshape(1, DIFF_V_DIM))
        if ride:
            o_a, w_up_b, w_down_b, w_out_b = _attention(
                qd, kd, vd, diff_extras, mode="diff", tq=tq, lambda_init=lambda_init,
                cast=late_weights, cast_layer=l)
        else:
            o_a = _attention(qd, kd, vd, diff_extras, mode="diff", tq=tq, lambda_init=lambda_init)
            w_up_b, w_down_b, w_out_b = (w[l].astype(BF16) for w in late_weights)
        o_b = _attention(qf, kf, vf, (), mode="fox", tq=tq)
        o_c = _attention(qm, km, vm, (), mode="mla", tq=tq)
        xs = _out_proj(xs, o_a, o_b, o_c, w_out_b[None], 0, tm=_tile(seq, 512))
        xs = _ffn(xs, norm_ffn[l].reshape(1, d), w_up_b[None], w_down_b[None], 0,
                  tm=_tile(seq, 512), tf=1024)
    return xs.reshape(batch, seq, d)
```
